```python
import jax, jax.numpy as jnp
from jax import lax
import numpy as np

D_MODEL = 2048
BATCH = 8
SEQ = 4096
DEPTH = 4

N_A = DEPTH // 2
N_B = DEPTH - N_A
MIX_W = D_MODEL
MEM_HEADS = 4
MEM_HEAD_DIM = 128
MEM_W = MEM_HEADS * MEM_HEAD_DIM
CHUNK = 128
G_HEADS = 12
G_DIM = 128
G_W = G_HEADS * G_DIM
MLA_HEADS = 12
NOPE_DIM = 128
ROPE_DIM = 64
V_DIM = 128
Q_RANK = 512
KV_RANK = 512
Q_BLOCK = 128
ROPE_THETA = 10000.0
D_FF = 5632
CONV_W = 3
EPS = 1e-6

kernel_name = "yoco_gmlp_mla_memxattn_convffn"


def rmsnorm(x, g):
    x32 = x.astype(jnp.float32)
    y = x32 * lax.rsqrt(jnp.mean(x32 * x32, axis=-1, keepdims=True) + EPS)
    return (y * g.astype(jnp.float32)).astype(x.dtype)


def rope_tables(positions, dtype):
    inv = 1.0 / (ROPE_THETA ** (jnp.arange(0, ROPE_DIM, 2, dtype=jnp.float32) / ROPE_DIM))
    ang = positions.astype(jnp.float32)[..., None] * inv
    return jnp.cos(ang).astype(dtype), jnp.sin(ang).astype(dtype)


def apply_rope(x, cos, sin):
    x1, x2 = jnp.split(x, 2, axis=-1)
    return jnp.concatenate([x1 * cos - x2 * sin, x2 * cos + x1 * sin], axis=-1)


def spatial_gating(z_u, z_v, g_v, w_sp, b_sp):
    B, S, _ = z_u.shape
    u = jax.nn.gelu(z_u, approximate=False)
    v = rmsnorm(jax.nn.gelu(z_v, approximate=False), g_v)
    vb = v.reshape(B, S // CHUNK, CHUNK, G_HEADS, G_DIM)
    w = w_sp * jnp.tril(jnp.ones((CHUNK, CHUNK), w_sp.dtype))
    sv = jnp.einsum('gts,bnsgc->bntgc', w, vb) + b_sp.T[None, None, :, :, None]
    return u * sv.reshape(B, S, G_W)


def mla_attention(q_lat, g_q_lat, w_uq, w_uk, w_uv, c_kv, k_rope, cos, sin):
    B, S, _ = q_lat.shape
    q = (rmsnorm(q_lat, g_q_lat) @ w_uq).reshape(B, S, MLA_HEADS, NOPE_DIM + ROPE_DIM)
    q_nope = q[..., :NOPE_DIM]
    q_rope = apply_rope(q[..., NOPE_DIM:], cos[:, :, None, :], sin[:, :, None, :])
    n_blk = S // Q_BLOCK
    scale = (NOPE_DIM + ROPE_DIM) ** -0.5
    k_pos = jnp.arange(S)

    def to_blocks(t):
        return jnp.moveaxis(t.reshape(B, n_blk, Q_BLOCK, *t.shape[2:]), 1, 0)

    def block(args):
        qn, qr, i = args
        qa = jnp.einsum('bqhn,rhn->bqhr', qn, w_uk)
        s = (jnp.einsum('bqhr,bkr->bhqk', qa, c_kv)
             + jnp.einsum('bqhp,bkp->bhqk', qr, k_rope)).astype(jnp.float32) * scale
        q_pos = i * Q_BLOCK + jnp.arange(Q_BLOCK)
        s = jnp.where(k_pos[None, :] <= q_pos[:, None], s, -jnp.inf)
        p = jax.nn.softmax(s, axis=-1).astype(c_kv.dtype)
        o_lat = jnp.einsum('bhqk,bkr->bqhr', p, c_kv)
        return jnp.einsum('bqhr,rhv->bqhv', o_lat, w_uv)

    o = lax.map(block, (to_blocks(q_nope), to_blocks(q_rope), jnp.arange(n_blk)))
    return jnp.moveaxis(o, 0, 1).reshape(B, S, MLA_HEADS * V_DIM)


def memory_attention(q_m, mem, g_mem, w_mem_kv):
    B, S, _ = q_m.shape
    M = mem.shape[1]
    q = q_m.reshape(B, S, MEM_HEADS, MEM_HEAD_DIM)
    kv = rmsnorm(mem, g_mem) @ w_mem_kv
    k = kv[..., :MEM_W].reshape(B, M, MEM_HEADS, MEM_HEAD_DIM)
    v = kv[..., MEM_W:].reshape(B, M, MEM_HEADS, MEM_HEAD_DIM)
    s = jnp.einsum('bshd,bmhd->bhsm', q, k).astype(jnp.float32) * (MEM_HEAD_DIM ** -0.5)
    p = jax.nn.softmax(s, axis=-1).astype(v.dtype)
    return jnp.einsum('bhsm,bmhd->bshd', p, v).reshape(B, S, MEM_W)


def conv_ffn(h, w_up, cw, cb, w_down):
    S = h.shape[1]
    a = h @ w_up
    ap = jnp.pad(a, ((0, 0), (CONV_W - 1, 0), (0, 0)))
    c = sum(ap[:, k:k + S] * cw[k] for k in range(CONV_W)) + cb
    gate, val = c[..., :D_FF], c[..., D_FF:]
    return (jax.nn.silu(gate) * val) @ w_down


def setup_inputs(seed: int = 0) -> dict:
    key = jax.random.key(seed)
    ks = jax.random.split(key, 26)

    def nrm(k, shape, scale):
        return jax.random.normal(k, shape, jnp.float32) * scale

    def gain(k, shape):
        return 1.0 + 0.05 * jax.random.normal(k, shape, jnp.float32)

    D = D_MODEL
    return {
        "x": nrm(ks[0], (BATCH, SEQ, D), 1.0),
        "mem": nrm(ks[1], (BATCH, 256, D), 1.0),
        "positions": (jnp.arange(SEQ, dtype=jnp.int32)[None, :]
                      + jax.random.randint(ks[2], (BATCH, 1), 0, 1024, dtype=jnp.int32)),
        "g_mix": gain(ks[3], (DEPTH, D)),
        "g_ffn": gain(ks[4], (DEPTH, D)),
        "g_final": gain(ks[5], (D,)),
        "w_in_a": nrm(ks[6], (N_A, D, 2 * G_W + MEM_W), D ** -0.5),
        "g_v": gain(ks[7], (N_A, G_W)),
        "w_sp": nrm(ks[8], (N_A, G_HEADS, CHUNK, CHUNK), CHUNK ** -0.5),
        "b_sp": 1.0 + 0.1 * jax.random.normal(ks[9], (N_A, G_HEADS, CHUNK), jnp.float32),
        "g_kv": gain(ks[10], (D,)),
        "w_kv_a": nrm(ks[11], (D, KV_RANK + ROPE_DIM), D ** -0.5),
        "g_kv_lat": gain(ks[12], (KV_RANK,)),
        "w_in_b": nrm(ks[13], (N_B, D, Q_RANK + MEM_W), D ** -0.5),
        "g_q_lat": gain(ks[14], (N_B, Q_RANK)),
        "w_uq": nrm(ks[15], (N_B, Q_RANK, MLA_HEADS * (NOPE_DIM + ROPE_DIM)), Q_RANK ** -0.5),
        "w_uk": nrm(ks[16], (N_B, KV_RANK, MLA_HEADS, NOPE_DIM), KV_RANK ** -0.5),
        "w_uv": nrm(ks[17], (N_B, KV_RANK, MLA_HEADS, V_DIM), KV_RANK ** -0.5),
        "g_mem": gain(ks[18], (DEPTH, D)),
        "w_mem_kv": nrm(ks[19], (DEPTH, D, 2 * MEM_W), D ** -0.5),
        "w_out": nrm(ks[20], (DEPTH, MIX_W, D), MIX_W ** -0.5),
        "w_ffn_up": nrm(ks[21], (DEPTH, D, 2 * D_FF), D ** -0.5),
        "conv_w": nrm(ks[22], (DEPTH, CONV_W, 2 * D_FF), CONV_W ** -0.5),
        "conv_b": nrm(ks[23], (DEPTH, 2 * D_FF), 0.01),
        "w_ffn_down": nrm(ks[24], (DEPTH, D_FF, D), D_FF ** -0.5),
    }


def reference(x, mem, positions, g_mix, g_ffn, g_final, w_in_a, g_v, w_sp, b_sp,
              g_kv, w_kv_a, g_kv_lat, w_in_b, g_q_lat, w_uq, w_uk, w_uv,
              g_mem, w_mem_kv, w_out, w_ffn_up, conv_w, conv_b, w_ffn_down):
    cos, sin = rope_tables(positions, x.dtype)
    c_kv = None
    k_rope = None
    for l in range(DEPTH):
        if l == N_A:
            kv = rmsnorm(x, g_kv) @ w_kv_a
            c_kv = rmsnorm(kv[..., :KV_RANK], g_kv_lat)
            k_rope = apply_rope(kv[..., KV_RANK:], cos, sin)
        h = rmsnorm(x, g_mix[l])
        if l < N_A:
            z = h @ w_in_a[l]
            main = spatial_gating(z[..., :G_W], z[..., G_W:2 * G_W], g_v[l], w_sp[l], b_sp[l])
            q_m = z[..., 2 * G_W:]
        else:
            j = l - N_A
            z = h @ w_in_b[j]
            main = mla_attention(z[..., :Q_RANK], g_q_lat[j], w_uq[j], w_uk[j], w_uv[j],
                                 c_kv, k_rope, cos, sin)
            q_m = z[..., Q_RANK:]
        mo = memory_attention(q_m, mem, g_mem[l], w_mem_kv[l])
        x = x + jnp.concatenate([main, mo], axis=-1) @ w_out[l]
        x = x + conv_ffn(rmsnorm(x, g_ffn[l]), w_ffn_up[l], conv_w[l], conv_b[l], w_ffn_down[l])
    return rmsnorm(x, g_final)
```

```python
import functools

import jax
import jax.numpy as jnp
import numpy as np
from jax import lax
from jax.experimental import pallas as pl
from jax.experimental.pallas import tpu as pltpu

EPS = 1e-6
ROPE_THETA = 10000.0
MEM_HEADS = 4
CHUNK = 128
LANES = 128
SUBLANES = 8
CONV_W = 3
VMEM_LIMIT = 56 * 1024 * 1024

F32 = jnp.float32
BF16 = jnp.bfloat16


def _params(semantics, vmem=VMEM_LIMIT):
    return pltpu.CompilerParams(dimension_semantics=semantics, vmem_limit_bytes=vmem)


def _rms(x, g):
    ms = jnp.mean(x * x, axis=-1, keepdims=True)
    return x * lax.rsqrt(ms + EPS) * g


def _gelu(x):
    return 0.5 * x * (1.0 + lax.erf(x * np.float32(np.sqrt(0.5))))


def _rope_table_kernel(pos_ref, inv_ref, cos_ref, sin_ref):
    ang = pos_ref[...].astype(F32) * inv_ref[...]
    cos_ref[...] = jnp.cos(ang)
    sin_ref[...] = jnp.sin(ang)


def rope_tables(pos_col, inv_row, tm=2048):
    T = pos_col.shape[0]
    return pl.pallas_call(
        _rope_table_kernel,
        out_shape=(jax.ShapeDtypeStruct((T, LANES), F32),) * 2,
        grid=(T // tm,),
        in_specs=[pl.BlockSpec((tm, 1), lambda i: (i, 0)),
                  pl.BlockSpec((1, LANES), lambda i: (0, 0))],
        out_specs=(pl.BlockSpec((tm, LANES), lambda i: (i, 0)),) * 2,
        compiler_params=_params(("arbitrary",)),
        name="rope_tables",
    )(pos_col, inv_row)


def _norm_matmul_kernel(x_ref, g_ref, w_ref, o_ref, h_ref):
    @pl.when(pl.program_id(1) == 0)
    def _():
        h_ref[...] = _rms(x_ref[...], g_ref[...]).astype(h_ref.dtype)

    o_ref[...] = jnp.dot(h_ref[...], w_ref[...],
                         preferred_element_type=F32).astype(o_ref.dtype)


def norm_matmul(x, g, w, out_dtype, tm, tn, name):
    T, K = x.shape
    N = w.shape[1]
    return pl.pallas_call(
        _norm_matmul_kernel,
        out_shape=jax.ShapeDtypeStruct((T, N), out_dtype),
        grid=(T // tm, N // tn),
        in_specs=[pl.BlockSpec((tm, K), lambda i, j: (i, 0)),
                  pl.BlockSpec((1, K), lambda i, j: (0, 0)),
                  pl.BlockSpec((K, tn), lambda i, j: (0, j))],
        out_specs=pl.BlockSpec((tm, tn), lambda i, j: (i, j)),
        scratch_shapes=[pltpu.VMEM((tm, K), BF16)],
        compiler_params=_params(("arbitrary", "arbitrary")),
        name=name,
    )(x, g.reshape(1, K), w)


def _kv_post_kernel(kv_ref, g_ref, cos_ref, sin_ref, ckv_ref, kr_ref, *, rank):
    lat = kv_ref[:, :rank]
    ckv_ref[...] = _rms(lat, g_ref[...]).astype(ckv_ref.dtype)
    kr = kv_ref[:, rank:rank + LANES] * cos_ref[...]
    kr = kr + kv_ref[:, rank + LANES:] * sin_ref[...]
    kr_ref[...] = kr.astype(kr_ref.dtype)


def kv_post(kv_raw, g_lat, cos_t, sin_t, tm=1024):
    T, N = kv_raw.shape
    rank = N - 2 * LANES
    return pl.pallas_call(
        functools.partial(_kv_post_kernel, rank=rank),
        out_shape=(jax.ShapeDtypeStruct((T, rank), BF16),
                   jax.ShapeDtypeStruct((T, LANES), BF16)),
        grid=(T // tm,),
        in_specs=[pl.BlockSpec((tm, N), lambda i: (i, 0)),
                  pl.BlockSpec((1, rank), lambda i: (0, 0)),
                  pl.BlockSpec((tm, LANES), lambda i: (i, 0)),
                  pl.BlockSpec((tm, LANES), lambda i: (i, 0))],
        out_specs=(pl.BlockSpec((tm, rank), lambda i: (i, 0)),
                   pl.BlockSpec((tm, LANES), lambda i: (i, 0))),
        compiler_params=_params(("arbitrary",)),
        name="kv_post",
    )(kv_raw, g_lat.reshape(1, rank), cos_t, sin_t)


def _gmlp_kernel(u_ref, v_ref, gv_ref, w_ref, b_ref, o_ref, *, n_groups, tm):
    row = lax.broadcasted_iota(jnp.int32, (CHUNK, CHUNK), 0)
    col = lax.broadcasted_iota(jnp.int32, (CHUNK, CHUNK), 1)
    causal = row >= col
    w_tril = [jnp.where(causal, w_ref[g], 0.0).astype(BF16) for g in range(n_groups)]
    for c in range(tm // CHUNK):
        rows = slice(c * CHUNK, (c + 1) * CHUNK)
        ug = _gelu(u_ref[rows, :].astype(F32))
        vn = _rms(_gelu(v_ref[rows, :].astype(F32)), gv_ref[...]).astype(BF16)
        for g in range(n_groups):
            cols = slice(g * CHUNK, (g + 1) * CHUNK)
            sv = jnp.dot(w_tril[g], vn[:, cols], preferred_element_type=F32) + b_ref[g]
            o_ref[rows, cols] = (ug[:, cols] * sv).astype(o_ref.dtype)


def gmlp_gate(z, g_v, w_sp, b_full, tm=512):
    T = z.shape[0]
    n_groups = w_sp.shape[0]
    gw = n_groups * CHUNK
    return pl.pallas_call(
        functools.partial(_gmlp_kernel, n_groups=n_groups, tm=tm),
        out_shape=jax.ShapeDtypeStruct((T, gw), BF16),
        grid=(T // tm,),
        in_specs=[pl.BlockSpec((tm, gw), lambda i: (i, 0)),
                  pl.BlockSpec((tm, gw), lambda i: (i, 1)),
                  pl.BlockSpec((1, gw), lambda i: (0, 0)),
                  pl.BlockSpec((n_groups, CHUNK, CHUNK), lambda i: (0, 0, 0)),
                  pl.BlockSpec((n_groups, CHUNK, CHUNK), lambda i: (0, 0, 0))],
        out_specs=pl.BlockSpec((tm, gw), lambda i: (i, 0)),
        compiler_params=_params(("arbitrary",)),
        name="gmlp_gate",
    )(z, z, g_v.reshape(1, gw), w_sp, b_full)


def _mem_attn_kernel(q_ref, kv_ref, o_ref, *, n_heads, head_dim):
    width = n_heads * head_dim
    scale = np.float32(head_dim ** -0.5)
    for h in range(n_heads):
        cols = slice(h * head_dim, (h + 1) * head_dim)
        q = q_ref[:, cols]
        k = kv_ref[:, cols]
        v = kv_ref[:, width + h * head_dim:width + (h + 1) * head_dim]
        s = lax.dot_general(q, k, (((1,), (1,)), ((), ())),
                            preferred_element_type=F32) * scale
        m = jnp.max(s, axis=-1, keepdims=True)
        p = jnp.exp(s - m)
        l = jnp.sum(p, axis=-1, keepdims=True)
        o = jnp.dot(p.astype(BF16), v, preferred_element_type=F32) / l
        o_ref[:, cols] = o.astype(o_ref.dtype)


def mem_attn(z, q_col_block, kvm, batch, tm=1024):
    T = z.shape[0]
    n_mem = kvm.shape[0] // batch
    width = kvm.shape[1] // 2
    per_b = T // batch // tm
    return pl.pallas_call(
        functools.partial(_mem_attn_kernel, n_heads=MEM_HEADS, head_dim=width // MEM_HEADS),
        out_shape=jax.ShapeDtypeStruct((T, width), BF16),
        grid=(batch, per_b),
        in_specs=[pl.BlockSpec((tm, width), lambda b, i: (b * per_b + i, q_col_block)),
                  pl.BlockSpec((n_mem, 2 * width), lambda b, i: (b, 0))],
        out_specs=pl.BlockSpec((tm, width), lambda b, i: (b * per_b + i, 0)),
        compiler_params=_params(("arbitrary", "arbitrary")),
        name="mem_attn",
    )(z, kvm)


def _mix_out_kernel(x_ref, a_ref, b_ref, w_ref, o_ref):
    ka = a_ref.shape[1]
    acc = jnp.dot(a_ref[...], w_ref[:ka, :], preferred_element_type=F32)
    acc = acc + jnp.dot(b_ref[...], w_ref[ka:, :], preferred_element_type=F32)
    o_ref[...] = x_ref[...] + acc


def mix_out(x, main, mo, w_out, tm=512):
    T, D = x.shape
    ka, kb = main.shape[1], mo.shape[1]
    return pl.pallas_call(
        _mix_out_kernel,
        out_shape=jax.ShapeDtypeStruct((T, D), F32),
        grid=(T // tm,),
        in_specs=[pl.BlockSpec((tm, D), lambda i: (i, 0)),
                  pl.BlockSpec((tm, ka), lambda i: (i, 0)),
                  pl.BlockSpec((tm, kb), lambda i: (i, 0)),
                  pl.BlockSpec((ka + kb, D), lambda i: (0, 0))],
        out_specs=pl.BlockSpec((tm, D), lambda i: (i, 0)),
        compiler_params=_params(("arbitrary",)),
        name="mix_out",
    )(x, main, mo, w_out)


def _conv_ffn_kernel(x_ref, g_ref, wg_ref, wv_ref, cwg_ref, cwv_ref, cbg_ref, cbv_ref,
                     wd_ref, o_ref, h_ref, ag_ref, av_ref, carry_g_ref, carry_v_ref,
                     *, tm, tiles_per_seq):
    i = pl.program_id(0)
    j = pl.program_id(1)
    first_in_seq = (i % tiles_per_seq) == 0

    @pl.when(j == 0)
    def _():
        x = x_ref[...]
        h_ref[...] = _rms(x, g_ref[...]).astype(h_ref.dtype)
        o_ref[...] = x

    def conv_branch(w_ref, cw_ref, cb_ref, a_ref, carry_ref):
        a_ref[SUBLANES:, :] = jnp.dot(h_ref[...], w_ref[...], preferred_element_type=F32)

        @pl.when(first_in_seq)
        def _():
            a_ref[:SUBLANES, :] = jnp.zeros((SUBLANES, a_ref.shape[1]), F32)

        @pl.when(jnp.logical_not(first_in_seq))
        def _():
            a_ref[:SUBLANES, :] = carry_ref[j]

        carry_ref[j] = a_ref[tm:, :]
        c = cb_ref[...] + a_ref[SUBLANES:, :] * cw_ref[CONV_W - 1:CONV_W, :]
        for k in range(CONV_W - 1):
            lag = CONV_W - 1 - k
            c = c + a_ref[SUBLANES - lag:SUBLANES - lag + tm, :] * cw_ref[k:k + 1, :]
        return c

    cg = conv_branch(wg_ref, cwg_ref, cbg_ref, ag_ref, carry_g_ref)
    cv = conv_branch(wv_ref, cwv_ref, cbv_ref, av_ref, carry_v_ref)
    gated = (cg * jax.nn.sigmoid(cg) * cv).astype(BF16)
    o_ref[...] += jnp.dot(gated, wd_ref[...], preferred_element_type=F32)


def conv_ffn(x, g, w_up, conv_w, conv_b, w_down, seq, tm=512, tf=512):
    T, D = x.shape
    d_ff = w_down.shape[0]
    nff = d_ff // tf
    kern = functools.partial(_conv_ffn_kernel, tm=tm, tiles_per_seq=seq // tm)
    return pl.pallas_call(
        kern,
        out_shape=jax.ShapeDtypeStruct((T, D), F32),
        grid=(T // tm, nff),
        in_specs=[pl.BlockSpec((tm, D), lambda i, j: (i, 0)),
                  pl.BlockSpec((1, D), lambda i, j: (0, 0)),
                  pl.BlockSpec((D, tf), lambda i, j: (0, j)),
                  pl.BlockSpec((D, tf), lambda i, j: (0, nff + j)),
                  pl.BlockSpec((CONV_W, tf), lambda i, j: (0, j)),
                  pl.BlockSpec((CONV_W, tf), lambda i, j: (0, nff + j)),
                  pl.BlockSpec((1, tf), lambda i, j: (0, j)),
                  pl.BlockSpec((1, tf), lambda i, j: (0, nff + j)),
                  pl.BlockSpec((tf, D), lambda i, j: (j, 0))],
        out_specs=pl.BlockSpec((tm, D), lambda i, j: (i, 0)),
        scratch_shapes=[pltpu.VMEM((tm, D), BF16),
                        pltpu.VMEM((tm + SUBLANES, tf), F32),
                        pltpu.VMEM((tm + SUBLANES, tf), F32),
                        pltpu.VMEM((nff, SUBLANES, tf), F32),
                        pltpu.VMEM((nff, SUBLANES, tf), F32)],
        compiler_params=_params(("arbitrary", "arbitrary")),
        name="conv_ffn",
    )(x, g.reshape(1, D), w_up, w_up, conv_w, conv_w,
      conv_b.reshape(1, -1), conv_b.reshape(1, -1), w_down)


def _q_prep_kernel(q_ref, g_ref, wm_ref, wr_ref, cos_ref, sin_ref, o_ref, *, n_heads, scale):
    qn = _rms(q_ref[...].astype(F32), g_ref[...]).astype(BF16)
    a = jnp.dot(qn, wm_ref[...], preferred_element_type=F32)
    r = jnp.dot(qn, wr_ref[...], preferred_element_type=F32)
    cos = cos_ref[...]
    sin = sin_ref[...]
    for h in range(n_heads):
        base = 2 * LANES * h
        o_ref[0, h, :, :LANES] = (a[:, base:base + LANES] * scale).astype(o_ref.dtype)
        rope = a[:, base + LANES:base + 2 * LANES] * cos + r[:, h * LANES:(h + 1) * LANES] * sin
        o_ref[0, h, :, LANES:] = (rope * scale).astype(o_ref.dtype)


def q_prep(z, g_q, w_main, w_rot, cos_t, sin_t, batch, n_heads, scale, tm=512):
    T = z.shape[0]
    seq = T // batch
    rank = w_main.shape[0]
    per_b = seq // tm
    kern = functools.partial(_q_prep_kernel, n_heads=n_heads, scale=np.float32(scale))
    return pl.pallas_call(
        kern,
        out_shape=jax.ShapeDtypeStruct((batch, n_heads, seq, 2 * LANES), BF16),
        grid=(batch, per_b),
        in_specs=[pl.BlockSpec((tm, rank), lambda b, i: (b * per_b + i, 0)),
                  pl.BlockSpec((1, rank), lambda b, i: (0, 0)),
                  pl.BlockSpec(w_main.shape, lambda b, i: (0, 0)),
                  pl.BlockSpec(w_rot.shape, lambda b, i: (0, 0)),
                  pl.BlockSpec((tm, LANES), lambda b, i: (b * per_b + i, 0)),
                  pl.BlockSpec((tm, LANES), lambda b, i: (b * per_b + i, 0))],
        out_specs=pl.BlockSpec((1, n_heads, tm, 2 * LANES), lambda b, i: (b, 0, i, 0)),
        compiler_params=_params(("arbitrary", "arbitrary")),
        name="q_prep",
    )(z, g_q.reshape(1, rank), w_main, w_rot, cos_t, sin_t)


def _mla_attn_kernel(q_ref, ckv_ref, kr_ref, wkv_ref, o_ref, k_ref, v_ref, *, tq, tk):
    i = pl.program_id(2)

    @pl.when(i == 0)
    def _():
        kv = jnp.dot(ckv_ref[...], wkv_ref[0], preferred_element_type=F32)
        k_ref[:, :LANES] = kv[:, :LANES].astype(k_ref.dtype)
        k_ref[:, LANES:] = kr_ref[...]
        v_ref[...] = kv[:, LANES:].astype(v_ref.dtype)

    q = q_ref[0, 0]
    blocks_per_q = tq // tk

    def step(start, carry, masked_block):
        m, l, acc = carry
        k = k_ref[pl.ds(start, tk), :]
        v = v_ref[pl.ds(start, tk), :]
        s = lax.dot_general(q, k, (((1,), (1,)), ((), ())), preferred_element_type=F32)
        if masked_block is not None:
            q_pos = lax.broadcasted_iota(jnp.int32, (tq, tk), 0)
            k_pos = lax.broadcasted_iota(jnp.int32, (tq, tk), 1) + masked_block * tk
            s = jnp.where(k_pos <= q_pos, s, -jnp.inf)
        m_new = jnp.maximum(m, jnp.max(s, axis=-1, keepdims=True))
        alpha = jnp.exp(m - m_new)
        p = jnp.exp(s - m_new)
        l = alpha * l + jnp.sum(p, axis=-1, keepdims=True)
        acc = alpha * acc + jnp.dot(p.astype(BF16), v, preferred_element_type=F32)
        return m_new, l, acc

    init = (jnp.full((tq, 1), -jnp.inf, F32), jnp.zeros((tq, 1), F32),
            jnp.zeros((tq, v_ref.shape[1]), F32))
    carry = lax.fori_loop(
        0, i * blocks_per_q,
        lambda j, c: step(pl.multiple_of(j * tk, tk), c, None), init)
    for d in range(blocks_per_q):
        carry = step(pl.multiple_of(i * tq + d * tk, tk), carry, d)
    _, l, acc = carry
    o_ref[...] = (acc / l).astype(o_ref.dtype)


def mla_attn(q, c_kv, kr_pad, w_kv, tq=512, tk=512):
    batch, n_heads, seq, qk_dim = q.shape
    rank = c_kv.shape[1]
    v_dim = w_kv.shape[2] - LANES
    nq = seq // tq
    return pl.pallas_call(
        functools.partial(_mla_attn_kernel, tq=tq, tk=tk),
        out_shape=jax.ShapeDtypeStruct((batch * seq, n_heads * v_dim), BF16),
        grid=(batch, n_heads, nq),
        in_specs=[pl.BlockSpec((1, 1, tq, qk_dim), lambda b, h, i: (b, h, i, 0)),
                  pl.BlockSpec((seq, rank), lambda b, h, i: (b, 0)),
                  pl.BlockSpec((seq, LANES), lambda b, h, i: (b, 0)),
                  pl.BlockSpec((1, rank, LANES + v_dim), lambda b, h, i: (h, 0, 0))],
        out_specs=pl.BlockSpec((tq, v_dim), lambda b, h, i: (b * nq + i, h)),
        scratch_shapes=[pltpu.VMEM((seq, qk_dim), BF16),
                        pltpu.VMEM((seq, v_dim), BF16)],
        compiler_params=_params(("arbitrary", "arbitrary", "arbitrary")),
        name="mla_attn",
    )(q, c_kv, kr_pad, w_kv)


def _final_norm_kernel(x_ref, g_ref, o_ref):
    o_ref[...] = _rms(x_ref[...], g_ref[...])


def final_norm(x, g, tm=1024):
    T, D = x.shape
    return pl.pallas_call(
        _final_norm_kernel,
        out_shape=jax.ShapeDtypeStruct((T, D), F32),
        grid=(T // tm,),
        in_specs=[pl.BlockSpec((tm, D), lambda i: (i, 0)),
                  pl.BlockSpec((1, D), lambda i: (0, 0))],
        out_specs=pl.BlockSpec((tm, D), lambda i: (i, 0)),
        compiler_params=_params(("arbitrary",)),
        name="final_norm",
    )(x, g.reshape(1, D))


def _rot_cols(w):
    half = w.shape[-1] // 2
    return jnp.concatenate([-w[..., half:], w[..., :half]], axis=-1)


def _pad_lanes(w):
    pad = [(0, 0)] * (w.ndim - 1) + [(0, LANES - w.shape[-1])]
    return jnp.pad(w, pad)


def kernel(x, mem, positions, g_mix, g_ffn, g_final, w_in_a, g_v, w_sp, b_sp, g_kv, w_kv_a,
           g_kv_lat, w_in_b, g_q_lat, w_uq, w_uk, w_uv, g_mem, w_mem_kv, w_out, w_ffn_up,
           conv_w, conv_b, w_ffn_down):
    batch, seq, d_model = x.shape
    depth = g_mix.shape[0]
    n_a = w_in_a.shape[0]
    n_mem = mem.shape[1]
    T = batch * seq
    kv_rank = g_kv_lat.shape[0]
    q_rank = g_q_lat.shape[1]
    n_heads, nope_dim = w_uk.shape[2], w_uk.shape[3]
    rope_dim = w_kv_a.shape[1] - kv_rank
    mem_w = w_mem_kv.shape[2] // 2
    g_w = g_v.shape[1]
    scale = (nope_dim + rope_dim) ** -0.5

    xs = x.reshape(T, d_model)
    mems = mem.reshape(batch * n_mem, d_model)

    inv = 1.0 / (ROPE_THETA ** (jnp.arange(0, rope_dim, 2, dtype=F32) / rope_dim))
    inv_row = _pad_lanes(jnp.concatenate([inv, inv])).reshape(1, LANES)
    cos_t, sin_t = rope_tables(positions.reshape(T, 1), inv_row)

    c_kv = kr_pad = None
    for l in range(depth):
        if l == n_a:
            w_kr = w_kv_a[:, kv_rank:]
            w_kv_cat = jnp.concatenate(
                [w_kv_a[:, :kv_rank], _pad_lanes(w_kr), _pad_lanes(_rot_cols(w_kr))], axis=1)
            kv_raw = norm_matmul(xs, g_kv, w_kv_cat.astype(BF16), F32,
                                 tm=1024, tn=w_kv_cat.shape[1], name="kv_proj")
            c_kv, kr_pad = kv_post(kv_raw, g_kv_lat, cos_t, sin_t)

        kvm = norm_matmul(mems, g_mem[l], w_mem_kv[l].astype(BF16), BF16,
                          tm=1024, tn=2 * mem_w, name="mem_kv_proj")
        if l < n_a:
            w_in = w_in_a[l].astype(BF16)
            z = norm_matmul(xs, g_mix[l], w_in, BF16, tm=1024, tn=w_in.shape[1] // 4,
                            name="in_proj_a")
            b_full = jnp.broadcast_to(b_sp[l][:, :, None], w_sp[l].shape)
            main = gmlp_gate(z, g_v[l], w_sp[l], b_full)
            q_col_block = (2 * g_w) // mem_w
        else:
            j = l - n_a
            w_in = w_in_b[j].astype(BF16)
            z = norm_matmul(xs, g_mix[l], w_in, BF16, tm=1024, tn=w_in.shape[1],
                            name="in_proj_b")
            wq = w_uq[j].reshape(q_rank, n_heads, nope_dim + rope_dim)
            wq_rope = wq[..., nope_dim:]
            w_main = jnp.concatenate(
                [wq[..., :nope_dim], _pad_lanes(wq_rope)], axis=-1).reshape(q_rank, -1)
            w_rot = _pad_lanes(_rot_cols(wq_rope)).reshape(q_rank, -1)
            q = q_prep(z, g_q_lat[j], w_main.astype(BF16), w_rot.astype(BF16), cos_t, sin_t,
                       batch, n_heads, scale)
            w_kv_h = jnp.concatenate([w_uk[j], w_uv[j]], axis=-1)
            w_kv_h = jnp.transpose(w_kv_h, (1, 0, 2)).astype(BF16)
            main = mla_attn(q, c_kv, kr_pad, w_kv_h)
            q_col_block = q_rank // mem_w
        mo = mem_attn(z, q_col_block, kvm, batch)
        xs = mix_out(xs, main, mo, w_out[l].astype(BF16))
        xs = conv_ffn(xs, g_ffn[l], w_ffn_up[l].astype(BF16), conv_w[l], conv_b[l],
                      w_ffn_down[l].astype(BF16), seq)
    return final_norm(xs, g_final).reshape(batch, seq, d_model)
```

```python
import functools

import jax
import jax.numpy as jnp
import numpy as np
from jax import lax
from jax.experimental import pallas as pl
from jax.experimental.pallas import tpu as pltpu

EPS = 1e-6
ROPE_THETA = 10000.0
MEM_HEADS = 4
CHUNK = 128
LANES = 128
SUBLANES = 8
CONV_W = 3
FFN_TF = 512
VMEM_LIMIT = 56 * 1024 * 1024

F32 = jnp.float32
BF16 = jnp.bfloat16


def _params(semantics, vmem=VMEM_LIMIT):
    return pltpu.CompilerParams(dimension_semantics=semantics, vmem_limit_bytes=vmem)


def _rms(x, g):
    ms = jnp.mean(x * x, axis=-1, keepdims=True)
    return x * lax.rsqrt(ms + EPS) * g


def _gelu(x):
    return 0.5 * x * (1.0 + lax.erf(x * np.float32(np.sqrt(0.5))))


def _rope_table_kernel(pos_ref, inv_ref, cos_ref, sin_ref):
    ang = pos_ref[...].astype(F32) * inv_ref[...]
    cos_ref[...] = jnp.cos(ang)
    sin_ref[...] = jnp.sin(ang)


def rope_tables(pos_col, inv_row, tm=2048):
    T = pos_col.shape[0]
    return pl.pallas_call(
        _rope_table_kernel,
        out_shape=(jax.ShapeDtypeStruct((T, LANES), F32),) * 2,
        grid=(T // tm,),
        in_specs=[pl.BlockSpec((tm, 1), lambda i: (i, 0)),
                  pl.BlockSpec((1, LANES), lambda i: (0, 0))],
        out_specs=(pl.BlockSpec((tm, LANES), lambda i: (i, 0)),) * 2,
        compiler_params=_params(("arbitrary",)),
        name="rope_tables",
    )(pos_col, inv_row)


def _norm_matmul_kernel(x_ref, g_ref, w_ref, o_ref, h_ref):
    @pl.when(pl.program_id(1) == 0)
    def _():
        h_ref[...] = _rms(x_ref[...], g_ref[...]).astype(h_ref.dtype)

    o_ref[...] = jnp.dot(h_ref[...], w_ref[...],
                         preferred_element_type=F32).astype(o_ref.dtype)


def norm_matmul(x, g, w, out_dtype, tm, tn, name):
    T, K = x.shape
    N = w.shape[1]
    tm = min(tm, T)
    return pl.pallas_call(
        _norm_matmul_kernel,
        out_shape=jax.ShapeDtypeStruct((T, N), out_dtype),
        grid=(T // tm, N // tn),
        in_specs=[pl.BlockSpec((tm, K), lambda i, j: (i, 0)),
                  pl.BlockSpec((1, K), lambda i, j: (0, 0)),
                  pl.BlockSpec((K, tn), lambda i, j: (0, j))],
        out_specs=pl.BlockSpec((tm, tn), lambda i, j: (i, j)),
        scratch_shapes=[pltpu.VMEM((tm, K), BF16)],
        compiler_params=_params(("arbitrary", "arbitrary")),
        name=name,
    )(x, g.reshape(1, K), w)


def _kv_post_kernel(kv_ref, g_ref, cos_ref, sin_ref, ckv_ref, kr_ref, *, rank):
    lat = kv_ref[:, :rank]
    ckv_ref[...] = _rms(lat, g_ref[...]).astype(ckv_ref.dtype)
    kr = kv_ref[:, rank:rank + LANES] * cos_ref[...]
    kr = kr + kv_ref[:, rank + LANES:] * sin_ref[...]
    kr_ref[...] = kr.astype(kr_ref.dtype)


def kv_post(kv_raw, g_lat, cos_t, sin_t, tm=1024):
    T, N = kv_raw.shape
    rank = N - 2 * LANES
    return pl.pallas_call(
        functools.partial(_kv_post_kernel, rank=rank),
        out_shape=(jax.ShapeDtypeStruct((T, rank), BF16),
                   jax.ShapeDtypeStruct((T, LANES), BF16)),
        grid=(T // tm,),
        in_specs=[pl.BlockSpec((tm, N), lambda i: (i, 0)),
                  pl.BlockSpec((1, rank), lambda i: (0, 0)),
                  pl.BlockSpec((tm, LANES), lambda i: (i, 0)),
                  pl.BlockSpec((tm, LANES), lambda i: (i, 0))],
        out_specs=(pl.BlockSpec((tm, rank), lambda i: (i, 0)),
                   pl.BlockSpec((tm, LANES), lambda i: (i, 0))),
        compiler_params=_params(("arbitrary",)),
        name="kv_post",
    )(kv_raw, g_lat.reshape(1, rank), cos_t, sin_t)


def _gmlp_kernel(u_ref, v_ref, gv_ref, w_ref, b_ref, o_ref, *, n_groups, tm):
    row = lax.broadcasted_iota(jnp.int32, (CHUNK, CHUNK), 0)
    col = lax.broadcasted_iota(jnp.int32, (CHUNK, CHUNK), 1)
    causal = row >= col
    w_tril = [jnp.where(causal, w_ref[g], 0.0).astype(BF16) for g in range(n_groups)]
    for c in range(tm // CHUNK):
        rows = slice(c * CHUNK, (c + 1) * CHUNK)
        ug = _gelu(u_ref[rows, :].astype(F32))
        vn = _rms(_gelu(v_ref[rows, :].astype(F32)), gv_ref[...]).astype(BF16)
        for g in range(n_groups):
            cols = slice(g * CHUNK, (g + 1) * CHUNK)
            sv = jnp.dot(w_tril[g], vn[:, cols], preferred_element_type=F32) + b_ref[g]
            o_ref[rows, cols] = (ug[:, cols] * sv).astype(o_ref.dtype)


def gmlp_gate(z, g_v, w_sp, b_full, tm=512):
    T = z.shape[0]
    n_groups = w_sp.shape[0]
    gw = n_groups * CHUNK
    return pl.pallas_call(
        functools.partial(_gmlp_kernel, n_groups=n_groups, tm=tm),
        out_shape=jax.ShapeDtypeStruct((T, gw), BF16),
        grid=(T // tm,),
        in_specs=[pl.BlockSpec((tm, gw), lambda i: (i, 0)),
                  pl.BlockSpec((tm, gw), lambda i: (i, 1)),
                  pl.BlockSpec((1, gw), lambda i: (0, 0)),
                  pl.BlockSpec((n_groups, CHUNK, CHUNK), lambda i: (0, 0, 0)),
                  pl.BlockSpec((n_groups, CHUNK, CHUNK), lambda i: (0, 0, 0))],
        out_specs=pl.BlockSpec((tm, gw), lambda i: (i, 0)),
        compiler_params=_params(("arbitrary",)),
        name="gmlp_gate",
    )(z, z, g_v.reshape(1, gw), w_sp, b_full)


def _mem_attn_kernel(q_ref, kv_ref, o_ref, *, n_heads, head_dim):
    width = n_heads * head_dim
    scale = np.float32(head_dim ** -0.5)
    for h in range(n_heads):
        cols = slice(h * head_dim, (h + 1) * head_dim)
        q = q_ref[:, cols]
        k = kv_ref[:, cols]
        v = kv_ref[:, width + h * head_dim:width + (h + 1) * head_dim]
        s = lax.dot_general(q, k, (((1,), (1,)), ((), ())),
                            preferred_element_type=F32) * scale
        m = jnp.max(s, axis=-1, keepdims=True)
        p = jnp.exp(s - m)
        l = jnp.sum(p, axis=-1, keepdims=True)
        o = jnp.dot(p.astype(BF16), v, preferred_element_type=F32) / l
        o_ref[:, cols] = o.astype(o_ref.dtype)


def mem_attn(z, q_col_block, kvm, batch, tm=1024):
    T = z.shape[0]
    n_mem = kvm.shape[0] // batch
    width = kvm.shape[1] // 2
    per_b = T // batch // tm
    return pl.pallas_call(
        functools.partial(_mem_attn_kernel, n_heads=MEM_HEADS, head_dim=width // MEM_HEADS),
        out_shape=jax.ShapeDtypeStruct((T, width), BF16),
        grid=(batch, per_b),
        in_specs=[pl.BlockSpec((tm, width), lambda b, i: (b * per_b + i, q_col_block)),
                  pl.BlockSpec((n_mem, 2 * width), lambda b, i: (b, 0))],
        out_specs=pl.BlockSpec((tm, width), lambda b, i: (b * per_b + i, 0)),
        compiler_params=_params(("arbitrary", "arbitrary")),
        name="mem_attn",
    )(z, kvm)


def _mix_out_kernel(x_ref, a_ref, b_ref, w_ref, o_ref):
    ka = a_ref.shape[1]
    acc = jnp.dot(a_ref[...], w_ref[:ka, :], preferred_element_type=F32)
    acc = acc + jnp.dot(b_ref[...], w_ref[ka:, :], preferred_element_type=F32)
    o_ref[...] = x_ref[...] + acc


def mix_out(x, main, mo, w_out, tm=512):
    T, D = x.shape
    ka, kb = main.shape[1], mo.shape[1]
    return pl.pallas_call(
        _mix_out_kernel,
        out_shape=jax.ShapeDtypeStruct((T, D), F32),
        grid=(T // tm,),
        in_specs=[pl.BlockSpec((tm, D), lambda i: (i, 0)),
                  pl.BlockSpec((tm, ka), lambda i: (i, 0)),
                  pl.BlockSpec((tm, kb), lambda i: (i, 0)),
                  pl.BlockSpec((ka + kb, D), lambda i: (0, 0))],
        out_specs=pl.BlockSpec((tm, D), lambda i: (i, 0)),
        compiler_params=_params(("arbitrary",)),
        name="mix_out",
    )(x, main, mo, w_out)


def _conv_ffn_kernel(x_ref, g_ref, wu_ref, cw_ref, cb_ref, wd_ref, o_ref,
                     h_ref, a_ref, carry_ref, *, tm, rc, tf, tiles_per_seq):
    i = pl.program_id(0)
    j = pl.program_id(1)

    @pl.when(jnp.logical_and(i == 0, j == 0))
    def _():
        carry_ref[...] = jnp.zeros(carry_ref.shape, F32)

    @pl.when(j == 0)
    def _():
        for r in range(tm // rc):
            rows = slice(r * rc, (r + 1) * rc)
            x = x_ref[rows, :]
            h_ref[rows, :] = _rms(x, g_ref[...]).astype(h_ref.dtype)
            o_ref[rows, :] = x

    first_in_seq = (i % tiles_per_seq) == 0
    a_ref[:SUBLANES, :] = jnp.where(first_in_seq, 0.0, carry_ref[j])
    cw = cw_ref[...]
    cb = cb_ref[...]
    for r in range(tm // rc):
        lo = SUBLANES + r * rc
        a_ref[lo:lo + rc, :] = jnp.dot(h_ref[r * rc:(r + 1) * rc, :], wu_ref[...],
                                       preferred_element_type=F32)
        c = cb + a_ref[lo:lo + rc, :] * cw[CONV_W - 1:CONV_W, :]
        for k in range(CONV_W - 1):
            lag = CONV_W - 1 - k
            c = c + a_ref[lo - lag:lo - lag + rc, :] * cw[k:k + 1, :]
        cg = c[:, :tf]
        gated = (cg * jax.nn.sigmoid(cg) * c[:, tf:]).astype(BF16)
        o_ref[r * rc:(r + 1) * rc, :] += jnp.dot(gated, wd_ref[...],
                                                 preferred_element_type=F32)
    carry_ref[j] = a_ref[tm:, :]


def conv_ffn(x, g, w_up, conv_w, conv_b, w_down, seq, tm=1024, tf=512, rc=512):
    T, D = x.shape
    d_ff = w_down.shape[0]
    nff = d_ff // tf
    kern = functools.partial(_conv_ffn_kernel, tm=tm, rc=rc, tf=tf, tiles_per_seq=seq // tm)
    return pl.pallas_call(
        kern,
        out_shape=jax.ShapeDtypeStruct((T, D), F32),
        grid=(T // tm, nff),
        in_specs=[pl.BlockSpec((tm, D), lambda i, j: (i, 0), pipeline_mode=pl.Buffered(1)),
                  pl.BlockSpec((1, D), lambda i, j: (0, 0)),
                  pl.BlockSpec((D, 2 * tf), lambda i, j: (0, j)),
                  pl.BlockSpec((CONV_W, 2 * tf), lambda i, j: (0, j)),
                  pl.BlockSpec((1, 2 * tf), lambda i, j: (0, j)),
                  pl.BlockSpec((tf, D), lambda i, j: (j, 0))],
        out_specs=pl.BlockSpec((tm, D), lambda i, j: (i, 0)),
        scratch_shapes=[pltpu.VMEM((tm, D), BF16),
                        pltpu.VMEM((tm + SUBLANES, 2 * tf), F32),
                        pltpu.VMEM((nff, SUBLANES, 2 * tf), F32)],
        compiler_params=_params(("arbitrary", "arbitrary")),
        name="conv_ffn",
    )(x, g.reshape(1, D), w_up, conv_w, conv_b.reshape(1, -1), w_down)


def _interleave_gate_value(w, tf):
    lead = w.shape[:-1]
    d_ff = w.shape[-1] // 2
    w = w.reshape(*lead, 2, d_ff // tf, tf)
    return jnp.swapaxes(w, -3, -2).reshape(*lead, 2 * d_ff)


def _q_prep_kernel(q_ref, g_ref, wm_ref, wr_ref, cos_ref, sin_ref, o_ref, *, n_heads, scale):
    qn = _rms(q_ref[...].astype(F32), g_ref[...]).astype(BF16)
    a = jnp.dot(qn, wm_ref[...], preferred_element_type=F32)
    r = jnp.dot(qn, wr_ref[...], preferred_element_type=F32)
    cos = cos_ref[...]
    sin = sin_ref[...]
    for h in range(n_heads):
        base = 2 * LANES * h
        o_ref[0, h, :, :LANES] = (a[:, base:base + LANES] * scale).astype(o_ref.dtype)
        rope = a[:, base + LANES:base + 2 * LANES] * cos + r[:, h * LANES:(h + 1) * LANES] * sin
        o_ref[0, h, :, LANES:] = (rope * scale).astype(o_ref.dtype)


def q_prep(z, g_q, w_main, w_rot, cos_t, sin_t, batch, n_heads, scale, tm=512):
    T = z.shape[0]
    seq = T // batch
    rank = w_main.shape[0]
    per_b = seq // tm
    kern = functools.partial(_q_prep_kernel, n_heads=n_heads, scale=np.float32(scale))
    return pl.pallas_call(
        kern,
        out_shape=jax.ShapeDtypeStruct((batch, n_heads, seq, 2 * LANES), BF16),
        grid=(batch, per_b),
        in_specs=[pl.BlockSpec((tm, rank), lambda b, i: (b * per_b + i, 0)),
                  pl.BlockSpec((1, rank), lambda b, i: (0, 0)),
                  pl.BlockSpec(w_main.shape, lambda b, i: (0, 0)),
                  pl.BlockSpec(w_rot.shape, lambda b, i: (0, 0)),
                  pl.BlockSpec((tm, LANES), lambda b, i: (b * per_b + i, 0)),
                  pl.BlockSpec((tm, LANES), lambda b, i: (b * per_b + i, 0))],
        out_specs=pl.BlockSpec((1, n_heads, tm, 2 * LANES), lambda b, i: (b, 0, i, 0)),
        compiler_params=_params(("arbitrary", "arbitrary")),
        name="q_prep",
    )(z, g_q.reshape(1, rank), w_main, w_rot, cos_t, sin_t)


def _mla_attn_kernel(q_ref, ckv_ref, kr_ref, wkv_ref, o_ref, k_ref, vt_ref, acc_ref,
                     *, tq, tk, hp):
    i = pl.program_id(2)
    n_kb = k_ref.shape[1]
    v_dim = vt_ref.shape[2]

    @pl.when(i == 0)
    def _():
        for jb in range(n_kb):
            rows = slice(jb * tk, (jb + 1) * tk)
            for hh in range(hp):
                kv = jnp.dot(ckv_ref[rows, :], wkv_ref[hh], preferred_element_type=F32)
                k_ref[hh, jb, :, :LANES] = kv[:, :LANES].astype(k_ref.dtype)
                k_ref[hh, jb, :, LANES:] = kr_ref[rows, :]
                vt_ref[hh, jb] = kv[:, LANES:].T.astype(vt_ref.dtype)

    acc_ref[...] = jnp.zeros(acc_ref.shape, F32)

    def step(jb, carry, masked):
        out = []
        acc_new = []
        for hh in range(hp):
            m, l = carry[hh]
            st = lax.dot_general(k_ref[hh, jb], q_ref[0, hh], (((1,), (1,)), ((), ())),
                                 preferred_element_type=F32)
            if masked:
                k_pos = lax.broadcasted_iota(jnp.int32, (tk, tq), 0)
                q_pos = lax.broadcasted_iota(jnp.int32, (tk, tq), 1)
                st = jnp.where(k_pos <= q_pos, st, -jnp.inf)
            m_new = jnp.maximum(m, jnp.max(st, axis=0, keepdims=True))
            alpha = jnp.exp2(m - m_new)
            p = jnp.exp2(st - m_new)
            l = alpha * l + jnp.sum(p, axis=0, keepdims=True)
            pv = jnp.dot(vt_ref[hh, jb], p.astype(BF16), preferred_element_type=F32)
            acc_new.append(alpha * acc_ref[hh] + pv)
            out.append((m_new, l))
        acc_ref[...] = jnp.stack(acc_new)
        return tuple(out)

    init = tuple((jnp.full((1, tq), -jnp.inf, F32), jnp.zeros((1, tq), F32))
                 for _ in range(hp))
    carry = lax.fori_loop(0, i, lambda jb, c: step(jb, c, False), init)
    carry = step(i, carry, True)
    for hh in range(hp):
        o = acc_ref[hh] / carry[hh][1]
        o_ref[:, hh * v_dim:(hh + 1) * v_dim] = o.T.astype(o_ref.dtype)


def mla_attn(q, c_kv, kr_pad, w_kv, tq=512, hp=2):
    batch, n_heads, seq, qk_dim = q.shape
    rank = c_kv.shape[1]
    v_dim = w_kv.shape[2] - LANES
    nq = seq // tq
    return pl.pallas_call(
        functools.partial(_mla_attn_kernel, tq=tq, tk=tq, hp=hp),
        out_shape=jax.ShapeDtypeStruct((batch * seq, n_heads * v_dim), BF16),
        grid=(batch, n_heads // hp, nq),
        in_specs=[pl.BlockSpec((1, hp, tq, qk_dim), lambda b, h, i: (b, h, i, 0)),
                  pl.BlockSpec((seq, rank), lambda b, h, i: (b, 0)),
                  pl.BlockSpec((seq, LANES), lambda b, h, i: (b, 0)),
                  pl.BlockSpec((hp, rank, LANES + v_dim), lambda b, h, i: (h, 0, 0))],
        out_specs=pl.BlockSpec((tq, hp * v_dim), lambda b, h, i: (b * nq + i, h)),
        scratch_shapes=[pltpu.VMEM((hp, nq, tq, qk_dim), BF16),
                        pltpu.VMEM((hp, nq, v_dim, tq), BF16),
                        pltpu.VMEM((hp, v_dim, tq), F32)],
        compiler_params=_params(("arbitrary", "arbitrary", "arbitrary")),
        name="mla_attn",
    )(q, c_kv, kr_pad, w_kv)


def _final_norm_kernel(x_ref, g_ref, o_ref):
    o_ref[...] = _rms(x_ref[...], g_ref[...])


def final_norm(x, g, tm=1024):
    T, D = x.shape
    return pl.pallas_call(
        _final_norm_kernel,
        out_shape=jax.ShapeDtypeStruct((T, D), F32),
        grid=(T // tm,),
        in_specs=[pl.BlockSpec((tm, D), lambda i: (i, 0)),
                  pl.BlockSpec((1, D), lambda i: (0, 0))],
        out_specs=pl.BlockSpec((tm, D), lambda i: (i, 0)),
        compiler_params=_params(("arbitrary",)),
        name="final_norm",
    )(x, g.reshape(1, D))


def _rot_cols(w):
    half = w.shape[-1] // 2
    return jnp.concatenate([-w[..., half:], w[..., :half]], axis=-1)


def _pad_lanes(w):
    pad = [(0, 0)] * (w.ndim - 1) + [(0, LANES - w.shape[-1])]
    return jnp.pad(w, pad)


def kernel(x, mem, positions, g_mix, g_ffn, g_final, w_in_a, g_v, w_sp, b_sp, g_kv, w_kv_a,
           g_kv_lat, w_in_b, g_q_lat, w_uq, w_uk, w_uv, g_mem, w_mem_kv, w_out, w_ffn_up,
           conv_w, conv_b, w_ffn_down):
    batch, seq, d_model = x.shape
    depth = g_mix.shape[0]
    n_a = w_in_a.shape[0]
    n_mem = mem.shape[1]
    T = batch * seq
    kv_rank = g_kv_lat.shape[0]
    q_rank = g_q_lat.shape[1]
    n_heads, nope_dim = w_uk.shape[2], w_uk.shape[3]
    rope_dim = w_kv_a.shape[1] - kv_rank
    mem_w = w_mem_kv.shape[2] // 2
    g_w = g_v.shape[1]
    scale = (nope_dim + rope_dim) ** -0.5 * np.log2(np.e)

    xs = x.reshape(T, d_model)
    mems = mem.reshape(batch * n_mem, d_model)

    inv = 1.0 / (ROPE_THETA ** (jnp.arange(0, rope_dim, 2, dtype=F32) / rope_dim))
    inv_row = _pad_lanes(jnp.concatenate([inv, inv])).reshape(1, LANES)
    cos_t, sin_t = rope_tables(positions.reshape(T, 1), inv_row)

    c_kv = kr_pad = None
    for l in range(depth):
        if l == n_a:
            w_kr = w_kv_a[:, kv_rank:]
            w_kv_cat = jnp.concatenate(
                [w_kv_a[:, :kv_rank], _pad_lanes(w_kr), _pad_lanes(_rot_cols(w_kr))], axis=1)
            kv_raw = norm_matmul(xs, g_kv, w_kv_cat.astype(BF16), F32,
                                 tm=1024, tn=w_kv_cat.shape[1], name="kv_proj")
            c_kv, kr_pad = kv_post(kv_raw, g_kv_lat, cos_t, sin_t)

        kvm = norm_matmul(mems, g_mem[l], w_mem_kv[l].astype(BF16), BF16,
                          tm=1024, tn=2 * mem_w, name="mem_kv_proj")
        if l < n_a:
            w_in = w_in_a[l].astype(BF16)
            z = norm_matmul(xs, g_mix[l], w_in, BF16, tm=1024, tn=w_in.shape[1] // 4,
                            name="in_proj_a")
            b_full = jnp.broadcast_to(b_sp[l][:, :, None], w_sp[l].shape)
            main = gmlp_gate(z, g_v[l], w_sp[l], b_full)
            q_col_block = (2 * g_w) // mem_w
        else:
            j = l - n_a
            w_in = w_in_b[j].astype(BF16)
            z = norm_matmul(xs, g_mix[l], w_in, BF16, tm=1024, tn=w_in.shape[1],
                            name="in_proj_b")
            wq = w_uq[j].reshape(q_rank, n_heads, nope_dim + rope_dim)
            wq_rope = wq[..., nope_dim:]
            w_main = jnp.concatenate(
                [wq[..., :nope_dim], _pad_lanes(wq_rope)], axis=-1).reshape(q_rank, -1)
            w_rot = _pad_lanes(_rot_cols(wq_rope)).reshape(q_rank, -1)
            q = q_prep(z, g_q_lat[j], w_main.astype(BF16), w_rot.astype(BF16), cos_t, sin_t,
                       batch, n_heads, scale)
            w_kv_h = jnp.concatenate([w_uk[j], w_uv[j]], axis=-1)
            w_kv_h = jnp.transpose(w_kv_h, (1, 0, 2)).astype(BF16)
            main = mla_attn(q, c_kv, kr_pad, w_kv_h)
            q_col_block = q_rank // mem_w
        mo = mem_attn(z, q_col_block, kvm, batch)
        xs = mix_out(xs, main, mo, w_out[l].astype(BF16))
        xs = conv_ffn(xs, g_ffn[l], _interleave_gate_value(w_ffn_up[l], FFN_TF).astype(BF16),
                      _interleave_gate_value(conv_w[l], FFN_TF),
                      _interleave_gate_value(conv_b[l], FFN_TF),
                      w_ffn_down[l].astype(BF16), seq, tf=FFN_TF)
    return final_norm(xs, g_final).reshape(batch, seq, d_model)
```

```python
import functools

import jax
import jax.numpy as jnp
import numpy as np
from jax import lax
from jax.experimental import pallas as pl
from jax.experimental.pallas import tpu as pltpu

EPS = 1e-6
ROPE_THETA = 10000.0
MEM_HEADS = 4
CHUNK = 128
LANES = 128
SUBLANES = 8
CONV_W = 3
FFN_TF = 512
VMEM_LIMIT = 56 * 1024 * 1024

F32 = jnp.float32
BF16 = jnp.bfloat16


def _params(semantics, vmem=VMEM_LIMIT):
    return pltpu.CompilerParams(dimension_semantics=semantics, vmem_limit_bytes=vmem)


def _rms(x, g):
    ms = jnp.mean(x * x, axis=-1, keepdims=True)
    return x * lax.rsqrt(ms + EPS) * g


def _gelu(x):
    return 0.5 * x * (1.0 + lax.erf(x * np.float32(np.sqrt(0.5))))


def _rope_table_kernel(pos_ref, inv_ref, cos_ref, sin_ref):
    ang = pos_ref[...].astype(F32) * inv_ref[...]
    cos_ref[...] = jnp.cos(ang)
    sin_ref[...] = jnp.sin(ang)


def rope_tables(pos_col, inv_row, tm=2048):
    T = pos_col.shape[0]
    return pl.pallas_call(
        _rope_table_kernel,
        out_shape=(jax.ShapeDtypeStruct((T, LANES), F32),) * 2,
        grid=(T // tm,),
        in_specs=[pl.BlockSpec((tm, 1), lambda i: (i, 0)),
                  pl.BlockSpec((1, LANES), lambda i: (0, 0))],
        out_specs=(pl.BlockSpec((tm, LANES), lambda i: (i, 0)),) * 2,
        compiler_params=_params(("arbitrary",)),
        name="rope_tables",
    )(pos_col, inv_row)


def _norm_matmul_kernel(x_ref, g_ref, w_ref, o_ref, h_ref):
    @pl.when(pl.program_id(1) == 0)
    def _():
        h_ref[...] = _rms(x_ref[...], g_ref[...]).astype(h_ref.dtype)

    o_ref[...] = jnp.dot(h_ref[...], w_ref[...],
                         preferred_element_type=F32).astype(o_ref.dtype)


def norm_matmul(x, g, w, out_dtype, tm, tn, name):
    T, K = x.shape
    N = w.shape[1]
    tm = min(tm, T)
    return pl.pallas_call(
        _norm_matmul_kernel,
        out_shape=jax.ShapeDtypeStruct((T, N), out_dtype),
        grid=(T // tm, N // tn),
        in_specs=[pl.BlockSpec((tm, K), lambda i, j: (i, 0)),
                  pl.BlockSpec((1, K), lambda i, j: (0, 0)),
                  pl.BlockSpec((K, tn), lambda i, j: (0, j))],
        out_specs=pl.BlockSpec((tm, tn), lambda i, j: (i, j)),
        scratch_shapes=[pltpu.VMEM((tm, K), BF16)],
        compiler_params=_params(("arbitrary", "arbitrary")),
        name=name,
    )(x, g.reshape(1, K), w)


def _kv_post_kernel(kv_ref, g_ref, cos_ref, sin_ref, ckv_ref, kr_ref, *, rank):
    lat = kv_ref[:, :rank]
    ckv_ref[...] = _rms(lat, g_ref[...]).astype(ckv_ref.dtype)
    kr = kv_ref[:, rank:rank + LANES] * cos_ref[...]
    kr = kr + kv_ref[:, rank + LANES:] * sin_ref[...]
    kr_ref[...] = kr.astype(kr_ref.dtype)


def kv_post(kv_raw, g_lat, cos_t, sin_t, tm=1024):
    T, N = kv_raw.shape
    rank = N - 2 * LANES
    return pl.pallas_call(
        functools.partial(_kv_post_kernel, rank=rank),
        out_shape=(jax.ShapeDtypeStruct((T, rank), BF16),
                   jax.ShapeDtypeStruct((T, LANES), BF16)),
        grid=(T // tm,),
        in_specs=[pl.BlockSpec((tm, N), lambda i: (i, 0)),
                  pl.BlockSpec((1, rank), lambda i: (0, 0)),
                  pl.BlockSpec((tm, LANES), lambda i: (i, 0)),
                  pl.BlockSpec((tm, LANES), lambda i: (i, 0))],
        out_specs=(pl.BlockSpec((tm, rank), lambda i: (i, 0)),
                   pl.BlockSpec((tm, LANES), lambda i: (i, 0))),
        compiler_params=_params(("arbitrary",)),
        name="kv_post",
    )(kv_raw, g_lat.reshape(1, rank), cos_t, sin_t)


def _gmlp_kernel(u_ref, v_ref, gv_ref, w_ref, b_ref, o_ref, *, n_groups, tm):
    row = lax.broadcasted_iota(jnp.int32, (CHUNK, CHUNK), 0)
    col = lax.broadcasted_iota(jnp.int32, (CHUNK, CHUNK), 1)
    causal = row >= col
    w_tril = [jnp.where(causal, w_ref[g], 0.0).astype(BF16) for g in range(n_groups)]
    for c in range(tm // CHUNK):
        rows = slice(c * CHUNK, (c + 1) * CHUNK)
        ug = _gelu(u_ref[rows, :].astype(F32))
        vn = _rms(_gelu(v_ref[rows, :].astype(F32)), gv_ref[...]).astype(BF16)
        for g in range(n_groups):
            cols = slice(g * CHUNK, (g + 1) * CHUNK)
            sv = jnp.dot(w_tril[g], vn[:, cols], preferred_element_type=F32) + b_ref[g]
            o_ref[rows, cols] = (ug[:, cols] * sv).astype(o_ref.dtype)


def gmlp_gate(z, g_v, w_sp, b_full, tm=512):
    T = z.shape[0]
    n_groups = w_sp.shape[0]
    gw = n_groups * CHUNK
    return pl.pallas_call(
        functools.partial(_gmlp_kernel, n_groups=n_groups, tm=tm),
        out_shape=jax.ShapeDtypeStruct((T, gw), BF16),
        grid=(T // tm,),
        in_specs=[pl.BlockSpec((tm, gw), lambda i: (i, 0)),
                  pl.BlockSpec((tm, gw), lambda i: (i, 1)),
                  pl.BlockSpec((1, gw), lambda i: (0, 0)),
                  pl.BlockSpec((n_groups, CHUNK, CHUNK), lambda i: (0, 0, 0)),
                  pl.BlockSpec((n_groups, CHUNK, CHUNK), lambda i: (0, 0, 0))],
        out_specs=pl.BlockSpec((tm, gw), lambda i: (i, 0)),
        compiler_params=_params(("arbitrary",)),
        name="gmlp_gate",
    )(z, z, g_v.reshape(1, gw), w_sp, b_full)


def _mem_attn_kernel(q_ref, kv_ref, o_ref, *, n_heads, head_dim):
    width = n_heads * head_dim
    scale = np.float32(head_dim ** -0.5)
    for h in range(n_heads):
        cols = slice(h * head_dim, (h + 1) * head_dim)
        q = q_ref[:, cols]
        k = kv_ref[:, cols]
        v = kv_ref[:, width + h * head_dim:width + (h + 1) * head_dim]
        s = lax.dot_general(q, k, (((1,), (1,)), ((), ())),
                            preferred_element_type=F32) * scale
        m = jnp.max(s, axis=-1, keepdims=True)
        p = jnp.exp(s - m)
        l = jnp.sum(p, axis=-1, keepdims=True)
        o = jnp.dot(p.astype(BF16), v, preferred_element_type=F32) / l
        o_ref[:, cols] = o.astype(o_ref.dtype)


def mem_attn(z, q_col_block, kvm, batch, tm=1024):
    T = z.shape[0]
    n_mem = kvm.shape[0] // batch
    width = kvm.shape[1] // 2
    per_b = T // batch // tm
    return pl.pallas_call(
        functools.partial(_mem_attn_kernel, n_heads=MEM_HEADS, head_dim=width // MEM_HEADS),
        out_shape=jax.ShapeDtypeStruct((T, width), BF16),
        grid=(batch, per_b),
        in_specs=[pl.BlockSpec((tm, width), lambda b, i: (b * per_b + i, q_col_block)),
                  pl.BlockSpec((n_mem, 2 * width), lambda b, i: (b, 0))],
        out_specs=pl.BlockSpec((tm, width), lambda b, i: (b * per_b + i, 0)),
        compiler_params=_params(("arbitrary", "arbitrary")),
        name="mem_attn",
    )(z, kvm)


def _mix_out_kernel(x_ref, a_ref, b_ref, w_ref, o_ref):
    ka = a_ref.shape[1]
    acc = jnp.dot(a_ref[...], w_ref[:ka, :], preferred_element_type=F32)
    acc = acc + jnp.dot(b_ref[...], w_ref[ka:, :], preferred_element_type=F32)
    o_ref[...] = x_ref[...] + acc


def mix_out(x, main, mo, w_out, tm=512):
    T, D = x.shape
    ka, kb = main.shape[1], mo.shape[1]
    return pl.pallas_call(
        _mix_out_kernel,
        out_shape=jax.ShapeDtypeStruct((T, D), F32),
        grid=(T // tm,),
        in_specs=[pl.BlockSpec((tm, D), lambda i: (i, 0)),
                  pl.BlockSpec((tm, ka), lambda i: (i, 0)),
                  pl.BlockSpec((tm, kb), lambda i: (i, 0)),
                  pl.BlockSpec((ka + kb, D), lambda i: (0, 0))],
        out_specs=pl.BlockSpec((tm, D), lambda i: (i, 0)),
        compiler_params=_params(("arbitrary",)),
        name="mix_out",
    )(x, main, mo, w_out)


def _conv_ffn_kernel(x_ref, g_ref, wg_ref, wv_ref, cw_ref, cb_ref, wd_ref, o_ref,
                     h_ref, a_ref, carry_ref, *, tm, rc, tf, tiles_per_seq):
    i = pl.program_id(0)
    j = pl.program_id(1)

    @pl.when(jnp.logical_and(i == 0, j == 0))
    def _():
        carry_ref[...] = jnp.zeros(carry_ref.shape, F32)

    @pl.when(j == 0)
    def _():
        for r in range(tm // rc):
            rows = slice(r * rc, (r + 1) * rc)
            x = x_ref[rows, :]
            h_ref[rows, :] = _rms(x, g_ref[...]).astype(h_ref.dtype)
            o_ref[rows, :] = x

    first_in_seq = (i % tiles_per_seq) == 0
    a_ref[:SUBLANES, :] = jnp.where(first_in_seq, 0.0, carry_ref[j])
    cw = cw_ref[...]
    cb = cb_ref[...]
    for r in range(tm // rc):
        lo = SUBLANES + r * rc
        h = h_ref[r * rc:(r + 1) * rc, :]
        a_ref[lo:lo + rc, :tf] = jnp.dot(h, wg_ref[...], preferred_element_type=F32)
        a_ref[lo:lo + rc, tf:] = jnp.dot(h, wv_ref[...], preferred_element_type=F32)
        c = cb + a_ref[lo:lo + rc, :] * cw[CONV_W - 1:CONV_W, :]
        for k in range(CONV_W - 1):
            lag = CONV_W - 1 - k
            c = c + a_ref[lo - lag:lo - lag + rc, :] * cw[k:k + 1, :]
        cg = c[:, :tf]
        gated = (cg * jax.nn.sigmoid(cg) * c[:, tf:]).astype(BF16)
        o_ref[r * rc:(r + 1) * rc, :] += jnp.dot(gated, wd_ref[...],
                                                 preferred_element_type=F32)
    carry_ref[j] = a_ref[tm:, :]


def conv_ffn(x, g, w_up, conv_w, conv_b, w_down, seq, tm=1024, tf=512, rc=512):
    T, D = x.shape
    d_ff = w_down.shape[0]
    nff = d_ff // tf
    kern = functools.partial(_conv_ffn_kernel, tm=tm, rc=rc, tf=tf, tiles_per_seq=seq // tm)
    return pl.pallas_call(
        kern,
        out_shape=jax.ShapeDtypeStruct((T, D), F32),
        grid=(T // tm, nff),
        in_specs=[pl.BlockSpec((tm, D), lambda i, j: (i, 0), pipeline_mode=pl.Buffered(1)),
                  pl.BlockSpec((1, D), lambda i, j: (0, 0)),
                  pl.BlockSpec((D, tf), lambda i, j: (0, j)),
                  pl.BlockSpec((D, tf), lambda i, j: (0, nff + j)),
                  pl.BlockSpec((CONV_W, 2 * tf), lambda i, j: (0, j)),
                  pl.BlockSpec((1, 2 * tf), lambda i, j: (0, j)),
                  pl.BlockSpec((tf, D), lambda i, j: (j, 0))],
        out_specs=pl.BlockSpec((tm, D), lambda i, j: (i, 0)),
        scratch_shapes=[pltpu.VMEM((tm, D), BF16),
                        pltpu.VMEM((tm + SUBLANES, 2 * tf), F32),
                        pltpu.VMEM((nff, SUBLANES, 2 * tf), F32)],
        compiler_params=_params(("arbitrary", "arbitrary")),
        name="conv_ffn",
    )(x, g.reshape(1, D), w_up, w_up, conv_w, conv_b.reshape(1, -1), w_down)


def _interleave_gate_value(w, tf):
    lead = w.shape[:-1]
    d_ff = w.shape[-1] // 2
    w = w.reshape(*lead, 2, d_ff // tf, tf)
    return jnp.swapaxes(w, -3, -2).reshape(*lead, 2 * d_ff)


def _q_prep_kernel(q_ref, g_ref, wm_ref, wr_ref, cos_ref, sin_ref, o_ref, *, n_heads, scale):
    qn = _rms(q_ref[...].astype(F32), g_ref[...]).astype(BF16)
    a = jnp.dot(qn, wm_ref[...], preferred_element_type=F32)
    r = jnp.dot(qn, wr_ref[...], preferred_element_type=F32)
    cos = cos_ref[...]
    sin = sin_ref[...]
    for h in range(n_heads):
        base = 2 * LANES * h
        o_ref[0, h, :, :LANES] = (a[:, base:base + LANES] * scale).astype(o_ref.dtype)
        rope = a[:, base + LANES:base + 2 * LANES] * cos + r[:, h * LANES:(h + 1) * LANES] * sin
        o_ref[0, h, :, LANES:] = (rope * scale).astype(o_ref.dtype)


def q_prep(z, g_q, w_main, w_rot, cos_t, sin_t, batch, n_heads, scale, tm=512):
    T = z.shape[0]
    seq = T // batch
    rank = w_main.shape[0]
    per_b = seq // tm
    kern = functools.partial(_q_prep_kernel, n_heads=n_heads, scale=np.float32(scale))
    return pl.pallas_call(
        kern,
        out_shape=jax.ShapeDtypeStruct((batch, n_heads, seq, 2 * LANES), BF16),
        grid=(batch, per_b),
        in_specs=[pl.BlockSpec((tm, rank), lambda b, i: (b * per_b + i, 0)),
                  pl.BlockSpec((1, rank), lambda b, i: (0, 0)),
                  pl.BlockSpec(w_main.shape, lambda b, i: (0, 0)),
                  pl.BlockSpec(w_rot.shape, lambda b, i: (0, 0)),
                  pl.BlockSpec((tm, LANES), lambda b, i: (b * per_b + i, 0)),
                  pl.BlockSpec((tm, LANES), lambda b, i: (b * per_b + i, 0))],
        out_specs=pl.BlockSpec((1, n_heads, tm, 2 * LANES), lambda b, i: (b, 0, i, 0)),
        compiler_params=_params(("arbitrary", "arbitrary")),
        name="q_prep",
    )(z, g_q.reshape(1, rank), w_main, w_rot, cos_t, sin_t)


def _mla_attn_kernel(q_ref, ckv_ref, kr_ref, wkv_ref, o_ref, k_ref, vt_ref, acc_ref,
                     sa_ref, sb_ref, *, tq, tk, hp):
    i = pl.program_id(2)
    n_kb = k_ref.shape[1]
    v_dim = vt_ref.shape[2]

    @pl.when(i == 0)
    def _():
        for jb in range(n_kb):
            rows = slice(jb * tk, (jb + 1) * tk)
            for hh in range(hp):
                kv = jnp.dot(ckv_ref[rows, :], wkv_ref[hh], preferred_element_type=F32)
                k_ref[hh, jb, :, :LANES] = kv[:, :LANES].astype(k_ref.dtype)
                k_ref[hh, jb, :, LANES:] = kr_ref[rows, :]
                vt_ref[hh, jb] = kv[:, LANES:].T.astype(vt_ref.dtype)

    acc_ref[...] = jnp.zeros(acc_ref.shape, F32)

    def scores(jb, dst_ref):
        for hh in range(hp):
            dst_ref[hh] = lax.dot_general(k_ref[hh, jb], q_ref[0, hh], (((1,), (1,)), ((), ())),
                                          preferred_element_type=F32)

    def consume(jb, src_ref, carry, diag_offset):
        out = []
        for hh in range(hp):
            m, l = carry[hh]
            st = src_ref[hh]
            if diag_offset is not None:
                k_pos = lax.broadcasted_iota(jnp.int32, (tk, tq), 0) + diag_offset
                q_pos = lax.broadcasted_iota(jnp.int32, (tk, tq), 1)
                st = jnp.where(k_pos <= q_pos, st, -jnp.inf)
            m_new = jnp.maximum(m, jnp.max(st, axis=0, keepdims=True))
            alpha = jnp.exp2(m - m_new)
            p = jnp.exp2(st - m_new)
            l = alpha * l + jnp.sum(p, axis=0, keepdims=True)
            pv = jnp.dot(vt_ref[hh, jb], p.astype(BF16), preferred_element_type=F32)
            acc_ref[hh] = alpha * acc_ref[hh] + pv
            out.append((m_new, l))
        return tuple(out)

    def pair(t, carry):
        scores(2 * t + 1, sb_ref)
        carry = consume(2 * t, sa_ref, carry, None)
        scores(2 * t + 2, sa_ref)
        return consume(2 * t + 1, sb_ref, carry, None)

    init = tuple((jnp.full((1, tq), -jnp.inf, F32), jnp.zeros((1, tq), F32))
                 for _ in range(hp))
    scores(0, sa_ref)
    carry = lax.fori_loop(0, i, pair, init)
    scores(2 * i + 1, sb_ref)
    carry = consume(2 * i, sa_ref, carry, 0)
    carry = consume(2 * i + 1, sb_ref, carry, tk)
    for hh in range(hp):
        o = acc_ref[hh] / carry[hh][1]
        o_ref[:, hh * v_dim:(hh + 1) * v_dim] = o.T.astype(o_ref.dtype)


def mla_attn(q, c_kv, kr_pad, w_kv, tq=1024, hp=2):
    tk = tq // 2
    batch, n_heads, seq, qk_dim = q.shape
    rank = c_kv.shape[1]
    v_dim = w_kv.shape[2] - LANES
    nq = seq // tq
    return pl.pallas_call(
        functools.partial(_mla_attn_kernel, tq=tq, tk=tk, hp=hp),
        out_shape=jax.ShapeDtypeStruct((batch * seq, n_heads * v_dim), BF16),
        grid=(batch, n_heads // hp, nq),
        in_specs=[pl.BlockSpec((1, hp, tq, qk_dim), lambda b, h, i: (b, h, i, 0)),
                  pl.BlockSpec((seq, rank), lambda b, h, i: (b, 0)),
                  pl.BlockSpec((seq, LANES), lambda b, h, i: (b, 0)),
                  pl.BlockSpec((hp, rank, LANES + v_dim), lambda b, h, i: (h, 0, 0))],
        out_specs=pl.BlockSpec((tq, hp * v_dim), lambda b, h, i: (b * nq + i, h)),
        scratch_shapes=[pltpu.VMEM((hp, seq // tk, tk, qk_dim), BF16),
                        pltpu.VMEM((hp, seq // tk, v_dim, tk), BF16),
                        pltpu.VMEM((hp, v_dim, tq), F32),
                        pltpu.VMEM((hp, tk, tq), F32),
                        pltpu.VMEM((hp, tk, tq), F32)],
        compiler_params=_params(("arbitrary", "arbitrary", "arbitrary")),
        name="mla_attn",
    )(q, c_kv, kr_pad, w_kv)


def _final_norm_kernel(x_ref, g_ref, o_ref):
    o_ref[...] = _rms(x_ref[...], g_ref[...])


def final_norm(x, g, tm=1024):
    T, D = x.shape
    return pl.pallas_call(
        _final_norm_kernel,
        out_shape=jax.ShapeDtypeStruct((T, D), F32),
        grid=(T // tm,),
        in_specs=[pl.BlockSpec((tm, D), lambda i: (i, 0)),
                  pl.BlockSpec((1, D), lambda i: (0, 0))],
        out_specs=pl.BlockSpec((tm, D), lambda i: (i, 0)),
        compiler_params=_params(("arbitrary",)),
        name="final_norm",
    )(x, g.reshape(1, D))


def _rot_cols(w):
    half = w.shape[-1] // 2
    return jnp.concatenate([-w[..., half:], w[..., :half]], axis=-1)


def _pad_lanes(w):
    pad = [(0, 0)] * (w.ndim - 1) + [(0, LANES - w.shape[-1])]
    return jnp.pad(w, pad)


def kernel(x, mem, positions, g_mix, g_ffn, g_final, w_in_a, g_v, w_sp, b_sp, g_kv, w_kv_a,
           g_kv_lat, w_in_b, g_q_lat, w_uq, w_uk, w_uv, g_mem, w_mem_kv, w_out, w_ffn_up,
           conv_w, conv_b, w_ffn_down):
    batch, seq, d_model = x.shape
    depth = g_mix.shape[0]
    n_a = w_in_a.shape[0]
    n_mem = mem.shape[1]
    T = batch * seq
    kv_rank = g_kv_lat.shape[0]
    q_rank = g_q_lat.shape[1]
    n_heads, nope_dim = w_uk.shape[2], w_uk.shape[3]
    rope_dim = w_kv_a.shape[1] - kv_rank
    mem_w = w_mem_kv.shape[2] // 2
    g_w = g_v.shape[1]
    scale = (nope_dim + rope_dim) ** -0.5 * np.log2(np.e)

    xs = x.reshape(T, d_model)
    mems = mem.reshape(batch * n_mem, d_model)

    inv = 1.0 / (ROPE_THETA ** (jnp.arange(0, rope_dim, 2, dtype=F32) / rope_dim))
    inv_row = _pad_lanes(jnp.concatenate([inv, inv])).reshape(1, LANES)
    cos_t, sin_t = rope_tables(positions.reshape(T, 1), inv_row)

    c_kv = kr_pad = None
    for l in range(depth):
        if l == n_a:
            w_kr = w_kv_a[:, kv_rank:]
            w_kv_cat = jnp.concatenate(
                [w_kv_a[:, :kv_rank], _pad_lanes(w_kr), _pad_lanes(_rot_cols(w_kr))], axis=1)
            kv_raw = norm_matmul(xs, g_kv, w_kv_cat.astype(BF16), F32,
                                 tm=1024, tn=w_kv_cat.shape[1], name="kv_proj")
            c_kv, kr_pad = kv_post(kv_raw, g_kv_lat, cos_t, sin_t)

        kvm = norm_matmul(mems, g_mem[l], w_mem_kv[l].astype(BF16), BF16,
                          tm=1024, tn=2 * mem_w, name="mem_kv_proj")
        if l < n_a:
            w_in = w_in_a[l].astype(BF16)
            z = norm_matmul(xs, g_mix[l], w_in, BF16, tm=1024, tn=w_in.shape[1] // 4,
                            name="in_proj_a")
            b_full = jnp.broadcast_to(b_sp[l][:, :, None], w_sp[l].shape)
            main = gmlp_gate(z, g_v[l], w_sp[l], b_full)
            q_col_block = (2 * g_w) // mem_w
        else:
            j = l - n_a
            w_in = w_in_b[j].astype(BF16)
            z = norm_matmul(xs, g_mix[l], w_in, BF16, tm=1024, tn=w_in.shape[1],
                            name="in_proj_b")
            wq = w_uq[j].reshape(q_rank, n_heads, nope_dim + rope_dim)
            wq_rope = wq[..., nope_dim:]
            w_main = jnp.concatenate(
                [wq[..., :nope_dim], _pad_lanes(wq_rope)], axis=-1).reshape(q_rank, -1)
            w_rot = _pad_lanes(_rot_cols(wq_rope)).reshape(q_rank, -1)
            q = q_prep(z, g_q_lat[j], w_main.astype(BF16), w_rot.astype(BF16), cos_t, sin_t,
                       batch, n_heads, scale)
            w_kv_h = jnp.concatenate([w_uk[j], w_uv[j]], axis=-1)
            w_kv_h = jnp.transpose(w_kv_h, (1, 0, 2)).astype(BF16)
            main = mla_attn(q, c_kv, kr_pad, w_kv_h)
            q_col_block = q_rank // mem_w
        mo = mem_attn(z, q_col_block, kvm, batch)
        xs = mix_out(xs, main, mo, w_out[l].astype(BF16))
        xs = conv_ffn(xs, g_ffn[l], w_ffn_up[l].astype(BF16),
                      _interleave_gate_value(conv_w[l], FFN_TF),
                      _interleave_gate_value(conv_b[l], FFN_TF),
                      w_ffn_down[l].astype(BF16), seq, tf=FFN_TF)
    return final_norm(xs, g_final).reshape(batch, seq, d_model)
```

```python
import functools

import jax
import jax.numpy as jnp
import numpy as np
from jax import lax
from jax.experimental import pallas as pl
from jax.experimental.pallas import tpu as pltpu

EPS = 1e-6
ROPE_THETA = 10000.0
MEM_HEADS = 4
CHUNK = 128
LANES = 128
SUBLANES = 8
CONV_W = 3
FFN_TF = 512
VMEM_LIMIT = 56 * 1024 * 1024
FFN_VMEM_LIMIT = 62 * 1024 * 1024

F32 = jnp.float32
BF16 = jnp.bfloat16


def _params(semantics, vmem=VMEM_LIMIT):
    return pltpu.CompilerParams(dimension_semantics=semantics, vmem_limit_bytes=vmem)


def _rms(x, g):
    ms = jnp.mean(x * x, axis=-1, keepdims=True)
    return x * lax.rsqrt(ms + EPS) * g


def _gelu(x):
    return 0.5 * x * (1.0 + lax.erf(x * np.float32(np.sqrt(0.5))))


def _rope_table_kernel(pos_ref, inv_ref, cos_ref, sin_ref):
    ang = pos_ref[...].astype(F32) * inv_ref[...]
    cos_ref[...] = jnp.cos(ang)
    sin_ref[...] = jnp.sin(ang)


def rope_tables(pos_col, inv_row, tm=2048):
    T = pos_col.shape[0]
    return pl.pallas_call(
        _rope_table_kernel,
        out_shape=(jax.ShapeDtypeStruct((T, LANES), F32),) * 2,
        grid=(T // tm,),
        in_specs=[pl.BlockSpec((tm, 1), lambda i: (i, 0)),
                  pl.BlockSpec((1, LANES), lambda i: (0, 0))],
        out_specs=(pl.BlockSpec((tm, LANES), lambda i: (i, 0)),) * 2,
        compiler_params=_params(("arbitrary",)),
        name="rope_tables",
    )(pos_col, inv_row)


def _norm_matmul_kernel(x_ref, g_ref, w_ref, o_ref, h_ref):
    @pl.when(pl.program_id(1) == 0)
    def _():
        h_ref[...] = _rms(x_ref[...], g_ref[...]).astype(h_ref.dtype)

    o_ref[...] = jnp.dot(h_ref[...], w_ref[...],
                         preferred_element_type=F32).astype(o_ref.dtype)


def norm_matmul(x, g, w, out_dtype, tm, tn, name):
    T, K = x.shape
    N = w.shape[1]
    tm = min(tm, T)
    return pl.pallas_call(
        _norm_matmul_kernel,
        out_shape=jax.ShapeDtypeStruct((T, N), out_dtype),
        grid=(T // tm, N // tn),
        in_specs=[pl.BlockSpec((tm, K), lambda i, j: (i, 0)),
                  pl.BlockSpec((1, K), lambda i, j: (0, 0)),
                  pl.BlockSpec((K, tn), lambda i, j: (0, j))],
        out_specs=pl.BlockSpec((tm, tn), lambda i, j: (i, j)),
        scratch_shapes=[pltpu.VMEM((tm, K), BF16)],
        compiler_params=_params(("arbitrary", "arbitrary")),
        name=name,
    )(x, g.reshape(1, K), w)


def _kv_post_kernel(kv_ref, g_ref, cos_ref, sin_ref, ckv_ref, kr_ref, *, rank):
    lat = kv_ref[:, :rank]
    ckv_ref[...] = _rms(lat, g_ref[...]).astype(ckv_ref.dtype)
    kr = kv_ref[:, rank:rank + LANES] * cos_ref[...]
    kr = kr + kv_ref[:, rank + LANES:] * sin_ref[...]
    kr_ref[...] = kr.astype(kr_ref.dtype)


def kv_post(kv_raw, g_lat, cos_t, sin_t, tm=1024):
    T, N = kv_raw.shape
    rank = N - 2 * LANES
    return pl.pallas_call(
        functools.partial(_kv_post_kernel, rank=rank),
        out_shape=(jax.ShapeDtypeStruct((T, rank), BF16),
                   jax.ShapeDtypeStruct((T, LANES), BF16)),
        grid=(T // tm,),
        in_specs=[pl.BlockSpec((tm, N), lambda i: (i, 0)),
                  pl.BlockSpec((1, rank), lambda i: (0, 0)),
                  pl.BlockSpec((tm, LANES), lambda i: (i, 0)),
                  pl.BlockSpec((tm, LANES), lambda i: (i, 0))],
        out_specs=(pl.BlockSpec((tm, rank), lambda i: (i, 0)),
                   pl.BlockSpec((tm, LANES), lambda i: (i, 0))),
        compiler_params=_params(("arbitrary",)),
        name="kv_post",
    )(kv_raw, g_lat.reshape(1, rank), cos_t, sin_t)


def _gmlp_kernel(u_ref, v_ref, gv_ref, w_ref, b_ref, o_ref, *, n_groups, tm):
    row = lax.broadcasted_iota(jnp.int32, (CHUNK, CHUNK), 0)
    col = lax.broadcasted_iota(jnp.int32, (CHUNK, CHUNK), 1)
    causal = row >= col
    w_tril = [jnp.where(causal, w_ref[g], 0.0).astype(BF16) for g in range(n_groups)]
    for c in range(tm // CHUNK):
        rows = slice(c * CHUNK, (c + 1) * CHUNK)
        ug = _gelu(u_ref[rows, :].astype(F32))
        vn = _rms(_gelu(v_ref[rows, :].astype(F32)), gv_ref[...]).astype(BF16)
        for g in range(n_groups):
            cols = slice(g * CHUNK, (g + 1) * CHUNK)
            sv = jnp.dot(w_tril[g], vn[:, cols], preferred_element_type=F32) + b_ref[g]
            o_ref[rows, cols] = (ug[:, cols] * sv).astype(o_ref.dtype)


def gmlp_gate(z, g_v, w_sp, b_full, tm=512):
    T = z.shape[0]
    n_groups = w_sp.shape[0]
    gw = n_groups * CHUNK
    return pl.pallas_call(
        functools.partial(_gmlp_kernel, n_groups=n_groups, tm=tm),
        out_shape=jax.ShapeDtypeStruct((T, gw), BF16),
        grid=(T // tm,),
        in_specs=[pl.BlockSpec((tm, gw), lambda i: (i, 0)),
                  pl.BlockSpec((tm, gw), lambda i: (i, 1)),
                  pl.BlockSpec((1, gw), lambda i: (0, 0)),
                  pl.BlockSpec((n_groups, CHUNK, CHUNK), lambda i: (0, 0, 0)),
                  pl.BlockSpec((n_groups, CHUNK, CHUNK), lambda i: (0, 0, 0))],
        out_specs=pl.BlockSpec((tm, gw), lambda i: (i, 0)),
        compiler_params=_params(("arbitrary",)),
        name="gmlp_gate",
    )(z, z, g_v.reshape(1, gw), w_sp, b_full)


def _mem_attn_kernel(q_ref, kv_ref, o_ref, *, n_heads, head_dim):
    width = n_heads * head_dim
    scale = np.float32(head_dim ** -0.5)
    for h in range(n_heads):
        cols = slice(h * head_dim, (h + 1) * head_dim)
        q = q_ref[:, cols]
        k = kv_ref[:, cols]
        v = kv_ref[:, width + h * head_dim:width + (h + 1) * head_dim]
        s = lax.dot_general(q, k, (((1,), (1,)), ((), ())),
                            preferred_element_type=F32) * scale
        m = jnp.max(s, axis=-1, keepdims=True)
        p = jnp.exp(s - m)
        l = jnp.sum(p, axis=-1, keepdims=True)
        o = jnp.dot(p.astype(BF16), v, preferred_element_type=F32) / l
        o_ref[:, cols] = o.astype(o_ref.dtype)


def mem_attn(z, q_col_block, kvm, batch, tm=1024):
    T = z.shape[0]
    n_mem = kvm.shape[0] // batch
    width = kvm.shape[1] // 2
    per_b = T // batch // tm
    return pl.pallas_call(
        functools.partial(_mem_attn_kernel, n_heads=MEM_HEADS, head_dim=width // MEM_HEADS),
        out_shape=jax.ShapeDtypeStruct((T, width), BF16),
        grid=(batch, per_b),
        in_specs=[pl.BlockSpec((tm, width), lambda b, i: (b * per_b + i, q_col_block)),
                  pl.BlockSpec((n_mem, 2 * width), lambda b, i: (b, 0))],
        out_specs=pl.BlockSpec((tm, width), lambda b, i: (b * per_b + i, 0)),
        compiler_params=_params(("arbitrary", "arbitrary")),
        name="mem_attn",
    )(z, kvm)


def _mix_out_kernel(x_ref, a_ref, b_ref, w_ref, o_ref):
    ka = a_ref.shape[1]
    acc = jnp.dot(a_ref[...], w_ref[:ka, :], preferred_element_type=F32)
    acc = acc + jnp.dot(b_ref[...], w_ref[ka:, :], preferred_element_type=F32)
    o_ref[...] = x_ref[...] + acc


def mix_out(x, main, mo, w_out, tm=512):
    T, D = x.shape
    ka, kb = main.shape[1], mo.shape[1]
    return pl.pallas_call(
        _mix_out_kernel,
        out_shape=jax.ShapeDtypeStruct((T, D), F32),
        grid=(T // tm,),
        in_specs=[pl.BlockSpec((tm, D), lambda i: (i, 0)),
                  pl.BlockSpec((tm, ka), lambda i: (i, 0)),
                  pl.BlockSpec((tm, kb), lambda i: (i, 0)),
                  pl.BlockSpec((ka + kb, D), lambda i: (0, 0))],
        out_specs=pl.BlockSpec((tm, D), lambda i: (i, 0)),
        compiler_params=_params(("arbitrary",)),
        name="mix_out",
    )(x, main, mo, w_out)


def _conv_ffn_kernel(x_ref, g_ref, wg_ref, wv_ref, cw_ref, cb_ref, wd_ref, o_ref,
                     h_ref, a_ref, carry_ref, *, tm, rc, tf, tiles_per_seq, norm_out):
    i = pl.program_id(0)
    j = pl.program_id(1)

    @pl.when(jnp.logical_and(i == 0, j == 0))
    def _():
        carry_ref[...] = jnp.zeros(carry_ref.shape, F32)

    @pl.when(j == 0)
    def _():
        for r in range(tm // rc):
            rows = slice(r * rc, (r + 1) * rc)
            x = x_ref[rows, :]
            h_ref[rows, :] = _rms(x, g_ref[0:1, :]).astype(h_ref.dtype)
            o_ref[rows, :] = x

    first_in_seq = (i % tiles_per_seq) == 0
    a_ref[:SUBLANES, :] = jnp.where(first_in_seq, 0.0, carry_ref[j])
    cw = cw_ref[...]
    cb = cb_ref[...]
    def up(r):
        lo = SUBLANES + r * rc
        h = h_ref[r * rc:(r + 1) * rc, :]
        a_ref[lo:lo + rc, :tf] = jnp.dot(h, wg_ref[...], preferred_element_type=F32)
        a_ref[lo:lo + rc, tf:] = jnp.dot(h, wv_ref[...], preferred_element_type=F32)

    def gate_down(r):
        lo = SUBLANES + r * rc
        c = cb + a_ref[lo:lo + rc, :] * cw[CONV_W - 1:CONV_W, :]
        for k in range(CONV_W - 1):
            lag = CONV_W - 1 - k
            c = c + a_ref[lo - lag:lo - lag + rc, :] * cw[k:k + 1, :]
        cg = c[:, :tf]
        gated = (cg * jax.nn.sigmoid(cg) * c[:, tf:]).astype(BF16)
        o_ref[r * rc:(r + 1) * rc, :] += jnp.dot(gated, wd_ref[...],
                                                 preferred_element_type=F32)

    n_chunks = tm // rc
    up(0)
    for r in range(n_chunks):
        if r + 1 < n_chunks:
            up(r + 1)
        gate_down(r)
    carry_ref[j] = a_ref[tm:, :]

    if norm_out:
        @pl.when(j == pl.num_programs(1) - 1)
        def _():
            for r in range(n_chunks):
                rows = slice(r * rc, (r + 1) * rc)
                o_ref[rows, :] = _rms(o_ref[rows, :], g_ref[1:2, :])


def conv_ffn(x, g, w_up, conv_w, conv_b, w_down, seq, out_gain=None, tm=1024, tf=512, rc=512):
    T, D = x.shape
    d_ff = w_down.shape[0]
    nff = d_ff // tf
    gains = g.reshape(1, D) if out_gain is None else jnp.stack([g, out_gain])
    kern = functools.partial(_conv_ffn_kernel, tm=tm, rc=rc, tf=tf, tiles_per_seq=seq // tm,
                             norm_out=out_gain is not None)
    return pl.pallas_call(
        kern,
        out_shape=jax.ShapeDtypeStruct((T, D), F32),
        grid=(T // tm, nff),
        in_specs=[pl.BlockSpec((tm, D), lambda i, j: (i, 0)),
                  pl.BlockSpec(gains.shape, lambda i, j: (0, 0)),
                  pl.BlockSpec((D, tf), lambda i, j: (0, j)),
                  pl.BlockSpec((D, tf), lambda i, j: (0, nff + j)),
                  pl.BlockSpec((CONV_W, 2 * tf), lambda i, j: (0, j)),
                  pl.BlockSpec((1, 2 * tf), lambda i, j: (0, j)),
                  pl.BlockSpec((tf, D), lambda i, j: (j, 0))],
        out_specs=pl.BlockSpec((tm, D), lambda i, j: (i, 0)),
        scratch_shapes=[pltpu.VMEM((tm, D), BF16),
                        pltpu.VMEM((tm + SUBLANES, 2 * tf), F32),
                        pltpu.VMEM((nff, SUBLANES, 2 * tf), F32)],
        compiler_params=_params(("arbitrary", "arbitrary"), vmem=FFN_VMEM_LIMIT),
        name="conv_ffn",
    )(x, gains, w_up, w_up, conv_w, conv_b.reshape(1, -1), w_down)


def _interleave_gate_value(w, tf):
    lead = w.shape[:-1]
    d_ff = w.shape[-1] // 2
    w = w.reshape(*lead, 2, d_ff // tf, tf)
    return jnp.swapaxes(w, -3, -2).reshape(*lead, 2 * d_ff)


def _q_prep_kernel(q_ref, g_ref, wm_ref, wr_ref, cos_ref, sin_ref, o_ref, *, n_heads, scale):
    qn = _rms(q_ref[...].astype(F32), g_ref[...]).astype(BF16)
    a = jnp.dot(qn, wm_ref[...], preferred_element_type=F32)
    r = jnp.dot(qn, wr_ref[...], preferred_element_type=F32)
    cos = cos_ref[...]
    sin = sin_ref[...]
    for h in range(n_heads):
        base = 2 * LANES * h
        o_ref[0, h, :, :LANES] = (a[:, base:base + LANES] * scale).astype(o_ref.dtype)
        rope = a[:, base + LANES:base + 2 * LANES] * cos + r[:, h * LANES:(h + 1) * LANES] * sin
        o_ref[0, h, :, LANES:] = (rope * scale).astype(o_ref.dtype)


def q_prep(z, g_q, w_main, w_rot, cos_t, sin_t, batch, n_heads, scale, tm=512):
    T = z.shape[0]
    seq = T // batch
    rank = w_main.shape[0]
    per_b = seq // tm
    kern = functools.partial(_q_prep_kernel, n_heads=n_heads, scale=np.float32(scale))
    return pl.pallas_call(
        kern,
        out_shape=jax.ShapeDtypeStruct((batch, n_heads, seq, 2 * LANES), BF16),
        grid=(batch, per_b),
        in_specs=[pl.BlockSpec((tm, rank), lambda b, i: (b * per_b + i, 0)),
                  pl.BlockSpec((1, rank), lambda b, i: (0, 0)),
                  pl.BlockSpec(w_main.shape, lambda b, i: (0, 0)),
                  pl.BlockSpec(w_rot.shape, lambda b, i: (0, 0)),
                  pl.BlockSpec((tm, LANES), lambda b, i: (b * per_b + i, 0)),
                  pl.BlockSpec((tm, LANES), lambda b, i: (b * per_b + i, 0))],
        out_specs=pl.BlockSpec((1, n_heads, tm, 2 * LANES), lambda b, i: (b, 0, i, 0)),
        compiler_params=_params(("arbitrary", "arbitrary")),
        name="q_prep",
    )(z, g_q.reshape(1, rank), w_main, w_rot, cos_t, sin_t)


def _mla_attn_kernel(q_ref, ckv_ref, kr_ref, wkv_ref, o_ref, k_ref, vt_ref, acc_ref,
                     sa_ref, sb_ref, *, tq, tk, hp):
    i = pl.program_id(2)
    n_kb = k_ref.shape[1]
    v_dim = vt_ref.shape[2]

    @pl.when(i == 0)
    def _():
        for jb in range(n_kb):
            rows = slice(jb * tk, (jb + 1) * tk)
            for hh in range(hp):
                kv = jnp.dot(ckv_ref[rows, :], wkv_ref[hh], preferred_element_type=F32)
                k_ref[hh, jb, :, :LANES] = kv[:, :LANES].astype(k_ref.dtype)
                k_ref[hh, jb, :, LANES:] = kr_ref[rows, :]
                vt_ref[hh, jb] = kv[:, LANES:].T.astype(vt_ref.dtype)

    acc_ref[...] = jnp.zeros(acc_ref.shape, F32)

    def scores(jb, dst_ref, q_lo=0):
        for hh in range(hp):
            dst_ref[hh, :, q_lo:] = lax.dot_general(
                k_ref[hh, jb], q_ref[0, hh, q_lo:, :], (((1,), (1,)), ((), ())),
                preferred_element_type=F32)

    def consume(jb, src_ref, carry, diag_offset, q_lo=0):
        nq_cols = tq - q_lo
        out = []
        for hh in range(hp):
            m_all, l_all = carry[hh]
            m, l = m_all[:, q_lo:], l_all[:, q_lo:]
            st = src_ref[hh, :, q_lo:]
            if diag_offset is not None:
                k_pos = lax.broadcasted_iota(jnp.int32, (tk, nq_cols), 0) + diag_offset
                q_pos = lax.broadcasted_iota(jnp.int32, (tk, nq_cols), 1)
                st = jnp.where(k_pos <= q_pos, st, -jnp.inf)
            m_new = jnp.maximum(m, jnp.max(st, axis=0, keepdims=True))
            alpha = jnp.exp2(m - m_new)
            p = jnp.exp2(st - m_new)
            l = alpha * l + jnp.sum(p, axis=0, keepdims=True)
            pv = jnp.dot(vt_ref[hh, jb], p.astype(BF16), preferred_element_type=F32)
            acc_ref[hh, :, q_lo:] = alpha * acc_ref[hh, :, q_lo:] + pv
            if q_lo:
                m_new = jnp.concatenate([m_all[:, :q_lo], m_new], axis=1)
                l = jnp.concatenate([l_all[:, :q_lo], l], axis=1)
            out.append((m_new, l))
        return tuple(out)

    def pair(t, carry):
        scores(2 * t + 1, sb_ref)
        carry = consume(2 * t, sa_ref, carry, None)
        scores(2 * t + 2, sa_ref)
        return consume(2 * t + 1, sb_ref, carry, None)

    init = tuple((jnp.full((1, tq), -jnp.inf, F32), jnp.zeros((1, tq), F32))
                 for _ in range(hp))
    scores(0, sa_ref)
    carry = lax.fori_loop(0, i, pair, init)
    scores(2 * i + 1, sb_ref, q_lo=tk)
    carry = consume(2 * i, sa_ref, carry, 0)
    carry = consume(2 * i + 1, sb_ref, carry, 0, q_lo=tk)
    for hh in range(hp):
        o = acc_ref[hh] / carry[hh][1]
        o_ref[:, hh * v_dim:(hh + 1) * v_dim] = o.T.astype(o_ref.dtype)


def mla_attn(q, c_kv, kr_pad, w_kv, tq=1024, hp=2):
    tk = tq // 2
    batch, n_heads, seq, qk_dim = q.shape
    rank = c_kv.shape[1]
    v_dim = w_kv.shape[2] - LANES
    nq = seq // tq
    return pl.pallas_call(
        functools.partial(_mla_attn_kernel, tq=tq, tk=tk, hp=hp),
        out_shape=jax.ShapeDtypeStruct((batch * seq, n_heads * v_dim), BF16),
        grid=(batch, n_heads // hp, nq),
        in_specs=[pl.BlockSpec((1, hp, tq, qk_dim), lambda b, h, i: (b, h, i, 0)),
                  pl.BlockSpec((seq, rank), lambda b, h, i: (b, 0)),
                  pl.BlockSpec((seq, LANES), lambda b, h, i: (b, 0)),
                  pl.BlockSpec((hp, rank, LANES + v_dim), lambda b, h, i: (h, 0, 0))],
        out_specs=pl.BlockSpec((tq, hp * v_dim), lambda b, h, i: (b * nq + i, h)),
        scratch_shapes=[pltpu.VMEM((hp, seq // tk, tk, qk_dim), BF16),
                        pltpu.VMEM((hp, seq // tk, v_dim, tk), BF16),
                        pltpu.VMEM((hp, v_dim, tq), F32),
                        pltpu.VMEM((hp, tk, tq), F32),
                        pltpu.VMEM((hp, tk, tq), F32)],
        compiler_params=_params(("arbitrary", "arbitrary", "arbitrary")),
        name="mla_attn",
    )(q, c_kv, kr_pad, w_kv)


def _rot_cols(w):
    half = w.shape[-1] // 2
    return jnp.concatenate([-w[..., half:], w[..., :half]], axis=-1)


def _pad_lanes(w):
    pad = [(0, 0)] * (w.ndim - 1) + [(0, LANES - w.shape[-1])]
    return jnp.pad(w, pad)


def kernel(x, mem, positions, g_mix, g_ffn, g_final, w_in_a, g_v, w_sp, b_sp, g_kv, w_kv_a,
           g_kv_lat, w_in_b, g_q_lat, w_uq, w_uk, w_uv, g_mem, w_mem_kv, w_out, w_ffn_up,
           conv_w, conv_b, w_ffn_down):
    batch, seq, d_model = x.shape
    depth = g_mix.shape[0]
    n_a = w_in_a.shape[0]
    n_mem = mem.shape[1]
    T = batch * seq
    kv_rank = g_kv_lat.shape[0]
    q_rank = g_q_lat.shape[1]
    n_heads, nope_dim = w_uk.shape[2], w_uk.shape[3]
    rope_dim = w_kv_a.shape[1] - kv_rank
    mem_w = w_mem_kv.shape[2] // 2
    g_w = g_v.shape[1]
    scale = (nope_dim + rope_dim) ** -0.5 * np.log2(np.e)

    xs = x.reshape(T, d_model)
    mems = mem.reshape(batch * n_mem, d_model)

    inv = 1.0 / (ROPE_THETA ** (jnp.arange(0, rope_dim, 2, dtype=F32) / rope_dim))
    inv_row = _pad_lanes(jnp.concatenate([inv, inv])).reshape(1, LANES)
    cos_t, sin_t = rope_tables(positions.reshape(T, 1), inv_row)

    c_kv = kr_pad = None
    for l in range(depth):
        if l == n_a:
            w_kr = w_kv_a[:, kv_rank:]
            w_kv_cat = jnp.concatenate(
                [w_kv_a[:, :kv_rank], _pad_lanes(w_kr), _pad_lanes(_rot_cols(w_kr))], axis=1)
            kv_raw = norm_matmul(xs, g_kv, w_kv_cat.astype(BF16), F32,
                                 tm=1024, tn=w_kv_cat.shape[1], name="kv_proj")
            c_kv, kr_pad = kv_post(kv_raw, g_kv_lat, cos_t, sin_t)

        kvm = norm_matmul(mems, g_mem[l], w_mem_kv[l].astype(BF16), BF16,
                          tm=1024, tn=2 * mem_w, name="mem_kv_proj")
        if l < n_a:
            w_in = w_in_a[l].astype(BF16)
            z = norm_matmul(xs, g_mix[l], w_in, BF16, tm=1024, tn=w_in.shape[1] // 4,
                            name="in_proj_a")
            b_full = jnp.broadcast_to(b_sp[l][:, :, None], w_sp[l].shape)
            main = gmlp_gate(z, g_v[l], w_sp[l], b_full)
            q_col_block = (2 * g_w) // mem_w
        else:
            j = l - n_a
            w_in = w_in_b[j].astype(BF16)
            z = norm_matmul(xs, g_mix[l], w_in, BF16, tm=1024, tn=w_in.shape[1],
                            name="in_proj_b")
            wq = w_uq[j].reshape(q_rank, n_heads, nope_dim + rope_dim)
            wq_rope = wq[..., nope_dim:]
            w_main = jnp.concatenate(
                [wq[..., :nope_dim], _pad_lanes(wq_rope)], axis=-1).reshape(q_rank, -1)
            w_rot = _pad_lanes(_rot_cols(wq_rope)).reshape(q_rank, -1)
            q = q_prep(z, g_q_lat[j], w_main.astype(BF16), w_rot.astype(BF16), cos_t, sin_t,
                       batch, n_heads, scale)
            w_kv_h = jnp.concatenate([w_uk[j], w_uv[j]], axis=-1)
            w_kv_h = jnp.transpose(w_kv_h, (1, 0, 2)).astype(BF16)
            main = mla_attn(q, c_kv, kr_pad, w_kv_h)
            q_col_block = q_rank // mem_w
        mo = mem_attn(z, q_col_block, kvm, batch)
        xs = mix_out(xs, main, mo, w_out[l].astype(BF16))
        xs = conv_ffn(xs, g_ffn[l], w_ffn_up[l].astype(BF16),
                      _interleave_gate_value(conv_w[l], FFN_TF),
                      _interleave_gate_value(conv_b[l], FFN_TF),
                      w_ffn_down[l].astype(BF16), seq, tf=FFN_TF,
                      out_gain=g_final if l == depth - 1 else None)
    return xs.reshape(batch, seq, d_model)
```

```python
import functools

import jax
import jax.numpy as jnp
import numpy as np
from jax import lax
from jax.experimental import pallas as pl
from jax.experimental.pallas import tpu as pltpu

EPS = 1e-6
ROPE_THETA = 10000.0
MEM_HEADS = 4
CHUNK = 128
LANES = 128
SUBLANES = 8
CONV_W = 3
FFN_TF = 512
VMEM_LIMIT = 56 * 1024 * 1024
FFN_VMEM_LIMIT = 62 * 1024 * 1024

F32 = jnp.float32
BF16 = jnp.bfloat16


def _params(semantics, vmem=VMEM_LIMIT):
    return pltpu.CompilerParams(dimension_semantics=semantics, vmem_limit_bytes=vmem)


def _rms(x, g):
    ms = jnp.mean(x * x, axis=-1, keepdims=True)
    return x * lax.rsqrt(ms + EPS) * g


def _gelu(x):
    return 0.5 * x * (1.0 + lax.erf(x * np.float32(np.sqrt(0.5))))


def _rope_table_kernel(pos_ref, inv_ref, cos_ref, sin_ref):
    ang = pos_ref[...].astype(F32) * inv_ref[...]
    cos_ref[...] = jnp.cos(ang)
    sin_ref[...] = jnp.sin(ang)


def rope_tables(pos_col, inv_row, tm=2048):
    T = pos_col.shape[0]
    return pl.pallas_call(
        _rope_table_kernel,
        out_shape=(jax.ShapeDtypeStruct((T, LANES), F32),) * 2,
        grid=(T // tm,),
        in_specs=[pl.BlockSpec((tm, 1), lambda i: (i, 0)),
                  pl.BlockSpec((1, LANES), lambda i: (0, 0))],
        out_specs=(pl.BlockSpec((tm, LANES), lambda i: (i, 0)),) * 2,
        compiler_params=_params(("arbitrary",)),
        name="rope_tables",
    )(pos_col, inv_row)


def _norm_matmul_kernel(x_ref, g_ref, w_ref, o_ref, h_ref, *, rc):
    n_chunks = x_ref.shape[0] // rc

    def norm(r):
        rows = slice(r * rc, (r + 1) * rc)
        h_ref[rows, :] = _rms(x_ref[rows, :], g_ref[...]).astype(h_ref.dtype)

    norm(0)
    for r in range(n_chunks):
        if r + 1 < n_chunks:
            norm(r + 1)
        rows = slice(r * rc, (r + 1) * rc)
        o_ref[rows, :] = jnp.dot(h_ref[rows, :], w_ref[...],
                                 preferred_element_type=F32).astype(o_ref.dtype)


def norm_matmul(x, g, w, layer, out_dtype, tm, name, rc=256):
    T, K = x.shape
    N = w.shape[2]
    tm = min(tm, T)
    return pl.pallas_call(
        functools.partial(_norm_matmul_kernel, rc=rc),
        out_shape=jax.ShapeDtypeStruct((T, N), out_dtype),
        grid=(T // tm,),
        in_specs=[pl.BlockSpec((tm, K), lambda i: (i, 0)),
                  pl.BlockSpec((1, K), lambda i: (0, 0)),
                  pl.BlockSpec((None, K, N), lambda i: (layer, 0, 0),
                               pipeline_mode=pl.Buffered(1))],
        out_specs=pl.BlockSpec((tm, N), lambda i: (i, 0)),
        scratch_shapes=[pltpu.VMEM((tm, K), BF16)],
        compiler_params=_params(("arbitrary",), vmem=FFN_VMEM_LIMIT),
        name=name,
    )(x, g.reshape(1, K), w)


def _kv_post_kernel(kv_ref, g_ref, cos_ref, sin_ref, ckv_ref, kr_ref, *, rank):
    lat = kv_ref[:, :rank]
    ckv_ref[...] = _rms(lat, g_ref[...]).astype(ckv_ref.dtype)
    kr = kv_ref[:, rank:rank + LANES] * cos_ref[...]
    kr = kr + kv_ref[:, rank + LANES:] * sin_ref[...]
    kr_ref[...] = kr.astype(kr_ref.dtype)


def kv_post(kv_raw, g_lat, cos_t, sin_t, tm=1024):
    T, N = kv_raw.shape
    rank = N - 2 * LANES
    return pl.pallas_call(
        functools.partial(_kv_post_kernel, rank=rank),
        out_shape=(jax.ShapeDtypeStruct((T, rank), BF16),
                   jax.ShapeDtypeStruct((T, LANES), BF16)),
        grid=(T // tm,),
        in_specs=[pl.BlockSpec((tm, N), lambda i: (i, 0)),
                  pl.BlockSpec((1, rank), lambda i: (0, 0)),
                  pl.BlockSpec((tm, LANES), lambda i: (i, 0)),
                  pl.BlockSpec((tm, LANES), lambda i: (i, 0))],
        out_specs=(pl.BlockSpec((tm, rank), lambda i: (i, 0)),
                   pl.BlockSpec((tm, LANES), lambda i: (i, 0))),
        compiler_params=_params(("arbitrary",)),
        name="kv_post",
    )(kv_raw, g_lat.reshape(1, rank), cos_t, sin_t)


def _gmlp_kernel(u_ref, v_ref, gv_ref, w_ref, b_ref, o_ref, *, n_groups, tm):
    row = lax.broadcasted_iota(jnp.int32, (CHUNK, CHUNK), 0)
    col = lax.broadcasted_iota(jnp.int32, (CHUNK, CHUNK), 1)
    causal = row >= col
    w_tril = [jnp.where(causal, w_ref[g], 0.0).astype(BF16) for g in range(n_groups)]
    for c in range(tm // CHUNK):
        rows = slice(c * CHUNK, (c + 1) * CHUNK)
        ug = _gelu(u_ref[rows, :].astype(F32))
        vn = _rms(_gelu(v_ref[rows, :].astype(F32)), gv_ref[...]).astype(BF16)
        for g in range(n_groups):
            cols = slice(g * CHUNK, (g + 1) * CHUNK)
            sv = jnp.dot(w_tril[g], vn[:, cols], preferred_element_type=F32) + b_ref[g]
            o_ref[rows, cols] = (ug[:, cols] * sv).astype(o_ref.dtype)


def gmlp_gate(z, g_v, w_sp, b_full, tm=512):
    T = z.shape[0]
    n_groups = w_sp.shape[0]
    gw = n_groups * CHUNK
    return pl.pallas_call(
        functools.partial(_gmlp_kernel, n_groups=n_groups, tm=tm),
        out_shape=jax.ShapeDtypeStruct((T, gw), BF16),
        grid=(T // tm,),
        in_specs=[pl.BlockSpec((tm, gw), lambda i: (i, 0)),
                  pl.BlockSpec((tm, gw), lambda i: (i, 1)),
                  pl.BlockSpec((1, gw), lambda i: (0, 0)),
                  pl.BlockSpec((n_groups, CHUNK, CHUNK), lambda i: (0, 0, 0)),
                  pl.BlockSpec((n_groups, CHUNK, CHUNK), lambda i: (0, 0, 0))],
        out_specs=pl.BlockSpec((tm, gw), lambda i: (i, 0)),
        compiler_params=_params(("arbitrary",)),
        name="gmlp_gate",
    )(z, z, g_v.reshape(1, gw), w_sp, b_full)


def _mix_out_kernel(x_ref, a_ref, q_ref, kv_ref, w_ref, o_ref, *, n_heads):
    ka = a_ref.shape[1]
    width = q_ref.shape[1]
    head_dim = width // n_heads
    scale = np.float32(head_dim ** -0.5)
    acc = jnp.dot(a_ref[...], w_ref[:ka, :], preferred_element_type=F32)
    mo = []
    for h in range(n_heads):
        cols = slice(h * head_dim, (h + 1) * head_dim)
        k = kv_ref[:, cols]
        v = kv_ref[:, width + h * head_dim:width + (h + 1) * head_dim]
        s = lax.dot_general(q_ref[:, cols], k, (((1,), (1,)), ((), ())),
                            preferred_element_type=F32) * scale
        m = jnp.max(s, axis=-1, keepdims=True)
        p = jnp.exp(s - m)
        l = jnp.sum(p, axis=-1, keepdims=True)
        o = jnp.dot(p.astype(BF16), v, preferred_element_type=F32) / l
        mo.append(o.astype(BF16))
    acc = acc + jnp.dot(jnp.concatenate(mo, axis=1), w_ref[ka:, :], preferred_element_type=F32)
    o_ref[...] = x_ref[...] + acc


def mix_out(x, main, z, q_col_block, kvm, w_out, layer, batch, tm=512):
    T, D = x.shape
    ka = main.shape[1]
    n_mem = kvm.shape[0] // batch
    width = kvm.shape[1] // 2
    per_b = T // batch // tm
    return pl.pallas_call(
        functools.partial(_mix_out_kernel, n_heads=MEM_HEADS),
        out_shape=jax.ShapeDtypeStruct((T, D), F32),
        grid=(batch, per_b),
        in_specs=[pl.BlockSpec((tm, D), lambda b, i: (b * per_b + i, 0)),
                  pl.BlockSpec((tm, ka), lambda b, i: (b * per_b + i, 0)),
                  pl.BlockSpec((tm, width), lambda b, i: (b * per_b + i, q_col_block)),
                  pl.BlockSpec((n_mem, 2 * width), lambda b, i: (b, 0)),
                  pl.BlockSpec((None, ka + width, D), lambda b, i: (layer, 0, 0))],
        out_specs=pl.BlockSpec((tm, D), lambda b, i: (b * per_b + i, 0)),
        compiler_params=_params(("arbitrary", "arbitrary")),
        name="mix_out",
    )(x, main, z, kvm, w_out)


def _conv_ffn_kernel(x_ref, g_ref, wg_ref, wv_ref, cw_ref, cb_ref, wd_ref, o_ref,
                     h_ref, a_ref, carry_ref, *, tm, rc, tf, tiles_per_seq, norm_out):
    i = pl.program_id(0)
    j = pl.program_id(1)

    @pl.when(jnp.logical_and(i == 0, j == 0))
    def _():
        carry_ref[...] = jnp.zeros(carry_ref.shape, F32)

    @pl.when(j == 0)
    def _():
        for r in range(tm // rc):
            rows = slice(r * rc, (r + 1) * rc)
            x = x_ref[rows, :]
            h_ref[rows, :] = _rms(x, g_ref[0:1, :]).astype(h_ref.dtype)
            o_ref[rows, :] = x

    first_in_seq = (i % tiles_per_seq) == 0
    a_ref[:SUBLANES, :] = jnp.where(first_in_seq, 0.0, carry_ref[j])
    cw = cw_ref[...]
    cb = cb_ref[...]
    def up(r):
        lo = SUBLANES + r * rc
        h = h_ref[r * rc:(r + 1) * rc, :]
        a_ref[lo:lo + rc, :tf] = jnp.dot(h, wg_ref[...], preferred_element_type=F32)
        a_ref[lo:lo + rc, tf:] = jnp.dot(h, wv_ref[...], preferred_element_type=F32)

    def gate_down(r):
        lo = SUBLANES + r * rc
        c = cb + a_ref[lo:lo + rc, :] * cw[CONV_W - 1:CONV_W, :]
        for k in range(CONV_W - 1):
            lag = CONV_W - 1 - k
            c = c + a_ref[lo - lag:lo - lag + rc, :] * cw[k:k + 1, :]
        cg = c[:, :tf]
        gated = (cg * jax.nn.sigmoid(cg) * c[:, tf:]).astype(BF16)
        o_ref[r * rc:(r + 1) * rc, :] += jnp.dot(gated, wd_ref[...],
                                                 preferred_element_type=F32)

    n_chunks = tm // rc
    up(0)
    for r in range(n_chunks):
        if r + 1 < n_chunks:
            up(r + 1)
        gate_down(r)
    carry_ref[j] = a_ref[tm:, :]

    if norm_out:
        @pl.when(j == pl.num_programs(1) - 1)
        def _():
            for r in range(n_chunks):
                rows = slice(r * rc, (r + 1) * rc)
                o_ref[rows, :] = _rms(o_ref[rows, :], g_ref[1:2, :])


def conv_ffn(x, g, w_up, conv_w, conv_b, w_down, layer, seq, out_gain=None,
             tm=1024, tf=512, rc=512):
    T, D = x.shape
    d_ff = w_down.shape[1]
    nff = d_ff // tf
    gains = g.reshape(1, D) if out_gain is None else jnp.stack([g, out_gain])
    kern = functools.partial(_conv_ffn_kernel, tm=tm, rc=rc, tf=tf, tiles_per_seq=seq // tm,
                             norm_out=out_gain is not None)
    return pl.pallas_call(
        kern,
        out_shape=jax.ShapeDtypeStruct((T, D), F32),
        grid=(T // tm, nff),
        in_specs=[pl.BlockSpec((tm, D), lambda i, j: (i, 0)),
                  pl.BlockSpec(gains.shape, lambda i, j: (0, 0)),
                  pl.BlockSpec((None, D, tf), lambda i, j: (layer, 0, j)),
                  pl.BlockSpec((None, D, tf), lambda i, j: (layer, 0, nff + j)),
                  pl.BlockSpec((CONV_W, 2 * tf), lambda i, j: (0, j)),
                  pl.BlockSpec((1, 2 * tf), lambda i, j: (0, j)),
                  pl.BlockSpec((None, tf, D), lambda i, j: (layer, j, 0))],
        out_specs=pl.BlockSpec((tm, D), lambda i, j: (i, 0)),
        scratch_shapes=[pltpu.VMEM((tm, D), BF16),
                        pltpu.VMEM((tm + SUBLANES, 2 * tf), F32),
                        pltpu.VMEM((nff, SUBLANES, 2 * tf), F32)],
        compiler_params=_params(("arbitrary", "arbitrary"), vmem=FFN_VMEM_LIMIT),
        name="conv_ffn",
    )(x, gains, w_up, w_up, conv_w, conv_b.reshape(1, -1), w_down)


def _interleave_gate_value(w, tf):
    lead = w.shape[:-1]
    d_ff = w.shape[-1] // 2
    w = w.reshape(*lead, 2, d_ff // tf, tf)
    return jnp.swapaxes(w, -3, -2).reshape(*lead, 2 * d_ff)


def _q_prep_kernel(q_ref, g_ref, wm_ref, wr_ref, cos_ref, sin_ref, o_ref, *, n_heads, scale):
    qn = _rms(q_ref[...].astype(F32), g_ref[...]).astype(BF16)
    a = jnp.dot(qn, wm_ref[...], preferred_element_type=F32)
    r = jnp.dot(qn, wr_ref[...], preferred_element_type=F32)
    cos = cos_ref[...]
    sin = sin_ref[...]
    for h in range(n_heads):
        base = 2 * LANES * h
        o_ref[0, h, :, :LANES] = (a[:, base:base + LANES] * scale).astype(o_ref.dtype)
        rope = a[:, base + LANES:base + 2 * LANES] * cos + r[:, h * LANES:(h + 1) * LANES] * sin
        o_ref[0, h, :, LANES:] = (rope * scale).astype(o_ref.dtype)


def q_prep(z, g_q, w_main, w_rot, cos_t, sin_t, batch, n_heads, scale, tm=512):
    T = z.shape[0]
    seq = T // batch
    rank = w_main.shape[0]
    per_b = seq // tm
    kern = functools.partial(_q_prep_kernel, n_heads=n_heads, scale=np.float32(scale))
    return pl.pallas_call(
        kern,
        out_shape=jax.ShapeDtypeStruct((batch, n_heads, seq, 2 * LANES), BF16),
        grid=(batch, per_b),
        in_specs=[pl.BlockSpec((tm, rank), lambda b, i: (b * per_b + i, 0)),
                  pl.BlockSpec((1, rank), lambda b, i: (0, 0)),
                  pl.BlockSpec(w_main.shape, lambda b, i: (0, 0)),
                  pl.BlockSpec(w_rot.shape, lambda b, i: (0, 0)),
                  pl.BlockSpec((tm, LANES), lambda b, i: (b * per_b + i, 0)),
                  pl.BlockSpec((tm, LANES), lambda b, i: (b * per_b + i, 0))],
        out_specs=pl.BlockSpec((1, n_heads, tm, 2 * LANES), lambda b, i: (b, 0, i, 0)),
        compiler_params=_params(("arbitrary", "arbitrary")),
        name="q_prep",
    )(z, g_q.reshape(1, rank), w_main, w_rot, cos_t, sin_t)


def _mla_attn_kernel(q_ref, ckv_ref, kr_ref, wkv_ref, o_ref, k_ref, vt_ref, acc_ref,
                     sa_ref, sb_ref, *, tq, tk, hp):
    i = pl.program_id(2)
    n_kb = k_ref.shape[1]
    v_dim = vt_ref.shape[2]

    @pl.when(i == 0)
    def _():
        for jb in range(n_kb):
            rows = slice(jb * tk, (jb + 1) * tk)
            for hh in range(hp):
                kv = jnp.dot(ckv_ref[rows, :], wkv_ref[hh], preferred_element_type=F32)
                k_ref[hh, jb, :, :LANES] = kv[:, :LANES].astype(k_ref.dtype)
                k_ref[hh, jb, :, LANES:] = kr_ref[rows, :]
                vt_ref[hh, jb] = kv[:, LANES:].T.astype(vt_ref.dtype)

    acc_ref[...] = jnp.zeros(acc_ref.shape, F32)

    def scores(jb, dst_ref, q_lo=0):
        for hh in range(hp):
            dst_ref[hh, :, q_lo:] = lax.dot_general(
                k_ref[hh, jb], q_ref[0, hh, q_lo:, :], (((1,), (1,)), ((), ())),
                preferred_element_type=F32)

    def consume(jb, src_ref, carry, diag_offset, q_lo=0):
        nq_cols = tq - q_lo
        out = []
        for hh in range(hp):
            m_all, l_all = carry[hh]
            m, l = m_all[:, q_lo:], l_all[:, q_lo:]
            st = src_ref[hh, :, q_lo:]
            if diag_offset is not None:
                k_pos = lax.broadcasted_iota(jnp.int32, (tk, nq_cols), 0) + diag_offset
                q_pos = lax.broadcasted_iota(jnp.int32, (tk, nq_cols), 1)
                st = jnp.where(k_pos <= q_pos, st, -jnp.inf)
            m_new = jnp.maximum(m, jnp.max(st, axis=0, keepdims=True))
            alpha = jnp.exp2(m - m_new)
            p = jnp.exp2(st - m_new)
            l = alpha * l + jnp.sum(p, axis=0, keepdims=True)
            pv = jnp.dot(vt_ref[hh, jb], p.astype(BF16), preferred_element_type=F32)
            acc_ref[hh, :, q_lo:] = alpha * acc_ref[hh, :, q_lo:] + pv
            if q_lo:
                m_new = jnp.concatenate([m_all[:, :q_lo], m_new], axis=1)
                l = jnp.concatenate([l_all[:, :q_lo], l], axis=1)
            out.append((m_new, l))
        return tuple(out)

    def pair(t, carry):
        scores(2 * t + 1, sb_ref)
        carry = consume(2 * t, sa_ref, carry, None)
        scores(2 * t + 2, sa_ref)
        return consume(2 * t + 1, sb_ref, carry, None)

    init = tuple((jnp.full((1, tq), -jnp.inf, F32), jnp.zeros((1, tq), F32))
                 for _ in range(hp))
    scores(0, sa_ref)
    carry = lax.fori_loop(0, i, pair, init)
    scores(2 * i + 1, sb_ref, q_lo=tk)
    carry = consume(2 * i, sa_ref, carry, 0)
    carry = consume(2 * i + 1, sb_ref, carry, 0, q_lo=tk)
    for hh in range(hp):
        o = acc_ref[hh] / carry[hh][1]
        o_ref[:, hh * v_dim:(hh + 1) * v_dim] = o.T.astype(o_ref.dtype)


def mla_attn(q, c_kv, kr_pad, w_kv, tq=1024, hp=2):
    tk = tq // 2
    batch, n_heads, seq, qk_dim = q.shape
    rank = c_kv.shape[1]
    v_dim = w_kv.shape[2] - LANES
    nq = seq // tq
    return pl.pallas_call(
        functools.partial(_mla_attn_kernel, tq=tq, tk=tk, hp=hp),
        out_shape=jax.ShapeDtypeStruct((batch * seq, n_heads * v_dim), BF16),
        grid=(batch, n_heads // hp, nq),
        in_specs=[pl.BlockSpec((1, hp, tq, qk_dim), lambda b, h, i: (b, h, i, 0)),
                  pl.BlockSpec((seq, rank), lambda b, h, i: (b, 0)),
                  pl.BlockSpec((seq, LANES), lambda b, h, i: (b, 0)),
                  pl.BlockSpec((hp, rank, LANES + v_dim), lambda b, h, i: (h, 0, 0))],
        out_specs=pl.BlockSpec((tq, hp * v_dim), lambda b, h, i: (b * nq + i, h)),
        scratch_shapes=[pltpu.VMEM((hp, seq // tk, tk, qk_dim), BF16),
                        pltpu.VMEM((hp, seq // tk, v_dim, tk), BF16),
                        pltpu.VMEM((hp, v_dim, tq), F32),
                        pltpu.VMEM((hp, tk, tq), F32),
                        pltpu.VMEM((hp, tk, tq), F32)],
        compiler_params=_params(("arbitrary", "arbitrary", "arbitrary")),
        name="mla_attn",
    )(q, c_kv, kr_pad, w_kv)


def _rot_cols(w):
    half = w.shape[-1] // 2
    return jnp.concatenate([-w[..., half:], w[..., :half]], axis=-1)


def _pad_lanes(w):
    pad = [(0, 0)] * (w.ndim - 1) + [(0, LANES - w.shape[-1])]
    return jnp.pad(w, pad)


def kernel(x, mem, positions, g_mix, g_ffn, g_final, w_in_a, g_v, w_sp, b_sp, g_kv, w_kv_a,
           g_kv_lat, w_in_b, g_q_lat, w_uq, w_uk, w_uv, g_mem, w_mem_kv, w_out, w_ffn_up,
           conv_w, conv_b, w_ffn_down):
    batch, seq, d_model = x.shape
    depth = g_mix.shape[0]
    n_a = w_in_a.shape[0]
    n_mem = mem.shape[1]
    T = batch * seq
    kv_rank = g_kv_lat.shape[0]
    q_rank = g_q_lat.shape[1]
    n_heads, nope_dim = w_uk.shape[2], w_uk.shape[3]
    rope_dim = w_kv_a.shape[1] - kv_rank
    mem_w = w_mem_kv.shape[2] // 2
    g_w = g_v.shape[1]
    scale = (nope_dim + rope_dim) ** -0.5 * np.log2(np.e)

    xs = x.reshape(T, d_model)
    mems = mem.reshape(batch * n_mem, d_model)

    inv = 1.0 / (ROPE_THETA ** (jnp.arange(0, rope_dim, 2, dtype=F32) / rope_dim))
    inv_row = _pad_lanes(jnp.concatenate([inv, inv])).reshape(1, LANES)
    cos_t, sin_t = rope_tables(positions.reshape(T, 1), inv_row)

    w_in_a_bf, w_in_b_bf, w_mem_kv_bf = (w.astype(BF16) for w in (w_in_a, w_in_b, w_mem_kv))
    w_out_bf, w_up_bf, w_down_bf = (w.astype(BF16) for w in (w_out, w_ffn_up, w_ffn_down))

    c_kv = kr_pad = None
    for l in range(depth):
        if l == n_a:
            w_kr = w_kv_a[:, kv_rank:]
            w_kv_cat = jnp.concatenate(
                [w_kv_a[:, :kv_rank], _pad_lanes(w_kr), _pad_lanes(_rot_cols(w_kr))], axis=1)
            kv_raw = norm_matmul(xs, g_kv, w_kv_cat.astype(BF16)[None], 0, F32,
                                 tm=1024, name="kv_proj")
            c_kv, kr_pad = kv_post(kv_raw, g_kv_lat, cos_t, sin_t)

        kvm = norm_matmul(mems, g_mem[l], w_mem_kv_bf, l, BF16,
                          tm=1024, name="mem_kv_proj")
        if l < n_a:
            z = norm_matmul(xs, g_mix[l], w_in_a_bf, l, BF16, tm=1024, name="in_proj_a")
            b_full = jnp.broadcast_to(b_sp[l][:, :, None], w_sp[l].shape)
            main = gmlp_gate(z, g_v[l], w_sp[l], b_full)
            q_col_block = (2 * g_w) // mem_w
        else:
            j = l - n_a
            z = norm_matmul(xs, g_mix[l], w_in_b_bf, j, BF16, tm=1024, name="in_proj_b")
            wq = w_uq[j].reshape(q_rank, n_heads, nope_dim + rope_dim)
            wq_rope = wq[..., nope_dim:]
            w_main = jnp.concatenate(
                [wq[..., :nope_dim], _pad_lanes(wq_rope)], axis=-1).reshape(q_rank, -1)
            w_rot = _pad_lanes(_rot_cols(wq_rope)).reshape(q_rank, -1)
            q = q_prep(z, g_q_lat[j], w_main.astype(BF16), w_rot.astype(BF16), cos_t, sin_t,
                       batch, n_heads, scale)
            w_kv_h = jnp.concatenate([w_uk[j], w_uv[j]], axis=-1)
            w_kv_h = jnp.transpose(w_kv_h, (1, 0, 2)).astype(BF16)
            main = mla_attn(q, c_kv, kr_pad, w_kv_h)
            q_col_block = q_rank // mem_w
        xs = mix_out(xs, main, z, q_col_block, kvm, w_out_bf, l, batch)
        xs = conv_ffn(xs, g_ffn[l], w_up_bf,
                      _interleave_gate_value(conv_w[l], FFN_TF),
                      _interleave_gate_value(conv_b[l], FFN_TF),
                      w_down_bf, l, seq, tf=FFN_TF,
                      out_gain=g_final if l == depth - 1 else None)
    return xs.reshape(batch, seq, d_model)
```

```python
import functools

import jax
import jax.numpy as jnp
import numpy as np
from jax import lax
from jax.experimental import pallas as pl
from jax.experimental.pallas import tpu as pltpu

EPS = 1e-6
ROPE_THETA = 10000.0
MEM_HEADS = 4
CHUNK = 128
LANES = 128
SUBLANES = 8
CONV_W = 3
FFN_TF = 512
VMEM_LIMIT = 56 * 1024 * 1024
FFN_VMEM_LIMIT = 62 * 1024 * 1024

F32 = jnp.float32
BF16 = jnp.bfloat16


def _params(semantics, vmem=VMEM_LIMIT):
    return pltpu.CompilerParams(dimension_semantics=semantics, vmem_limit_bytes=vmem)


def _rms(x, g):
    ms = jnp.mean(x * x, axis=-1, keepdims=True)
    return x * lax.rsqrt(ms + EPS) * g


def _gelu(x):
    return 0.5 * x * (1.0 + lax.erf(x * np.float32(np.sqrt(0.5))))


def _rope_table_kernel(pos_ref, inv_ref, cos_ref, sin_ref):
    ang = pos_ref[...].astype(F32) * inv_ref[...]
    cos_ref[...] = jnp.cos(ang)
    sin_ref[...] = jnp.sin(ang)


def rope_tables(pos_col, inv_row, tm=2048):
    T = pos_col.shape[0]
    return pl.pallas_call(
        _rope_table_kernel,
        out_shape=(jax.ShapeDtypeStruct((T, LANES), F32),) * 2,
        grid=(T // tm,),
        in_specs=[pl.BlockSpec((tm, 1), lambda i: (i, 0)),
                  pl.BlockSpec((1, LANES), lambda i: (0, 0))],
        out_specs=(pl.BlockSpec((tm, LANES), lambda i: (i, 0)),) * 2,
        compiler_params=_params(("arbitrary",)),
        name="rope_tables",
    )(pos_col, inv_row)


def _norm_matmul_kernel(x_ref, g_ref, w_ref, o_ref, h_ref, *, rc, gelu_cols):
    n_chunks = x_ref.shape[0] // rc

    def norm(r):
        rows = slice(r * rc, (r + 1) * rc)
        h_ref[rows, :] = _rms(x_ref[rows, :], g_ref[...]).astype(h_ref.dtype)

    norm(0)
    for r in range(n_chunks):
        if r + 1 < n_chunks:
            norm(r + 1)
        rows = slice(r * rc, (r + 1) * rc)
        acc = jnp.dot(h_ref[rows, :], w_ref[...], preferred_element_type=F32)
        if gelu_cols:
            acc = jnp.concatenate([_gelu(acc[:, :gelu_cols]), acc[:, gelu_cols:]], axis=1)
        o_ref[rows, :] = acc.astype(o_ref.dtype)


def norm_matmul(x, g, w, layer, out_dtype, tm, name, rc=256, gelu_cols=0):
    T, K = x.shape
    N = w.shape[2]
    tm = min(tm, T)
    return pl.pallas_call(
        functools.partial(_norm_matmul_kernel, rc=rc, gelu_cols=gelu_cols),
        out_shape=jax.ShapeDtypeStruct((T, N), out_dtype),
        grid=(T // tm,),
        in_specs=[pl.BlockSpec((tm, K), lambda i: (i, 0)),
                  pl.BlockSpec((1, K), lambda i: (0, 0)),
                  pl.BlockSpec((None, K, N), lambda i: (layer, 0, 0),
                               pipeline_mode=pl.Buffered(1))],
        out_specs=pl.BlockSpec((tm, N), lambda i: (i, 0)),
        scratch_shapes=[pltpu.VMEM((tm, K), BF16)],
        compiler_params=_params(("arbitrary",), vmem=FFN_VMEM_LIMIT),
        name=name,
    )(x, g.reshape(1, K), w)


def _kv_post_kernel(kv_ref, g_ref, cos_ref, sin_ref, ckv_ref, kr_ref, *, rank):
    lat = kv_ref[:, :rank]
    ckv_ref[...] = _rms(lat, g_ref[...]).astype(ckv_ref.dtype)
    kr = kv_ref[:, rank:rank + LANES] * cos_ref[...]
    kr = kr + kv_ref[:, rank + LANES:] * sin_ref[...]
    kr_ref[...] = kr.astype(kr_ref.dtype)


def kv_post(kv_raw, g_lat, cos_t, sin_t, tm=1024):
    T, N = kv_raw.shape
    rank = N - 2 * LANES
    return pl.pallas_call(
        functools.partial(_kv_post_kernel, rank=rank),
        out_shape=(jax.ShapeDtypeStruct((T, rank), BF16),
                   jax.ShapeDtypeStruct((T, LANES), BF16)),
        grid=(T // tm,),
        in_specs=[pl.BlockSpec((tm, N), lambda i: (i, 0)),
                  pl.BlockSpec((1, rank), lambda i: (0, 0)),
                  pl.BlockSpec((tm, LANES), lambda i: (i, 0)),
                  pl.BlockSpec((tm, LANES), lambda i: (i, 0))],
        out_specs=(pl.BlockSpec((tm, rank), lambda i: (i, 0)),
                   pl.BlockSpec((tm, LANES), lambda i: (i, 0))),
        compiler_params=_params(("arbitrary",)),
        name="kv_post",
    )(kv_raw, g_lat.reshape(1, rank), cos_t, sin_t)


def _gmlp_kernel(u_ref, v_ref, gv_ref, w_ref, b_ref, o_ref, *, n_groups, tm):
    row = lax.broadcasted_iota(jnp.int32, (CHUNK, CHUNK), 0)
    col = lax.broadcasted_iota(jnp.int32, (CHUNK, CHUNK), 1)
    causal = row >= col
    w_tril = [jnp.where(causal, w_ref[g], 0.0).astype(BF16) for g in range(n_groups)]
    for c in range(tm // CHUNK):
        rows = slice(c * CHUNK, (c + 1) * CHUNK)
        ug = u_ref[rows, :].astype(F32)
        vn = _rms(v_ref[rows, :].astype(F32), gv_ref[...]).astype(BF16)
        for g in range(n_groups):
            cols = slice(g * CHUNK, (g + 1) * CHUNK)
            sv = jnp.dot(w_tril[g], vn[:, cols], preferred_element_type=F32) + b_ref[g]
            o_ref[rows, cols] = (ug[:, cols] * sv).astype(o_ref.dtype)


def gmlp_gate(z, g_v, w_sp, b_full, tm=512):
    T = z.shape[0]
    n_groups = w_sp.shape[0]
    gw = n_groups * CHUNK
    return pl.pallas_call(
        functools.partial(_gmlp_kernel, n_groups=n_groups, tm=tm),
        out_shape=jax.ShapeDtypeStruct((T, gw), BF16),
        grid=(T // tm,),
        in_specs=[pl.BlockSpec((tm, gw), lambda i: (i, 0)),
                  pl.BlockSpec((tm, gw), lambda i: (i, 1)),
                  pl.BlockSpec((1, gw), lambda i: (0, 0)),
                  pl.BlockSpec((n_groups, CHUNK, CHUNK), lambda i: (0, 0, 0)),
                  pl.BlockSpec((n_groups, CHUNK, CHUNK), lambda i: (0, 0, 0))],
        out_specs=pl.BlockSpec((tm, gw), lambda i: (i, 0)),
        compiler_params=_params(("arbitrary",)),
        name="gmlp_gate",
    )(z, z, g_v.reshape(1, gw), w_sp, b_full)


def _mix_out_kernel(x_ref, a_ref, q_ref, kv_ref, w_ref, o_ref, *, n_heads):
    ka = a_ref.shape[1]
    width = q_ref.shape[1]
    head_dim = width // n_heads
    scale = np.float32(head_dim ** -0.5)
    acc = jnp.dot(a_ref[...], w_ref[:ka, :], preferred_element_type=F32)
    mo = []
    for h in range(n_heads):
        cols = slice(h * head_dim, (h + 1) * head_dim)
        k = kv_ref[:, cols]
        v = kv_ref[:, width + h * head_dim:width + (h + 1) * head_dim]
        s = lax.dot_general(q_ref[:, cols], k, (((1,), (1,)), ((), ())),
                            preferred_element_type=F32) * scale
        m = jnp.max(s, axis=-1, keepdims=True)
        p = jnp.exp(s - m)
        l = jnp.sum(p, axis=-1, keepdims=True)
        o = jnp.dot(p.astype(BF16), v, preferred_element_type=F32) / l
        mo.append(o.astype(BF16))
    acc = acc + jnp.dot(jnp.concatenate(mo, axis=1), w_ref[ka:, :], preferred_element_type=F32)
    o_ref[...] = x_ref[...] + acc


def mix_out(x, main, z, q_col_block, kvm, w_out, layer, batch, tm=512):
    T, D = x.shape
    ka = main.shape[1]
    n_mem = kvm.shape[0] // batch
    width = kvm.shape[1] // 2
    per_b = T // batch // tm
    return pl.pallas_call(
        functools.partial(_mix_out_kernel, n_heads=MEM_HEADS),
        out_shape=jax.ShapeDtypeStruct((T, D), F32),
        grid=(batch, per_b),
        in_specs=[pl.BlockSpec((tm, D), lambda b, i: (b * per_b + i, 0)),
                  pl.BlockSpec((tm, ka), lambda b, i: (b * per_b + i, 0)),
                  pl.BlockSpec((tm, width), lambda b, i: (b * per_b + i, q_col_block)),
                  pl.BlockSpec((n_mem, 2 * width), lambda b, i: (b, 0)),
                  pl.BlockSpec((None, ka + width, D), lambda b, i: (layer, 0, 0))],
        out_specs=pl.BlockSpec((tm, D), lambda b, i: (b * per_b + i, 0)),
        compiler_params=_params(("arbitrary", "arbitrary")),
        name="mix_out",
    )(x, main, z, kvm, w_out)


def _conv_ffn_kernel(x_ref, g_ref, wg_ref, wv_ref, cw_ref, cb_ref, wd_ref, o_ref,
                     h_ref, a_ref, carry_ref, *, tm, rc, tf, tiles_per_seq, norm_out):
    i = pl.program_id(0)
    j = pl.program_id(1)

    @pl.when(jnp.logical_and(i == 0, j == 0))
    def _():
        carry_ref[...] = jnp.zeros(carry_ref.shape, F32)

    @pl.when(j == 0)
    def _():
        for r in range(tm // rc):
            rows = slice(r * rc, (r + 1) * rc)
            x = x_ref[rows, :]
            h_ref[rows, :] = _rms(x, g_ref[0:1, :]).astype(h_ref.dtype)
            o_ref[rows, :] = x

    first_in_seq = (i % tiles_per_seq) == 0
    a_ref[:SUBLANES, :] = jnp.where(first_in_seq, 0.0, carry_ref[j])
    cw = cw_ref[...]
    cb = cb_ref[...]
    def up(r):
        lo = SUBLANES + r * rc
        h = h_ref[r * rc:(r + 1) * rc, :]
        a_ref[lo:lo + rc, :tf] = jnp.dot(h, wg_ref[...], preferred_element_type=F32)
        a_ref[lo:lo + rc, tf:] = jnp.dot(h, wv_ref[...], preferred_element_type=F32)

    def gate_down(r):
        lo = SUBLANES + r * rc
        c = cb + a_ref[lo:lo + rc, :] * cw[CONV_W - 1:CONV_W, :]
        for k in range(CONV_W - 1):
            lag = CONV_W - 1 - k
            c = c + a_ref[lo - lag:lo - lag + rc, :] * cw[k:k + 1, :]
        cg = c[:, :tf]
        gated = (cg * jax.nn.sigmoid(cg) * c[:, tf:]).astype(BF16)
        o_ref[r * rc:(r + 1) * rc, :] += jnp.dot(gated, wd_ref[...],
                                                 preferred_element_type=F32)

    n_chunks = tm // rc
    up(0)
    for r in range(n_chunks):
        if r + 1 < n_chunks:
            up(r + 1)
        gate_down(r)
    carry_ref[j] = a_ref[tm:, :]

    if norm_out:
        @pl.when(j == pl.num_programs(1) - 1)
        def _():
            for r in range(n_chunks):
                rows = slice(r * rc, (r + 1) * rc)
                o_ref[rows, :] = _rms(o_ref[rows, :], g_ref[1:2, :])


def conv_ffn(x, g, w_up, conv_w, conv_b, w_down, layer, seq, out_gain=None,
             tm=1024, tf=512, rc=512):
    T, D = x.shape
    d_ff = w_down.shape[1]
    nff = d_ff // tf
    gains = g.reshape(1, D) if out_gain is None else jnp.stack([g, out_gain])
    kern = functools.partial(_conv_ffn_kernel, tm=tm, rc=rc, tf=tf, tiles_per_seq=seq // tm,
                             norm_out=out_gain is not None)
    return pl.pallas_call(
        kern,
        out_shape=jax.ShapeDtypeStruct((T, D), F32),
        grid=(T // tm, nff),
        in_specs=[pl.BlockSpec((tm, D), lambda i, j: (i, 0)),
                  pl.BlockSpec(gains.shape, lambda i, j: (0, 0)),
                  pl.BlockSpec((None, D, tf), lambda i, j: (layer, 0, j)),
                  pl.BlockSpec((None, D, tf), lambda i, j: (layer, 0, nff + j)),
                  pl.BlockSpec((CONV_W, 2 * tf), lambda i, j: (0, j)),
                  pl.BlockSpec((1, 2 * tf), lambda i, j: (0, j)),
                  pl.BlockSpec((None, tf, D), lambda i, j: (layer, j, 0))],
        out_specs=pl.BlockSpec((tm, D), lambda i, j: (i, 0)),
        scratch_shapes=[pltpu.VMEM((tm, D), BF16),
                        pltpu.VMEM((tm + SUBLANES, 2 * tf), F32),
                        pltpu.VMEM((nff, SUBLANES, 2 * tf), F32)],
        compiler_params=_params(("arbitrary", "arbitrary"), vmem=FFN_VMEM_LIMIT),
        name="conv_ffn",
    )(x, gains, w_up, w_up, conv_w, conv_b.reshape(1, -1), w_down)


def _interleave_gate_value(w, tf):
    lead = w.shape[:-1]
    d_ff = w.shape[-1] // 2
    w = w.reshape(*lead, 2, d_ff // tf, tf)
    return jnp.swapaxes(w, -3, -2).reshape(*lead, 2 * d_ff)


def _q_prep_kernel(q_ref, g_ref, wm_ref, wr_ref, cos_ref, sin_ref, o_ref, *, n_heads, scale):
    qn = _rms(q_ref[...].astype(F32), g_ref[...]).astype(BF16)
    a = jnp.dot(qn, wm_ref[...], preferred_element_type=F32)
    r = jnp.dot(qn, wr_ref[...], preferred_element_type=F32)
    cos = cos_ref[...]
    sin = sin_ref[...]
    for h in range(n_heads):
        base = 2 * LANES * h
        o_ref[0, h, :, :LANES] = (a[:, base:base + LANES] * scale).astype(o_ref.dtype)
        rope = a[:, base + LANES:base + 2 * LANES] * cos + r[:, h * LANES:(h + 1) * LANES] * sin
        o_ref[0, h, :, LANES:] = (rope * scale).astype(o_ref.dtype)


def q_prep(z, g_q, w_main, w_rot, cos_t, sin_t, batch, n_heads, scale, tm=512):
    T = z.shape[0]
    seq = T // batch
    rank = w_main.shape[0]
    per_b = seq // tm
    kern = functools.partial(_q_prep_kernel, n_heads=n_heads, scale=np.float32(scale))
    return pl.pallas_call(
        kern,
        out_shape=jax.ShapeDtypeStruct((batch, n_heads, seq, 2 * LANES), BF16),
        grid=(batch, per_b),
        in_specs=[pl.BlockSpec((tm, rank), lambda b, i: (b * per_b + i, 0)),
                  pl.BlockSpec((1, rank), lambda b, i: (0, 0)),
                  pl.BlockSpec(w_main.shape, lambda b, i: (0, 0)),
                  pl.BlockSpec(w_rot.shape, lambda b, i: (0, 0)),
                  pl.BlockSpec((tm, LANES), lambda b, i: (b * per_b + i, 0)),
                  pl.BlockSpec((tm, LANES), lambda b, i: (b * per_b + i, 0))],
        out_specs=pl.BlockSpec((1, n_heads, tm, 2 * LANES), lambda b, i: (b, 0, i, 0)),
        compiler_params=_params(("arbitrary", "arbitrary")),
        name="q_prep",
    )(z, g_q.reshape(1, rank), w_main, w_rot, cos_t, sin_t)


def _mla_attn_kernel(q_ref, ckv_ref, kr_ref, wkv_ref, o_ref, k_ref, vt_ref, acc_ref,
                     sa_ref, sb_ref, *, tq, tk, hp):
    i = pl.program_id(2)
    n_kb = k_ref.shape[1]
    v_dim = vt_ref.shape[2]

    @pl.when(i == 0)
    def _():
        for jb in range(n_kb):
            rows = slice(jb * tk, (jb + 1) * tk)
            for hh in range(hp):
                kv = jnp.dot(ckv_ref[rows, :], wkv_ref[hh], preferred_element_type=F32)
                k_ref[hh, jb, :, :LANES] = kv[:, :LANES].astype(k_ref.dtype)
                k_ref[hh, jb, :, LANES:] = kr_ref[rows, :]
                vt_ref[hh, jb] = kv[:, LANES:].T.astype(vt_ref.dtype)

    acc_ref[...] = jnp.zeros(acc_ref.shape, F32)

    def scores(jb, dst_ref, q_lo=0):
        for hh in range(hp):
            dst_ref[hh, :, q_lo:] = lax.dot_general(
                k_ref[hh, jb], q_ref[0, hh, q_lo:, :], (((1,), (1,)), ((), ())),
                preferred_element_type=F32)

    def consume(jb, src_ref, carry, diag_offset, q_lo=0):
        nq_cols = tq - q_lo
        out = []
        for hh in range(hp):
            m_all, l_all = carry[hh]
            m, l = m_all[:, q_lo:], l_all[:, q_lo:]
            st = src_ref[hh, :, q_lo:]
            if diag_offset is not None:
                k_pos = lax.broadcasted_iota(jnp.int32, (tk, nq_cols), 0) + diag_offset
                q_pos = lax.broadcasted_iota(jnp.int32, (tk, nq_cols), 1)
                st = jnp.where(k_pos <= q_pos, st, -jnp.inf)
            m_new = jnp.maximum(m, jnp.max(st, axis=0, keepdims=True))
            alpha = jnp.exp2(m - m_new)
            p = jnp.exp2(st - m_new)
            l = alpha * l + jnp.sum(p, axis=0, keepdims=True)
            pv = jnp.dot(vt_ref[hh, jb], p.astype(BF16), preferred_element_type=F32)
            acc_ref[hh, :, q_lo:] = alpha * acc_ref[hh, :, q_lo:] + pv
            if q_lo:
                m_new = jnp.concatenate([m_all[:, :q_lo], m_new], axis=1)
                l = jnp.concatenate([l_all[:, :q_lo], l], axis=1)
            out.append((m_new, l))
        return tuple(out)

    def pair(t, carry):
        scores(2 * t + 1, sb_ref)
        carry = consume(2 * t, sa_ref, carry, None)
        scores(2 * t + 2, sa_ref)
        return consume(2 * t + 1, sb_ref, carry, None)

    init = tuple((jnp.full((1, tq), -jnp.inf, F32), jnp.zeros((1, tq), F32))
                 for _ in range(hp))
    scores(0, sa_ref)
    carry = lax.fori_loop(0, i, pair, init)
    scores(2 * i + 1, sb_ref, q_lo=tk)
    carry = consume(2 * i, sa_ref, carry, 0)
    carry = consume(2 * i + 1, sb_ref, carry, 0, q_lo=tk)
    for hh in range(hp):
        o = acc_ref[hh] / carry[hh][1]
        o_ref[:, hh * v_dim:(hh + 1) * v_dim] = o.T.astype(o_ref.dtype)


def mla_attn(q, c_kv, kr_pad, w_kv, tq=1024, hp=2):
    tk = tq // 2
    batch, n_heads, seq, qk_dim = q.shape
    rank = c_kv.shape[1]
    v_dim = w_kv.shape[2] - LANES
    nq = seq // tq
    return pl.pallas_call(
        functools.partial(_mla_attn_kernel, tq=tq, tk=tk, hp=hp),
        out_shape=jax.ShapeDtypeStruct((batch * seq, n_heads * v_dim), BF16),
        grid=(batch, n_heads // hp, nq),
        in_specs=[pl.BlockSpec((1, hp, tq, qk_dim), lambda b, h, i: (b, h, i, 0)),
                  pl.BlockSpec((seq, rank), lambda b, h, i: (b, 0)),
                  pl.BlockSpec((seq, LANES), lambda b, h, i: (b, 0)),
                  pl.BlockSpec((hp, rank, LANES + v_dim), lambda b, h, i: (h, 0, 0))],
        out_specs=pl.BlockSpec((tq, hp * v_dim), lambda b, h, i: (b * nq + i, h)),
        scratch_shapes=[pltpu.VMEM((hp, seq // tk, tk, qk_dim), BF16),
                        pltpu.VMEM((hp, seq // tk, v_dim, tk), BF16),
                        pltpu.VMEM((hp, v_dim, tq), F32),
                        pltpu.VMEM((hp, tk, tq), F32),
                        pltpu.VMEM((hp, tk, tq), F32)],
        compiler_params=_params(("arbitrary", "arbitrary", "arbitrary")),
        name="mla_attn",
    )(q, c_kv, kr_pad, w_kv)


def _rot_cols(w):
    half = w.shape[-1] // 2
    return jnp.concatenate([-w[..., half:], w[..., :half]], axis=-1)


def _pad_lanes(w):
    pad = [(0, 0)] * (w.ndim - 1) + [(0, LANES - w.shape[-1])]
    return jnp.pad(w, pad)


def kernel(x, mem, positions, g_mix, g_ffn, g_final, w_in_a, g_v, w_sp, b_sp, g_kv, w_kv_a,
           g_kv_lat, w_in_b, g_q_lat, w_uq, w_uk, w_uv, g_mem, w_mem_kv, w_out, w_ffn_up,
           conv_w, conv_b, w_ffn_down):
    batch, seq, d_model = x.shape
    depth = g_mix.shape[0]
    n_a = w_in_a.shape[0]
    n_mem = mem.shape[1]
    T = batch * seq
    kv_rank = g_kv_lat.shape[0]
    q_rank = g_q_lat.shape[1]
    n_heads, nope_dim = w_uk.shape[2], w_uk.shape[3]
    rope_dim = w_kv_a.shape[1] - kv_rank
    mem_w = w_mem_kv.shape[2] // 2
    g_w = g_v.shape[1]
    scale = (nope_dim + rope_dim) ** -0.5 * np.log2(np.e)

    xs = x.reshape(T, d_model)
    mems = mem.reshape(batch * n_mem, d_model)

    inv = 1.0 / (ROPE_THETA ** (jnp.arange(0, rope_dim, 2, dtype=F32) / rope_dim))
    inv_row = _pad_lanes(jnp.concatenate([inv, inv])).reshape(1, LANES)
    cos_t, sin_t = rope_tables(positions.reshape(T, 1), inv_row)

    w_in_a_bf, w_in_b_bf, w_mem_kv_bf = (w.astype(BF16) for w in (w_in_a, w_in_b, w_mem_kv))
    w_out_bf, w_up_bf, w_down_bf = (w.astype(BF16) for w in (w_out, w_ffn_up, w_ffn_down))

    c_kv = kr_pad = None
    for l in range(depth):
        if l == n_a:
            w_kr = w_kv_a[:, kv_rank:]
            w_kv_cat = jnp.concatenate(
                [w_kv_a[:, :kv_rank], _pad_lanes(w_kr), _pad_lanes(_rot_cols(w_kr))], axis=1)
            kv_raw = norm_matmul(xs, g_kv, w_kv_cat.astype(BF16)[None], 0, F32,
                                 tm=1024, name="kv_proj")
            c_kv, kr_pad = kv_post(kv_raw, g_kv_lat, cos_t, sin_t)

        kvm = norm_matmul(mems, g_mem[l], w_mem_kv_bf, l, BF16,
                          tm=1024, name="mem_kv_proj")
        if l < n_a:
            z = norm_matmul(xs, g_mix[l], w_in_a_bf, l, BF16, tm=1024, name="in_proj_a",
                            gelu_cols=2 * g_w)
            b_full = jnp.broadcast_to(b_sp[l][:, :, None], w_sp[l].shape)
            main = gmlp_gate(z, g_v[l], w_sp[l], b_full)
            q_col_block = (2 * g_w) // mem_w
        else:
            j = l - n_a
            z = norm_matmul(xs, g_mix[l], w_in_b_bf, j, BF16, tm=1024, name="in_proj_b")
            wq = w_uq[j].reshape(q_rank, n_heads, nope_dim + rope_dim)
            wq_rope = wq[..., nope_dim:]
            w_main = jnp.concatenate(
                [wq[..., :nope_dim], _pad_lanes(wq_rope)], axis=-1).reshape(q_rank, -1)
            w_rot = _pad_lanes(_rot_cols(wq_rope)).reshape(q_rank, -1)
            q = q_prep(z, g_q_lat[j], w_main.astype(BF16), w_rot.astype(BF16), cos_t, sin_t,
                       batch, n_heads, scale)
            w_kv_h = jnp.concatenate([w_uk[j], w_uv[j]], axis=-1)
            w_kv_h = jnp.transpose(w_kv_h, (1, 0, 2)).astype(BF16)
            main = mla_attn(q, c_kv, kr_pad, w_kv_h)
            q_col_block = q_rank // mem_w
        xs = mix_out(xs, main, z, q_col_block, kvm, w_out_bf, l, batch)
        xs = conv_ffn(xs, g_ffn[l], w_up_bf,
                      _interleave_gate_value(conv_w[l], FFN_TF),
                      _interleave_gate_value(conv_b[l], FFN_TF),
                      w_down_bf, l, seq, tf=FFN_TF,
                      out_gain=g_final if l == depth - 1 else None)
    return xs.reshape(batch, seq, d_model)
```

```python
import functools

import jax
import jax.numpy as jnp
import numpy as np
from jax import lax
from jax.experimental import pallas as pl
from jax.experimental.pallas import tpu as pltpu

EPS = 1e-6
ROPE_THETA = 10000.0
MEM_HEADS = 4
CHUNK = 128
LANES = 128
SUBLANES = 8
CONV_W = 3
FFN_TF = 512
VMEM_LIMIT = 56 * 1024 * 1024
FFN_VMEM_LIMIT = 62 * 1024 * 1024

F32 = jnp.float32
BF16 = jnp.bfloat16


def _params(semantics, vmem=VMEM_LIMIT):
    return pltpu.CompilerParams(dimension_semantics=semantics, vmem_limit_bytes=vmem)


def _rms(x, g):
    ms = jnp.mean(x * x, axis=-1, keepdims=True)
    return x * lax.rsqrt(ms + EPS) * g


def _gelu(x):
    return 0.5 * x * (1.0 + lax.erf(x * np.float32(np.sqrt(0.5))))


def _rope_table_kernel(pos_ref, inv_ref, cos_ref, sin_ref):
    ang = pos_ref[...].astype(F32) * inv_ref[...]
    cos_ref[...] = jnp.cos(ang)
    sin_ref[...] = jnp.sin(ang)


def rope_tables(pos_col, inv_row, tm=2048):
    T = pos_col.shape[0]
    return pl.pallas_call(
        _rope_table_kernel,
        out_shape=(jax.ShapeDtypeStruct((T, LANES), F32),) * 2,
        grid=(T // tm,),
        in_specs=[pl.BlockSpec((tm, 1), lambda i: (i, 0)),
                  pl.BlockSpec((1, LANES), lambda i: (0, 0))],
        out_specs=(pl.BlockSpec((tm, LANES), lambda i: (i, 0)),) * 2,
        compiler_params=_params(("arbitrary",)),
        name="rope_tables",
    )(pos_col, inv_row)


def _norm_matmul_kernel(x_ref, g_ref, w_ref, o_ref, h_ref, *, rc, gelu_cols):
    n_chunks = x_ref.shape[0] // rc

    def norm(r):
        rows = slice(r * rc, (r + 1) * rc)
        h_ref[rows, :] = _rms(x_ref[rows, :], g_ref[...]).astype(h_ref.dtype)

    norm(0)
    for r in range(n_chunks):
        if r + 1 < n_chunks:
            norm(r + 1)
        rows = slice(r * rc, (r + 1) * rc)
        acc = jnp.dot(h_ref[rows, :], w_ref[...], preferred_element_type=F32)
        if gelu_cols:
            acc = jnp.concatenate([_gelu(acc[:, :gelu_cols]), acc[:, gelu_cols:]], axis=1)
        o_ref[rows, :] = acc.astype(o_ref.dtype)


def norm_matmul(x, g, w, layer, out_dtype, tm, name, rc=256, gelu_cols=0):
    T, K = x.shape
    N = w.shape[2]
    tm = min(tm, T)
    return pl.pallas_call(
        functools.partial(_norm_matmul_kernel, rc=rc, gelu_cols=gelu_cols),
        out_shape=jax.ShapeDtypeStruct((T, N), out_dtype),
        grid=(T // tm,),
        in_specs=[pl.BlockSpec((tm, K), lambda i: (i, 0)),
                  pl.BlockSpec((1, K), lambda i: (0, 0)),
                  pl.BlockSpec((None, K, N), lambda i: (layer, 0, 0),
                               pipeline_mode=pl.Buffered(1))],
        out_specs=pl.BlockSpec((tm, N), lambda i: (i, 0)),
        scratch_shapes=[pltpu.VMEM((tm, K), BF16)],
        compiler_params=_params(("arbitrary",), vmem=FFN_VMEM_LIMIT),
        name=name,
    )(x, g.reshape(1, K), w)


def _kv_post_kernel(kv_ref, g_ref, cos_ref, sin_ref, ckv_ref, kr_ref, *, rank):
    lat = kv_ref[:, :rank]
    ckv_ref[...] = _rms(lat, g_ref[...]).astype(ckv_ref.dtype)
    kr = kv_ref[:, rank:rank + LANES] * cos_ref[...]
    kr = kr + kv_ref[:, rank + LANES:] * sin_ref[...]
    kr_ref[...] = kr.astype(kr_ref.dtype)


def kv_post(kv_raw, g_lat, cos_t, sin_t, tm=1024):
    T, N = kv_raw.shape
    rank = N - 2 * LANES
    return pl.pallas_call(
        functools.partial(_kv_post_kernel, rank=rank),
        out_shape=(jax.ShapeDtypeStruct((T, rank), BF16),
                   jax.ShapeDtypeStruct((T, LANES), BF16)),
        grid=(T // tm,),
        in_specs=[pl.BlockSpec((tm, N), lambda i: (i, 0)),
                  pl.BlockSpec((1, rank), lambda i: (0, 0)),
                  pl.BlockSpec((tm, LANES), lambda i: (i, 0)),
                  pl.BlockSpec((tm, LANES), lambda i: (i, 0))],
        out_specs=(pl.BlockSpec((tm, rank), lambda i: (i, 0)),
                   pl.BlockSpec((tm, LANES), lambda i: (i, 0))),
        compiler_params=_params(("arbitrary",)),
        name="kv_post",
    )(kv_raw, g_lat.reshape(1, rank), cos_t, sin_t)


def _gmlp_kernel(u_ref, v_ref, gv_ref, w_ref, b_ref, o_ref, *, n_groups, tm):
    row = lax.broadcasted_iota(jnp.int32, (CHUNK, CHUNK), 0)
    col = lax.broadcasted_iota(jnp.int32, (CHUNK, CHUNK), 1)
    causal = row >= col
    w_tril = [jnp.where(causal, w_ref[g], 0.0).astype(BF16) for g in range(n_groups)]
    for c in range(tm // CHUNK):
        rows = slice(c * CHUNK, (c + 1) * CHUNK)
        ug = u_ref[rows, :].astype(F32)
        vn = _rms(v_ref[rows, :].astype(F32), gv_ref[...]).astype(BF16)
        for g in range(n_groups):
            cols = slice(g * CHUNK, (g + 1) * CHUNK)
            sv = jnp.dot(w_tril[g], vn[:, cols], preferred_element_type=F32) + b_ref[g]
            o_ref[rows, cols] = (ug[:, cols] * sv).astype(o_ref.dtype)


def gmlp_gate(z, g_v, w_sp, b_full, tm=512):
    T = z.shape[0]
    n_groups = w_sp.shape[0]
    gw = n_groups * CHUNK
    return pl.pallas_call(
        functools.partial(_gmlp_kernel, n_groups=n_groups, tm=tm),
        out_shape=jax.ShapeDtypeStruct((T, gw), BF16),
        grid=(T // tm,),
        in_specs=[pl.BlockSpec((tm, gw), lambda i: (i, 0)),
                  pl.BlockSpec((tm, gw), lambda i: (i, 1)),
                  pl.BlockSpec((1, gw), lambda i: (0, 0)),
                  pl.BlockSpec((n_groups, CHUNK, CHUNK), lambda i: (0, 0, 0)),
                  pl.BlockSpec((n_groups, CHUNK, CHUNK), lambda i: (0, 0, 0))],
        out_specs=pl.BlockSpec((tm, gw), lambda i: (i, 0)),
        compiler_params=_params(("arbitrary",)),
        name="gmlp_gate",
    )(z, z, g_v.reshape(1, gw), w_sp, b_full)


def _mix_out_kernel(x_ref, a_ref, q_ref, kv_ref, w_ref, o_ref, *, n_heads):
    ka = a_ref.shape[1]
    width = q_ref.shape[1]
    head_dim = width // n_heads
    scale = np.float32(head_dim ** -0.5)
    acc = jnp.dot(a_ref[...], w_ref[:ka, :], preferred_element_type=F32)
    mo = []
    for h in range(n_heads):
        cols = slice(h * head_dim, (h + 1) * head_dim)
        k = kv_ref[:, cols]
        v = kv_ref[:, width + h * head_dim:width + (h + 1) * head_dim]
        s = lax.dot_general(q_ref[:, cols], k, (((1,), (1,)), ((), ())),
                            preferred_element_type=F32) * scale
        m = jnp.max(s, axis=-1, keepdims=True)
        p = jnp.exp(s - m)
        l = jnp.sum(p, axis=-1, keepdims=True)
        o = jnp.dot(p.astype(BF16), v, preferred_element_type=F32) / l
        mo.append(o.astype(BF16))
    acc = acc + jnp.dot(jnp.concatenate(mo, axis=1), w_ref[ka:, :], preferred_element_type=F32)
    o_ref[...] = x_ref[...] + acc


def mix_out(x, main, z, q_col_block, kvm, w_out, layer, batch, tm=512):
    T, D = x.shape
    ka = main.shape[1]
    n_mem = kvm.shape[0] // batch
    width = kvm.shape[1] // 2
    per_b = T // batch // tm
    return pl.pallas_call(
        functools.partial(_mix_out_kernel, n_heads=MEM_HEADS),
        out_shape=jax.ShapeDtypeStruct((T, D), F32),
        grid=(batch, per_b),
        in_specs=[pl.BlockSpec((tm, D), lambda b, i: (b * per_b + i, 0)),
                  pl.BlockSpec((tm, ka), lambda b, i: (b * per_b + i, 0)),
                  pl.BlockSpec((tm, width), lambda b, i: (b * per_b + i, q_col_block)),
                  pl.BlockSpec((n_mem, 2 * width), lambda b, i: (b, 0)),
                  pl.BlockSpec((None, ka + width, D), lambda b, i: (layer, 0, 0))],
        out_specs=pl.BlockSpec((tm, D), lambda b, i: (b * per_b + i, 0)),
        compiler_params=_params(("arbitrary", "arbitrary")),
        name="mix_out",
    )(x, main, z, kvm, w_out)


def _conv_ffn_kernel(x_ref, g_ref, wg_ref, wv_ref, cw_ref, cb_ref, wd_ref, o_ref,
                     h_ref, a_ref, carry_ref, *, tm, rc, tf, tiles_per_seq, norm_out):
    i = pl.program_id(0)
    j = pl.program_id(1)

    @pl.when(jnp.logical_and(i == 0, j == 0))
    def _():
        carry_ref[...] = jnp.zeros(carry_ref.shape, F32)

    @pl.when(j == 0)
    def _():
        for r in range(tm // rc):
            rows = slice(r * rc, (r + 1) * rc)
            x = x_ref[rows, :]
            h_ref[rows, :] = _rms(x, g_ref[0:1, :]).astype(h_ref.dtype)
            o_ref[rows, :] = x

    first_in_seq = (i % tiles_per_seq) == 0
    a_ref[:SUBLANES, :] = jnp.where(first_in_seq, 0.0, carry_ref[j])
    cw = cw_ref[...]
    cb = cb_ref[...]
    def up(r):
        lo = SUBLANES + r * rc
        h = h_ref[r * rc:(r + 1) * rc, :]
        a_ref[lo:lo + rc, :tf] = jnp.dot(h, wg_ref[...], preferred_element_type=F32)
        a_ref[lo:lo + rc, tf:] = jnp.dot(h, wv_ref[...], preferred_element_type=F32)

    def gate_down(r):
        lo = SUBLANES + r * rc
        c = cb + a_ref[lo:lo + rc, :] * cw[CONV_W - 1:CONV_W, :]
        for k in range(CONV_W - 1):
            lag = CONV_W - 1 - k
            c = c + a_ref[lo - lag:lo - lag + rc, :] * cw[k:k + 1, :]
        cg = c[:, :tf]
        gated = (cg * jax.nn.sigmoid(cg) * c[:, tf:]).astype(BF16)
        o_ref[r * rc:(r + 1) * rc, :] += jnp.dot(gated, wd_ref[...],
                                                 preferred_element_type=F32)

    n_chunks = tm // rc
    up(0)
    for r in range(n_chunks):
        if r + 1 < n_chunks:
            up(r + 1)
        gate_down(r)
    carry_ref[j] = a_ref[tm:, :]

    if norm_out:
        @pl.when(j == pl.num_programs(1) - 1)
        def _():
            for r in range(n_chunks):
                rows = slice(r * rc, (r + 1) * rc)
                o_ref[rows, :] = _rms(o_ref[rows, :], g_ref[1:2, :])


def conv_ffn(x, g, w_up, conv_w, conv_b, w_down, layer, seq, out_gain=None,
             tm=1024, tf=512, rc=512):
    T, D = x.shape
    d_ff = w_down.shape[1]
    nff = d_ff // tf
    gains = g.reshape(1, D) if out_gain is None else jnp.stack([g, out_gain])
    kern = functools.partial(_conv_ffn_kernel, tm=tm, rc=rc, tf=tf, tiles_per_seq=seq // tm,
                             norm_out=out_gain is not None)
    return pl.pallas_call(
        kern,
        out_shape=jax.ShapeDtypeStruct((T, D), F32),
        grid=(T // tm, nff),
        in_specs=[pl.BlockSpec((tm, D), lambda i, j: (i, 0)),
                  pl.BlockSpec(gains.shape, lambda i, j: (0, 0)),
                  pl.BlockSpec((None, D, tf), lambda i, j: (layer, 0, j)),
                  pl.BlockSpec((None, D, tf), lambda i, j: (layer, 0, nff + j)),
                  pl.BlockSpec((CONV_W, 2 * tf), lambda i, j: (0, j)),
                  pl.BlockSpec((1, 2 * tf), lambda i, j: (0, j)),
                  pl.BlockSpec((None, tf, D), lambda i, j: (layer, j, 0))],
        out_specs=pl.BlockSpec((tm, D), lambda i, j: (i, 0)),
        scratch_shapes=[pltpu.VMEM((tm, D), BF16),
                        pltpu.VMEM((tm + SUBLANES, 2 * tf), F32),
                        pltpu.VMEM((nff, SUBLANES, 2 * tf), F32)],
        compiler_params=_params(("arbitrary", "arbitrary"), vmem=FFN_VMEM_LIMIT),
        name="conv_ffn",
    )(x, gains, w_up, w_up, conv_w, conv_b.reshape(1, -1), w_down)


def _interleave_gate_value(w, tf):
    lead = w.shape[:-1]
    d_ff = w.shape[-1] // 2
    w = w.reshape(*lead, 2, d_ff // tf, tf)
    return jnp.swapaxes(w, -3, -2).reshape(*lead, 2 * d_ff)


def _q_prep_kernel(q_ref, g_ref, wm_ref, wr_ref, cos_ref, sin_ref, o_ref, *, n_heads, scale):
    qn = _rms(q_ref[...].astype(F32), g_ref[...]).astype(BF16)
    a = jnp.dot(qn, wm_ref[...], preferred_element_type=F32)
    r = jnp.dot(qn, wr_ref[...], preferred_element_type=F32)
    cos = cos_ref[...]
    sin = sin_ref[...]
    for h in range(n_heads):
        base = 2 * LANES * h
        o_ref[0, h, :, :LANES] = (a[:, base:base + LANES] * scale).astype(o_ref.dtype)
        rope = a[:, base + LANES:base + 2 * LANES] * cos + r[:, h * LANES:(h + 1) * LANES] * sin
        o_ref[0, h, :, LANES:] = (rope * scale).astype(o_ref.dtype)


def q_prep(z, g_q, w_main, w_rot, cos_t, sin_t, batch, n_heads, scale, tm=512):
    T = z.shape[0]
    seq = T // batch
    rank = w_main.shape[0]
    per_b = seq // tm
    kern = functools.partial(_q_prep_kernel, n_heads=n_heads, scale=np.float32(scale))
    return pl.pallas_call(
        kern,
        out_shape=jax.ShapeDtypeStruct((batch, n_heads, seq, 2 * LANES), BF16),
        grid=(batch, per_b),
        in_specs=[pl.BlockSpec((tm, rank), lambda b, i: (b * per_b + i, 0)),
                  pl.BlockSpec((1, rank), lambda b, i: (0, 0)),
                  pl.BlockSpec(w_main.shape, lambda b, i: (0, 0)),
                  pl.BlockSpec(w_rot.shape, lambda b, i: (0, 0)),
                  pl.BlockSpec((tm, LANES), lambda b, i: (b * per_b + i, 0)),
                  pl.BlockSpec((tm, LANES), lambda b, i: (b * per_b + i, 0))],
        out_specs=pl.BlockSpec((1, n_heads, tm, 2 * LANES), lambda b, i: (b, 0, i, 0)),
        compiler_params=_params(("arbitrary", "arbitrary")),
        name="q_prep",
    )(z, g_q.reshape(1, rank), w_main, w_rot, cos_t, sin_t)


def _mla_attn_kernel(q_ref, ckv_ref, kr_ref, wkv_ref, o_ref, k_ref, vt_ref, acc_ref,
                     sa_ref, sb_ref, ma_ref, mb_ref, *, tq, tk, hp, v_dim):
    i = pl.program_id(2)
    n_kb = k_ref.shape[1]

    @pl.when(i == 0)
    def _():
        ones = jnp.ones((vt_ref.shape[2] - v_dim, tk), vt_ref.dtype)
        for jb in range(n_kb):
            rows = slice(jb * tk, (jb + 1) * tk)
            for hh in range(hp):
                kv = jnp.dot(ckv_ref[rows, :], wkv_ref[hh], preferred_element_type=F32)
                k_ref[hh, jb, :, :LANES] = kv[:, :LANES].astype(k_ref.dtype)
                k_ref[hh, jb, :, LANES:] = kr_ref[rows, :]
                vt_ref[hh, jb, :v_dim, :] = kv[:, LANES:].T.astype(vt_ref.dtype)
                vt_ref[hh, jb, v_dim:, :] = ones

    acc_ref[...] = jnp.zeros(acc_ref.shape, F32)

    def scores(jb, dst_ref, mdst_ref, q_lo=0):
        for hh in range(hp):
            st = lax.dot_general(k_ref[hh, jb], q_ref[0, hh, q_lo:, :], (((1,), (1,)), ((), ())),
                                 preferred_element_type=F32)
            dst_ref[hh, :, q_lo:] = st
            mdst_ref[hh, :, q_lo:] = jnp.max(st, axis=0, keepdims=True)

    def consume(jb, src_ref, msrc_ref, m_all, diag=False, q_lo=0):
        out = []
        for hh in range(hp):
            m = m_all[hh][:, q_lo:]
            st = src_ref[hh, :, q_lo:]
            if diag:
                k_pos = lax.broadcasted_iota(jnp.int32, st.shape, 0)
                q_pos = lax.broadcasted_iota(jnp.int32, st.shape, 1)
                st = jnp.where(k_pos <= q_pos, st, -jnp.inf)
                m_blk = jnp.max(st, axis=0, keepdims=True)
            else:
                m_blk = msrc_ref[hh, :, q_lo:]
            m_new = jnp.maximum(m, m_blk)
            alpha = jnp.exp2(m - m_new)
            p = jnp.exp2(st - m_new).astype(BF16)
            pv = jnp.dot(vt_ref[hh, jb], p, preferred_element_type=F32)
            acc_ref[hh, :, q_lo:] = alpha * acc_ref[hh, :, q_lo:] + pv
            if q_lo:
                m_new = jnp.concatenate([m_all[hh][:, :q_lo], m_new], axis=1)
            out.append(m_new)
        return tuple(out)

    def pair(t, m_all):
        scores(2 * t + 1, sb_ref, mb_ref)
        m_all = consume(2 * t, sa_ref, ma_ref, m_all)
        scores(2 * t + 2, sa_ref, ma_ref)
        return consume(2 * t + 1, sb_ref, mb_ref, m_all)

    scores(0, sa_ref, ma_ref)
    m_all = lax.fori_loop(0, i, pair, tuple(jnp.full((1, tq), -jnp.inf, F32) for _ in range(hp)))
    scores(2 * i + 1, sb_ref, mb_ref, q_lo=tk)
    m_all = consume(2 * i, sa_ref, ma_ref, m_all, diag=True)
    consume(2 * i + 1, sb_ref, mb_ref, m_all, diag=True, q_lo=tk)
    for hh in range(hp):
        o = acc_ref[hh, :v_dim, :] / acc_ref[hh, v_dim:v_dim + 1, :]
        o_ref[:, hh * v_dim:(hh + 1) * v_dim] = o.T.astype(o_ref.dtype)


def mla_attn(q, c_kv, kr_pad, w_kv, tq=1024, hp=2):
    tk = tq // 2
    batch, n_heads, seq, qk_dim = q.shape
    rank = c_kv.shape[1]
    v_dim = w_kv.shape[2] - LANES
    v_rows = v_dim + 2 * SUBLANES
    nq = seq // tq
    return pl.pallas_call(
        functools.partial(_mla_attn_kernel, tq=tq, tk=tk, hp=hp, v_dim=v_dim),
        out_shape=jax.ShapeDtypeStruct((batch * seq, n_heads * v_dim), BF16),
        grid=(batch, n_heads // hp, nq),
        in_specs=[pl.BlockSpec((1, hp, tq, qk_dim), lambda b, h, i: (b, h, i, 0)),
                  pl.BlockSpec((seq, rank), lambda b, h, i: (b, 0)),
                  pl.BlockSpec((seq, LANES), lambda b, h, i: (b, 0)),
                  pl.BlockSpec((hp, rank, LANES + v_dim), lambda b, h, i: (h, 0, 0))],
        out_specs=pl.BlockSpec((tq, hp * v_dim), lambda b, h, i: (b * nq + i, h)),
        scratch_shapes=[pltpu.VMEM((hp, seq // tk, tk, qk_dim), BF16),
                        pltpu.VMEM((hp, seq // tk, v_rows, tk), BF16),
                        pltpu.VMEM((hp, v_rows, tq), F32),
                        pltpu.VMEM((hp, tk, tq), F32),
                        pltpu.VMEM((hp, tk, tq), F32),
                        pltpu.VMEM((hp, 1, tq), F32),
                        pltpu.VMEM((hp, 1, tq), F32)],
        compiler_params=_params(("arbitrary", "arbitrary", "arbitrary")),
        name="mla_attn",
    )(q, c_kv, kr_pad, w_kv)


def _rot_cols(w):
    half = w.shape[-1] // 2
    return jnp.concatenate([-w[..., half:], w[..., :half]], axis=-1)


def _pad_lanes(w):
    pad = [(0, 0)] * (w.ndim - 1) + [(0, LANES - w.shape[-1])]
    return jnp.pad(w, pad)


def kernel(x, mem, positions, g_mix, g_ffn, g_final, w_in_a, g_v, w_sp, b_sp, g_kv, w_kv_a,
           g_kv_lat, w_in_b, g_q_lat, w_uq, w_uk, w_uv, g_mem, w_mem_kv, w_out, w_ffn_up,
           conv_w, conv_b, w_ffn_down):
    batch, seq, d_model = x.shape
    depth = g_mix.shape[0]
    n_a = w_in_a.shape[0]
    n_mem = mem.shape[1]
    T = batch * seq
    kv_rank = g_kv_lat.shape[0]
    q_rank = g_q_lat.shape[1]
    n_heads, nope_dim = w_uk.shape[2], w_uk.shape[3]
    rope_dim = w_kv_a.shape[1] - kv_rank
    mem_w = w_mem_kv.shape[2] // 2
    g_w = g_v.shape[1]
    scale = (nope_dim + rope_dim) ** -0.5 * np.log2(np.e)

    xs = x.reshape(T, d_model)
    mems = mem.reshape(batch * n_mem, d_model)

    inv = 1.0 / (ROPE_THETA ** (jnp.arange(0, rope_dim, 2, dtype=F32) / rope_dim))
    inv_row = _pad_lanes(jnp.concatenate([inv, inv])).reshape(1, LANES)
    cos_t, sin_t = rope_tables(positions.reshape(T, 1), inv_row)

    w_in_a_bf, w_in_b_bf, w_mem_kv_bf = (w.astype(BF16) for w in (w_in_a, w_in_b, w_mem_kv))
    w_out_bf, w_up_bf, w_down_bf = (w.astype(BF16) for w in (w_out, w_ffn_up, w_ffn_down))

    c_kv = kr_pad = None
    for l in range(depth):
        if l == n_a:
            w_kr = w_kv_a[:, kv_rank:]
            w_kv_cat = jnp.concatenate(
                [w_kv_a[:, :kv_rank], _pad_lanes(w_kr), _pad_lanes(_rot_cols(w_kr))], axis=1)
            kv_raw = norm_matmul(xs, g_kv, w_kv_cat.astype(BF16)[None], 0, F32,
                                 tm=1024, name="kv_proj")
            c_kv, kr_pad = kv_post(kv_raw, g_kv_lat, cos_t, sin_t)

        kvm = norm_matmul(mems, g_mem[l], w_mem_kv_bf, l, BF16,
                          tm=1024, name="mem_kv_proj")
        if l < n_a:
            z = norm_matmul(xs, g_mix[l], w_in_a_bf, l, BF16, tm=1024, name="in_proj_a",
                            gelu_cols=2 * g_w)
            b_full = jnp.broadcast_to(b_sp[l][:, :, None], w_sp[l].shape)
            main = gmlp_gate(z, g_v[l], w_sp[l], b_full)
            q_col_block = (2 * g_w) // mem_w
        else:
            j = l - n_a
            z = norm_matmul(xs, g_mix[l], w_in_b_bf, j, BF16, tm=1024, name="in_proj_b")
            wq = w_uq[j].reshape(q_rank, n_heads, nope_dim + rope_dim)
            wq_rope = wq[..., nope_dim:]
            w_main = jnp.concatenate(
                [wq[..., :nope_dim], _pad_lanes(wq_rope)], axis=-1).reshape(q_rank, -1)
            w_rot = _pad_lanes(_rot_cols(wq_rope)).reshape(q_rank, -1)
            q = q_prep(z, g_q_lat[j], w_main.astype(BF16), w_rot.astype(BF16), cos_t, sin_t,
                       batch, n_heads, scale)
            w_kv_h = jnp.concatenate([w_uk[j], w_uv[j]], axis=-1)
            w_kv_h = jnp.transpose(w_kv_h, (1, 0, 2)).astype(BF16)
            main = mla_attn(q, c_kv, kr_pad, w_kv_h)
            q_col_block = q_rank // mem_w
        xs = mix_out(xs, main, z, q_col_block, kvm, w_out_bf, l, batch)
        xs = conv_ffn(xs, g_ffn[l], w_up_bf,
                      _interleave_gate_value(conv_w[l], FFN_TF),
                      _interleave_gate_value(conv_b[l], FFN_TF),
                      w_down_bf, l, seq, tf=FFN_TF,
                      out_gain=g_final if l == depth - 1 else None)
    return xs.reshape(batch, seq, d_model)
```

```python
import functools

import jax
import jax.numpy as jnp
import numpy as np
from jax import lax
from jax.experimental import pallas as pl
from jax.experimental.pallas import tpu as pltpu

EPS = 1e-6
ROPE_THETA = 10000.0
MEM_HEADS = 4
CHUNK = 128
LANES = 128
SUBLANES = 8
CONV_W = 3
FFN_TF = 512
VMEM_LIMIT = 56 * 1024 * 1024
VMEM_LIMIT_HIGH = 62 * 1024 * 1024

F32 = jnp.float32
BF16 = jnp.bfloat16


def _params(semantics, vmem=VMEM_LIMIT):
    return pltpu.CompilerParams(dimension_semantics=semantics, vmem_limit_bytes=vmem)


def _rms(x, g):
    ms = jnp.mean(x * x, axis=-1, keepdims=True)
    return x * lax.rsqrt(ms + EPS) * g


def _gelu(x):
    return 0.5 * x * (1.0 + lax.erf(x * np.float32(np.sqrt(0.5))))


def _rope_table_kernel(pos_ref, inv_ref, cos_ref, sin_ref):
    ang = pos_ref[...].astype(F32) * inv_ref[...]
    cos_ref[...] = jnp.cos(ang)
    sin_ref[...] = jnp.sin(ang)


def rope_tables(pos_col, inv_row, tm=2048):
    T = pos_col.shape[0]
    return pl.pallas_call(
        _rope_table_kernel,
        out_shape=(jax.ShapeDtypeStruct((T, LANES), F32),) * 2,
        grid=(T // tm,),
        in_specs=[pl.BlockSpec((tm, 1), lambda i: (i, 0)),
                  pl.BlockSpec((1, LANES), lambda i: (0, 0))],
        out_specs=(pl.BlockSpec((tm, LANES), lambda i: (i, 0)),) * 2,
        compiler_params=_params(("arbitrary",)),
        name="rope_tables",
    )(pos_col, inv_row)


def _norm_matmul_kernel(x_ref, g_ref, w_ref, o_ref, h_ref, *, rc, gelu_cols):
    n_chunks = x_ref.shape[0] // rc

    def norm(r):
        rows = slice(r * rc, (r + 1) * rc)
        h_ref[rows, :] = _rms(x_ref[rows, :], g_ref[...]).astype(h_ref.dtype)

    norm(0)
    for r in range(n_chunks):
        if r + 1 < n_chunks:
            norm(r + 1)
        rows = slice(r * rc, (r + 1) * rc)
        acc = jnp.dot(h_ref[rows, :], w_ref[...], preferred_element_type=F32)
        if gelu_cols:
            acc = jnp.concatenate([_gelu(acc[:, :gelu_cols]), acc[:, gelu_cols:]], axis=1)
        o_ref[rows, :] = acc.astype(o_ref.dtype)


def norm_matmul(x, g, w, layer, out_dtype, tm, name, rc=256, gelu_cols=0):
    T, K = x.shape
    N = w.shape[2]
    tm = min(tm, T)
    return pl.pallas_call(
        functools.partial(_norm_matmul_kernel, rc=rc, gelu_cols=gelu_cols),
        out_shape=jax.ShapeDtypeStruct((T, N), out_dtype),
        grid=(T // tm,),
        in_specs=[pl.BlockSpec((tm, K), lambda i: (i, 0)),
                  pl.BlockSpec((1, K), lambda i: (0, 0)),
                  pl.BlockSpec((None, K, N), lambda i: (layer, 0, 0),
                               pipeline_mode=pl.Buffered(1))],
        out_specs=pl.BlockSpec((tm, N), lambda i: (i, 0)),
        scratch_shapes=[pltpu.VMEM((tm, K), BF16)],
        compiler_params=_params(("arbitrary",), vmem=VMEM_LIMIT_HIGH),
        name=name,
    )(x, g.reshape(1, K), w)


def _kv_post_kernel(kv_ref, g_ref, cos_ref, sin_ref, ckv_ref, kr_ref, *, rank):
    lat = kv_ref[:, :rank]
    ckv_ref[...] = _rms(lat, g_ref[...]).astype(ckv_ref.dtype)
    kr = kv_ref[:, rank:rank + LANES] * cos_ref[...]
    kr = kr + kv_ref[:, rank + LANES:] * sin_ref[...]
    kr_ref[...] = kr.astype(kr_ref.dtype)


def kv_post(kv_raw, g_lat, cos_t, sin_t, tm=1024):
    T, N = kv_raw.shape
    rank = N - 2 * LANES
    return pl.pallas_call(
        functools.partial(_kv_post_kernel, rank=rank),
        out_shape=(jax.ShapeDtypeStruct((T, rank), BF16),
                   jax.ShapeDtypeStruct((T, LANES), BF16)),
        grid=(T // tm,),
        in_specs=[pl.BlockSpec((tm, N), lambda i: (i, 0)),
                  pl.BlockSpec((1, rank), lambda i: (0, 0)),
                  pl.BlockSpec((tm, LANES), lambda i: (i, 0)),
                  pl.BlockSpec((tm, LANES), lambda i: (i, 0))],
        out_specs=(pl.BlockSpec((tm, rank), lambda i: (i, 0)),
                   pl.BlockSpec((tm, LANES), lambda i: (i, 0))),
        compiler_params=_params(("arbitrary",)),
        name="kv_post",
    )(kv_raw, g_lat.reshape(1, rank), cos_t, sin_t)


def _gmlp_kernel(u_ref, v_ref, gv_ref, w_ref, b_ref, o_ref, *, n_groups, tm):
    row = lax.broadcasted_iota(jnp.int32, (CHUNK, CHUNK), 0)
    col = lax.broadcasted_iota(jnp.int32, (CHUNK, CHUNK), 1)
    causal = row >= col
    w_tril = [jnp.where(causal, w_ref[g], 0.0).astype(BF16) for g in range(n_groups)]
    for c in range(tm // CHUNK):
        rows = slice(c * CHUNK, (c + 1) * CHUNK)
        ug = u_ref[rows, :].astype(F32)
        vn = _rms(v_ref[rows, :].astype(F32), gv_ref[...]).astype(BF16)
        for g in range(n_groups):
            cols = slice(g * CHUNK, (g + 1) * CHUNK)
            sv = jnp.dot(w_tril[g], vn[:, cols], preferred_element_type=F32) + b_ref[g]
            o_ref[rows, cols] = (ug[:, cols] * sv).astype(o_ref.dtype)


def gmlp_gate(z, g_v, w_sp, b_full, tm=512):
    T = z.shape[0]
    n_groups = w_sp.shape[0]
    gw = n_groups * CHUNK
    return pl.pallas_call(
        functools.partial(_gmlp_kernel, n_groups=n_groups, tm=tm),
        out_shape=jax.ShapeDtypeStruct((T, gw), BF16),
        grid=(T // tm,),
        in_specs=[pl.BlockSpec((tm, gw), lambda i: (i, 0)),
                  pl.BlockSpec((tm, gw), lambda i: (i, 1)),
                  pl.BlockSpec((1, gw), lambda i: (0, 0)),
                  pl.BlockSpec((n_groups, CHUNK, CHUNK), lambda i: (0, 0, 0)),
                  pl.BlockSpec((n_groups, CHUNK, CHUNK), lambda i: (0, 0, 0))],
        out_specs=pl.BlockSpec((tm, gw), lambda i: (i, 0)),
        compiler_params=_params(("arbitrary",)),
        name="gmlp_gate",
    )(z, z, g_v.reshape(1, gw), w_sp, b_full)


def _mix_out_kernel(x_ref, a_ref, q_ref, kv_ref, w_ref, o_ref, *, n_heads):
    ka = a_ref.shape[1]
    width = q_ref.shape[1]
    head_dim = width // n_heads
    scale = np.float32(head_dim ** -0.5)
    acc = jnp.dot(a_ref[...], w_ref[:ka, :], preferred_element_type=F32)
    mo = []
    for h in range(n_heads):
        cols = slice(h * head_dim, (h + 1) * head_dim)
        k = kv_ref[:, cols]
        v = kv_ref[:, width + h * head_dim:width + (h + 1) * head_dim]
        s = lax.dot_general(q_ref[:, cols], k, (((1,), (1,)), ((), ())),
                            preferred_element_type=F32) * scale
        m = jnp.max(s, axis=-1, keepdims=True)
        p = jnp.exp(s - m)
        l = jnp.sum(p, axis=-1, keepdims=True)
        o = jnp.dot(p.astype(BF16), v, preferred_element_type=F32) / l
        mo.append(o.astype(BF16))
    acc = acc + jnp.dot(jnp.concatenate(mo, axis=1), w_ref[ka:, :], preferred_element_type=F32)
    o_ref[...] = x_ref[...] + acc


def mix_out(x, main, z, q_col_block, kvm, w_out, layer, batch, tm=512):
    T, D = x.shape
    ka = main.shape[1]
    n_mem = kvm.shape[0] // batch
    width = kvm.shape[1] // 2
    per_b = T // batch // tm
    return pl.pallas_call(
        functools.partial(_mix_out_kernel, n_heads=MEM_HEADS),
        out_shape=jax.ShapeDtypeStruct((T, D), F32),
        grid=(batch, per_b),
        in_specs=[pl.BlockSpec((tm, D), lambda b, i: (b * per_b + i, 0)),
                  pl.BlockSpec((tm, ka), lambda b, i: (b * per_b + i, 0)),
                  pl.BlockSpec((tm, width), lambda b, i: (b * per_b + i, q_col_block)),
                  pl.BlockSpec((n_mem, 2 * width), lambda b, i: (b, 0)),
                  pl.BlockSpec((None, ka + width, D), lambda b, i: (layer, 0, 0))],
        out_specs=pl.BlockSpec((tm, D), lambda b, i: (b * per_b + i, 0)),
        compiler_params=_params(("arbitrary", "arbitrary")),
        name="mix_out",
    )(x, main, z, kvm, w_out)


def _conv_ffn_kernel(x_ref, g_ref, wg_ref, wv_ref, cw_ref, cb_ref, wd_ref, o_ref,
                     h_ref, a_ref, carry_ref, *, tm, rc, tf, tiles_per_seq, norm_out):
    i = pl.program_id(0)
    j = pl.program_id(1)

    @pl.when(jnp.logical_and(i == 0, j == 0))
    def _():
        carry_ref[...] = jnp.zeros(carry_ref.shape, F32)

    @pl.when(j == 0)
    def _():
        for r in range(tm // rc):
            rows = slice(r * rc, (r + 1) * rc)
            x = x_ref[rows, :]
            h_ref[rows, :] = _rms(x, g_ref[0:1, :]).astype(h_ref.dtype)
            o_ref[rows, :] = x

    first_in_seq = (i % tiles_per_seq) == 0
    a_ref[:SUBLANES, :] = jnp.where(first_in_seq, 0.0, carry_ref[j])
    cw = cw_ref[...]
    cb = cb_ref[...]
    def up(r):
        lo = SUBLANES + r * rc
        h = h_ref[r * rc:(r + 1) * rc, :]
        a_ref[lo:lo + rc, :tf] = jnp.dot(h, wg_ref[...], preferred_element_type=F32)
        a_ref[lo:lo + rc, tf:] = jnp.dot(h, wv_ref[...], preferred_element_type=F32)

    def gate_down(r):
        lo = SUBLANES + r * rc
        c = cb + a_ref[lo:lo + rc, :] * cw[CONV_W - 1:CONV_W, :]
        for k in range(CONV_W - 1):
            lag = CONV_W - 1 - k
            c = c + a_ref[lo - lag:lo - lag + rc, :] * cw[k:k + 1, :]
        cg = c[:, :tf]
        gated = (cg * jax.nn.sigmoid(cg) * c[:, tf:]).astype(BF16)
        o_ref[r * rc:(r + 1) * rc, :] += jnp.dot(gated, wd_ref[...],
                                                 preferred_element_type=F32)

    n_chunks = tm // rc
    up(0)
    for r in range(n_chunks):
        if r + 1 < n_chunks:
            up(r + 1)
        gate_down(r)
    carry_ref[j] = a_ref[tm:, :]

    if norm_out:
        @pl.when(j == pl.num_programs(1) - 1)
        def _():
            for r in range(n_chunks):
                rows = slice(r * rc, (r + 1) * rc)
                o_ref[rows, :] = _rms(o_ref[rows, :], g_ref[1:2, :])


def conv_ffn(x, g, w_up, conv_w, conv_b, w_down, layer, seq, out_gain=None,
             tm=1024, tf=512, rc=512):
    T, D = x.shape
    d_ff = w_down.shape[1]
    nff = d_ff // tf
    gains = g.reshape(1, D) if out_gain is None else jnp.stack([g, out_gain])
    kern = functools.partial(_conv_ffn_kernel, tm=tm, rc=rc, tf=tf, tiles_per_seq=seq // tm,
                             norm_out=out_gain is not None)
    return pl.pallas_call(
        kern,
        out_shape=jax.ShapeDtypeStruct((T, D), F32),
        grid=(T // tm, nff),
        in_specs=[pl.BlockSpec((tm, D), lambda i, j: (i, 0)),
                  pl.BlockSpec(gains.shape, lambda i, j: (0, 0)),
                  pl.BlockSpec((None, D, tf), lambda i, j: (layer, 0, j)),
                  pl.BlockSpec((None, D, tf), lambda i, j: (layer, 0, nff + j)),
                  pl.BlockSpec((CONV_W, 2 * tf), lambda i, j: (0, j)),
                  pl.BlockSpec((1, 2 * tf), lambda i, j: (0, j)),
                  pl.BlockSpec((None, tf, D), lambda i, j: (layer, j, 0))],
        out_specs=pl.BlockSpec((tm, D), lambda i, j: (i, 0)),
        scratch_shapes=[pltpu.VMEM((tm, D), BF16),
                        pltpu.VMEM((tm + SUBLANES, 2 * tf), F32),
                        pltpu.VMEM((nff, SUBLANES, 2 * tf), F32)],
        compiler_params=_params(("arbitrary", "arbitrary"), vmem=VMEM_LIMIT_HIGH),
        name="conv_ffn",
    )(x, gains, w_up, w_up, conv_w, conv_b.reshape(1, -1), w_down)


def _interleave_gate_value(w, tf):
    lead = w.shape[:-1]
    d_ff = w.shape[-1] // 2
    w = w.reshape(*lead, 2, d_ff // tf, tf)
    return jnp.swapaxes(w, -3, -2).reshape(*lead, 2 * d_ff)


def _q_prep_kernel(q_ref, g_ref, wm_ref, wr_ref, cos_ref, sin_ref, o_ref, *, n_heads, scale):
    qn = _rms(q_ref[...].astype(F32), g_ref[...]).astype(BF16)
    a = jnp.dot(qn, wm_ref[...], preferred_element_type=F32)
    r = jnp.dot(qn, wr_ref[...], preferred_element_type=F32)
    cos = cos_ref[...]
    sin = sin_ref[...]
    for h in range(n_heads):
        base = 2 * LANES * h
        o_ref[0, h, :, :LANES] = (a[:, base:base + LANES] * scale).astype(o_ref.dtype)
        rope = a[:, base + LANES:base + 2 * LANES] * cos + r[:, h * LANES:(h + 1) * LANES] * sin
        o_ref[0, h, :, LANES:] = (rope * scale).astype(o_ref.dtype)


def q_prep(z, g_q, w_main, w_rot, cos_t, sin_t, batch, n_heads, scale, tm=512):
    T = z.shape[0]
    seq = T // batch
    rank = w_main.shape[0]
    per_b = seq // tm
    kern = functools.partial(_q_prep_kernel, n_heads=n_heads, scale=np.float32(scale))
    return pl.pallas_call(
        kern,
        out_shape=jax.ShapeDtypeStruct((batch, n_heads, seq, 2 * LANES), BF16),
        grid=(batch, per_b),
        in_specs=[pl.BlockSpec((tm, rank), lambda b, i: (b * per_b + i, 0)),
                  pl.BlockSpec((1, rank), lambda b, i: (0, 0)),
                  pl.BlockSpec(w_main.shape, lambda b, i: (0, 0)),
                  pl.BlockSpec(w_rot.shape, lambda b, i: (0, 0)),
                  pl.BlockSpec((tm, LANES), lambda b, i: (b * per_b + i, 0)),
                  pl.BlockSpec((tm, LANES), lambda b, i: (b * per_b + i, 0))],
        out_specs=pl.BlockSpec((1, n_heads, tm, 2 * LANES), lambda b, i: (b, 0, i, 0)),
        compiler_params=_params(("arbitrary", "arbitrary")),
        name="q_prep",
    )(z, g_q.reshape(1, rank), w_main, w_rot, cos_t, sin_t)


def _mla_attn_kernel(q_ref, ckv_ref, kr_ref, wkv_ref, o_ref, k_ref, vt_ref, acc_ref,
                     sa_ref, sb_ref, ma_ref, mb_ref, *, tq, tk, hp, v_dim):
    i = pl.program_id(2)
    n_kb = k_ref.shape[1]

    @pl.when(i == 0)
    def _():
        ones = jnp.ones((vt_ref.shape[2] - v_dim, tk), vt_ref.dtype)
        for jb in range(n_kb):
            rows = slice(jb * tk, (jb + 1) * tk)
            for hh in range(hp):
                kv = jnp.dot(ckv_ref[rows, :], wkv_ref[hh], preferred_element_type=F32)
                k_ref[hh, jb, :, :LANES] = kv[:, :LANES].astype(k_ref.dtype)
                k_ref[hh, jb, :, LANES:] = kr_ref[rows, :]
                vt_ref[hh, jb, :v_dim, :] = kv[:, LANES:].T.astype(vt_ref.dtype)
                vt_ref[hh, jb, v_dim:, :] = ones

    acc_ref[...] = jnp.zeros(acc_ref.shape, F32)

    def scores(jb, dst_ref, mdst_ref, q_lo=0):
        for hh in range(hp):
            st = lax.dot_general(k_ref[hh, jb], q_ref[0, hh, q_lo:, :], (((1,), (1,)), ((), ())),
                                 preferred_element_type=F32)
            dst_ref[hh, :, q_lo:] = st
            mdst_ref[hh, :, q_lo:] = jnp.max(st, axis=0, keepdims=True)

    def consume(jb, src_ref, msrc_ref, m_all, diag=False, q_lo=0):
        out = []
        for hh in range(hp):
            m = m_all[hh][:, q_lo:]
            st = src_ref[hh, :, q_lo:]
            if diag:
                k_pos = lax.broadcasted_iota(jnp.int32, st.shape, 0)
                q_pos = lax.broadcasted_iota(jnp.int32, st.shape, 1)
                st = jnp.where(k_pos <= q_pos, st, -jnp.inf)
                m_blk = jnp.max(st, axis=0, keepdims=True)
            else:
                m_blk = msrc_ref[hh, :, q_lo:]
            m_new = jnp.maximum(m, m_blk)
            alpha = jnp.exp2(m - m_new)
            p = jnp.exp2(st - m_new).astype(BF16)
            pv = jnp.dot(vt_ref[hh, jb], p, preferred_element_type=F32)
            acc_ref[hh, :, q_lo:] = alpha * acc_ref[hh, :, q_lo:] + pv
            if q_lo:
                m_new = jnp.concatenate([m_all[hh][:, :q_lo], m_new], axis=1)
            out.append(m_new)
        return tuple(out)

    def pair(t, m_all):
        scores(2 * t + 1, sb_ref, mb_ref)
        m_all = consume(2 * t, sa_ref, ma_ref, m_all)
        scores(2 * t + 2, sa_ref, ma_ref)
        return consume(2 * t + 1, sb_ref, mb_ref, m_all)

    scores(0, sa_ref, ma_ref)
    m_all = lax.fori_loop(0, i, pair, tuple(jnp.full((1, tq), -jnp.inf, F32) for _ in range(hp)))
    scores(2 * i + 1, sb_ref, mb_ref, q_lo=tk)
    m_all = consume(2 * i, sa_ref, ma_ref, m_all, diag=True)
    consume(2 * i + 1, sb_ref, mb_ref, m_all, diag=True, q_lo=tk)
    for hh in range(hp):
        o = acc_ref[hh, :v_dim, :] / acc_ref[hh, v_dim:v_dim + 1, :]
        o_ref[:, hh * v_dim:(hh + 1) * v_dim] = o.T.astype(o_ref.dtype)


def mla_attn(q, c_kv, kr_pad, w_kv, tq=1024, hp=4):
    tk = tq // 2
    batch, n_heads, seq, qk_dim = q.shape
    rank = c_kv.shape[1]
    v_dim = w_kv.shape[2] - LANES
    v_rows = v_dim + 2 * SUBLANES
    nq = seq // tq
    return pl.pallas_call(
        functools.partial(_mla_attn_kernel, tq=tq, tk=tk, hp=hp, v_dim=v_dim),
        out_shape=jax.ShapeDtypeStruct((batch * seq, n_heads * v_dim), BF16),
        grid=(batch, n_heads // hp, nq),
        in_specs=[pl.BlockSpec((1, hp, tq, qk_dim), lambda b, h, i: (b, h, i, 0)),
                  pl.BlockSpec((seq, rank), lambda b, h, i: (b, 0)),
                  pl.BlockSpec((seq, LANES), lambda b, h, i: (b, 0)),
                  pl.BlockSpec((hp, rank, LANES + v_dim), lambda b, h, i: (h, 0, 0))],
        out_specs=pl.BlockSpec((tq, hp * v_dim), lambda b, h, i: (b * nq + i, h)),
        scratch_shapes=[pltpu.VMEM((hp, seq // tk, tk, qk_dim), BF16),
                        pltpu.VMEM((hp, seq // tk, v_rows, tk), BF16),
                        pltpu.VMEM((hp, v_rows, tq), F32),
                        pltpu.VMEM((hp, tk, tq), F32),
                        pltpu.VMEM((hp, tk, tq), F32),
                        pltpu.VMEM((hp, 1, tq), F32),
                        pltpu.VMEM((hp, 1, tq), F32)],
        compiler_params=_params(("arbitrary", "arbitrary", "arbitrary"), vmem=VMEM_LIMIT_HIGH),
        name="mla_attn",
    )(q, c_kv, kr_pad, w_kv)


def _rot_cols(w):
    half = w.shape[-1] // 2
    return jnp.concatenate([-w[..., half:], w[..., :half]], axis=-1)


def _pad_lanes(w):
    pad = [(0, 0)] * (w.ndim - 1) + [(0, LANES - w.shape[-1])]
    return jnp.pad(w, pad)


def kernel(x, mem, positions, g_mix, g_ffn, g_final, w_in_a, g_v, w_sp, b_sp, g_kv, w_kv_a,
           g_kv_lat, w_in_b, g_q_lat, w_uq, w_uk, w_uv, g_mem, w_mem_kv, w_out, w_ffn_up,
           conv_w, conv_b, w_ffn_down):
    batch, seq, d_model = x.shape
    depth = g_mix.shape[0]
    n_a = w_in_a.shape[0]
    n_mem = mem.shape[1]
    T = batch * seq
    kv_rank = g_kv_lat.shape[0]
    q_rank = g_q_lat.shape[1]
    n_heads, nope_dim = w_uk.shape[2], w_uk.shape[3]
    rope_dim = w_kv_a.shape[1] - kv_rank
    mem_w = w_mem_kv.shape[2] // 2
    g_w = g_v.shape[1]
    scale = (nope_dim + rope_dim) ** -0.5 * np.log2(np.e)

    xs = x.reshape(T, d_model)
    mems = mem.reshape(batch * n_mem, d_model)

    inv = 1.0 / (ROPE_THETA ** (jnp.arange(0, rope_dim, 2, dtype=F32) / rope_dim))
    inv_row = _pad_lanes(jnp.concatenate([inv, inv])).reshape(1, LANES)
    cos_t, sin_t = rope_tables(positions.reshape(T, 1), inv_row)

    w_in_a_bf, w_in_b_bf, w_mem_kv_bf = (w.astype(BF16) for w in (w_in_a, w_in_b, w_mem_kv))
    w_out_bf, w_up_bf, w_down_bf = (w.astype(BF16) for w in (w_out, w_ffn_up, w_ffn_down))

    c_kv = kr_pad = None
    for l in range(depth):
        if l == n_a:
            w_kr = w_kv_a[:, kv_rank:]
            w_kv_cat = jnp.concatenate(
                [w_kv_a[:, :kv_rank], _pad_lanes(w_kr), _pad_lanes(_rot_cols(w_kr))], axis=1)
            kv_raw = norm_matmul(xs, g_kv, w_kv_cat.astype(BF16)[None], 0, F32,
                                 tm=1024, name="kv_proj")
            c_kv, kr_pad = kv_post(kv_raw, g_kv_lat, cos_t, sin_t)

        kvm = norm_matmul(mems, g_mem[l], w_mem_kv_bf, l, BF16,
                          tm=1024, name="mem_kv_proj")
        if l < n_a:
            z = norm_matmul(xs, g_mix[l], w_in_a_bf, l, BF16, tm=1024, name="in_proj_a",
                            gelu_cols=2 * g_w)
            b_full = jnp.broadcast_to(b_sp[l][:, :, None], w_sp[l].shape)
            main = gmlp_gate(z, g_v[l], w_sp[l], b_full)
            q_col_block = (2 * g_w) // mem_w
        else:
            j = l - n_a
            z = norm_matmul(xs, g_mix[l], w_in_b_bf, j, BF16, tm=1024, name="in_proj_b")
            wq = w_uq[j].reshape(q_rank, n_heads, nope_dim + rope_dim)
            wq_rope = wq[..., nope_dim:]
            w_main = jnp.concatenate(
                [wq[..., :nope_dim], _pad_lanes(wq_rope)], axis=-1).reshape(q_rank, -1)
            w_rot = _pad_lanes(_rot_cols(wq_rope)).reshape(q_rank, -1)
            q = q_prep(z, g_q_lat[j], w_main.astype(BF16), w_rot.astype(BF16), cos_t, sin_t,
                       batch, n_heads, scale)
            w_kv_h = jnp.concatenate([w_uk[j], w_uv[j]], axis=-1)
            w_kv_h = jnp.transpose(w_kv_h, (1, 0, 2)).astype(BF16)
            main = mla_attn(q, c_kv, kr_pad, w_kv_h)
            q_col_block = q_rank // mem_w
        xs = mix_out(xs, main, z, q_col_block, kvm, w_out_bf, l, batch)
        xs = conv_ffn(xs, g_ffn[l], w_up_bf,
                      _interleave_gate_value(conv_w[l], FFN_TF),
                      _interleave_gate_value(conv_b[l], FFN_TF),
                      w_down_bf, l, seq, tf=FFN_TF,
                      out_gain=g_final if l == depth - 1 else None)
    return xs.reshape(batch, seq, d_model)
```

```python
import functools

import jax
import jax.numpy as jnp
import numpy as np
from jax import lax
from jax.experimental import pallas as pl
from jax.experimental.pallas import tpu as pltpu

EPS = 1e-6
ROPE_THETA = 10000.0
MEM_HEADS = 4
CHUNK = 128
LANES = 128
SUBLANES = 8
CONV_W = 3
FFN_TF = 512
VMEM_LIMIT = 56 * 1024 * 1024
VMEM_LIMIT_HIGH = 62 * 1024 * 1024

F32 = jnp.float32
BF16 = jnp.bfloat16


def _params(semantics, vmem=VMEM_LIMIT):
    return pltpu.CompilerParams(dimension_semantics=semantics, vmem_limit_bytes=vmem)


def _rms(x, g):
    ms = jnp.mean(x * x, axis=-1, keepdims=True)
    return x * lax.rsqrt(ms + EPS) * g


def _gelu(x):
    return 0.5 * x * (1.0 + lax.erf(x * np.float32(np.sqrt(0.5))))


def _rope_table_kernel(pos_ref, inv_ref, cos_ref, sin_ref):
    ang = pos_ref[...].astype(F32) * inv_ref[...]
    cos_ref[...] = jnp.cos(ang)
    sin_ref[...] = jnp.sin(ang)


def rope_tables(pos_col, inv_row, tm=2048):
    T = pos_col.shape[0]
    return pl.pallas_call(
        _rope_table_kernel,
        out_shape=(jax.ShapeDtypeStruct((T, LANES), F32),) * 2,
        grid=(T // tm,),
        in_specs=[pl.BlockSpec((tm, 1), lambda i: (i, 0)),
                  pl.BlockSpec((1, LANES), lambda i: (0, 0))],
        out_specs=(pl.BlockSpec((tm, LANES), lambda i: (i, 0)),) * 2,
        compiler_params=_params(("arbitrary",)),
        name="rope_tables",
    )(pos_col, inv_row)


def _norm_matmul_kernel(x_ref, g_ref, w_ref, o_ref, h_ref, *, rc, gelu_cols):
    n_chunks = x_ref.shape[0] // rc

    def norm(r):
        rows = slice(r * rc, (r + 1) * rc)
        h_ref[rows, :] = _rms(x_ref[rows, :], g_ref[...]).astype(h_ref.dtype)

    norm(0)
    for r in range(n_chunks):
        if r + 1 < n_chunks:
            norm(r + 1)
        rows = slice(r * rc, (r + 1) * rc)
        acc = jnp.dot(h_ref[rows, :], w_ref[...], preferred_element_type=F32)
        if gelu_cols:
            acc = jnp.concatenate([_gelu(acc[:, :gelu_cols]), acc[:, gelu_cols:]], axis=1)
        o_ref[rows, :] = acc.astype(o_ref.dtype)


def norm_matmul(x, g, w, layer, out_dtype, tm, name, rc=256, gelu_cols=0):
    T, K = x.shape
    N = w.shape[2]
    tm = min(tm, T)
    return pl.pallas_call(
        functools.partial(_norm_matmul_kernel, rc=rc, gelu_cols=gelu_cols),
        out_shape=jax.ShapeDtypeStruct((T, N), out_dtype),
        grid=(T // tm,),
        in_specs=[pl.BlockSpec((tm, K), lambda i: (i, 0)),
                  pl.BlockSpec((1, K), lambda i: (0, 0)),
                  pl.BlockSpec((None, K, N), lambda i: (layer, 0, 0),
                               pipeline_mode=pl.Buffered(1))],
        out_specs=pl.BlockSpec((tm, N), lambda i: (i, 0)),
        scratch_shapes=[pltpu.VMEM((tm, K), BF16)],
        compiler_params=_params(("arbitrary",), vmem=VMEM_LIMIT_HIGH),
        name=name,
    )(x, g.reshape(1, K), w)


def _kv_post_kernel(kv_ref, g_ref, cos_ref, sin_ref, ckv_ref, kr_ref, *, rank):
    lat = kv_ref[:, :rank]
    ckv_ref[...] = _rms(lat, g_ref[...]).astype(ckv_ref.dtype)
    kr = kv_ref[:, rank:rank + LANES] * cos_ref[...]
    kr = kr + kv_ref[:, rank + LANES:] * sin_ref[...]
    kr_ref[...] = kr.astype(kr_ref.dtype)


def kv_post(kv_raw, g_lat, cos_t, sin_t, tm=1024):
    T, N = kv_raw.shape
    rank = N - 2 * LANES
    return pl.pallas_call(
        functools.partial(_kv_post_kernel, rank=rank),
        out_shape=(jax.ShapeDtypeStruct((T, rank), BF16),
                   jax.ShapeDtypeStruct((T, LANES), BF16)),
        grid=(T // tm,),
        in_specs=[pl.BlockSpec((tm, N), lambda i: (i, 0)),
                  pl.BlockSpec((1, rank), lambda i: (0, 0)),
                  pl.BlockSpec((tm, LANES), lambda i: (i, 0)),
                  pl.BlockSpec((tm, LANES), lambda i: (i, 0))],
        out_specs=(pl.BlockSpec((tm, rank), lambda i: (i, 0)),
                   pl.BlockSpec((tm, LANES), lambda i: (i, 0))),
        compiler_params=_params(("arbitrary",)),
        name="kv_post",
    )(kv_raw, g_lat.reshape(1, rank), cos_t, sin_t)


def _gmlp_kernel(u_ref, v_ref, gv_ref, w_ref, b_ref, o_ref, *, n_groups, tm):
    row = lax.broadcasted_iota(jnp.int32, (CHUNK, CHUNK), 0)
    col = lax.broadcasted_iota(jnp.int32, (CHUNK, CHUNK), 1)
    causal = row >= col
    w_tril = [jnp.where(causal, w_ref[g], 0.0).astype(BF16) for g in range(n_groups)]
    for c in range(tm // CHUNK):
        rows = slice(c * CHUNK, (c + 1) * CHUNK)
        ug = u_ref[rows, :].astype(F32)
        vn = _rms(v_ref[rows, :].astype(F32), gv_ref[...]).astype(BF16)
        for g in range(n_groups):
            cols = slice(g * CHUNK, (g + 1) * CHUNK)
            sv = jnp.dot(w_tril[g], vn[:, cols], preferred_element_type=F32) + b_ref[g]
            o_ref[rows, cols] = (ug[:, cols] * sv).astype(o_ref.dtype)


def gmlp_gate(z, g_v, w_sp, b_full, tm=512):
    T = z.shape[0]
    n_groups = w_sp.shape[0]
    gw = n_groups * CHUNK
    return pl.pallas_call(
        functools.partial(_gmlp_kernel, n_groups=n_groups, tm=tm),
        out_shape=jax.ShapeDtypeStruct((T, gw), BF16),
        grid=(T // tm,),
        in_specs=[pl.BlockSpec((tm, gw), lambda i: (i, 0)),
                  pl.BlockSpec((tm, gw), lambda i: (i, 1)),
                  pl.BlockSpec((1, gw), lambda i: (0, 0)),
                  pl.BlockSpec((n_groups, CHUNK, CHUNK), lambda i: (0, 0, 0)),
                  pl.BlockSpec((n_groups, CHUNK, CHUNK), lambda i: (0, 0, 0))],
        out_specs=pl.BlockSpec((tm, gw), lambda i: (i, 0)),
        compiler_params=_params(("arbitrary",)),
        name="gmlp_gate",
    )(z, z, g_v.reshape(1, gw), w_sp, b_full)


def _mix_out_kernel(x_ref, a_ref, q_ref, kv_ref, w_ref, o_ref, *, n_heads):
    ka = a_ref.shape[1]
    width = q_ref.shape[1]
    head_dim = width // n_heads
    scale = np.float32(head_dim ** -0.5)
    acc = jnp.dot(a_ref[...], w_ref[:ka, :], preferred_element_type=F32)
    mo = []
    for h in range(n_heads):
        cols = slice(h * head_dim, (h + 1) * head_dim)
        k = kv_ref[:, cols]
        v = kv_ref[:, width + h * head_dim:width + (h + 1) * head_dim]
        s = lax.dot_general(q_ref[:, cols], k, (((1,), (1,)), ((), ())),
                            preferred_element_type=F32) * scale
        m = jnp.max(s, axis=-1, keepdims=True)
        p = jnp.exp(s - m)
        l = jnp.sum(p, axis=-1, keepdims=True)
        o = jnp.dot(p.astype(BF16), v, preferred_element_type=F32) / l
        mo.append(o.astype(BF16))
    acc = acc + jnp.dot(jnp.concatenate(mo, axis=1), w_ref[ka:, :], preferred_element_type=F32)
    o_ref[...] = x_ref[...] + acc


def mix_out(x, main, z, q_col_block, kvm, w_out, layer, batch, tm=512):
    T, D = x.shape
    ka = main.shape[1]
    n_mem = kvm.shape[0] // batch
    width = kvm.shape[1] // 2
    per_b = T // batch // tm
    return pl.pallas_call(
        functools.partial(_mix_out_kernel, n_heads=MEM_HEADS),
        out_shape=jax.ShapeDtypeStruct((T, D), F32),
        grid=(batch, per_b),
        in_specs=[pl.BlockSpec((tm, D), lambda b, i: (b * per_b + i, 0)),
                  pl.BlockSpec((tm, ka), lambda b, i: (b * per_b + i, 0)),
                  pl.BlockSpec((tm, width), lambda b, i: (b * per_b + i, q_col_block)),
                  pl.BlockSpec((n_mem, 2 * width), lambda b, i: (b, 0)),
                  pl.BlockSpec((None, ka + width, D), lambda b, i: (layer, 0, 0))],
        out_specs=pl.BlockSpec((tm, D), lambda b, i: (b * per_b + i, 0)),
        compiler_params=_params(("arbitrary", "arbitrary")),
        name="mix_out",
    )(x, main, z, kvm, w_out)


def _conv_ffn_kernel(x_ref, g_ref, wg_ref, wv_ref, cw_ref, cb_ref, wd_ref, o_ref,
                     h_ref, a_ref, carry_ref, *, tm, rc, tf, tiles_per_seq, norm_out):
    i = pl.program_id(0)
    j = pl.program_id(1)

    @pl.when(jnp.logical_and(i == 0, j == 0))
    def _():
        carry_ref[...] = jnp.zeros(carry_ref.shape, F32)

    first_in_seq = (i % tiles_per_seq) == 0
    a_ref[:SUBLANES, :] = jnp.where(first_in_seq, 0.0, carry_ref[j])
    cw = cw_ref[...]
    cb = cb_ref[...]
    n_chunks = tm // rc

    def prep(r):
        rows = slice(r * rc, (r + 1) * rc)
        x = x_ref[rows, :]
        h_ref[rows, :] = _rms(x, g_ref[0:1, :]).astype(h_ref.dtype)
        o_ref[rows, :] = x

    def up(r):
        lo = SUBLANES + r * rc
        h = h_ref[r * rc:(r + 1) * rc, :]
        a_ref[lo:lo + rc, :tf] = jnp.dot(h, wg_ref[...], preferred_element_type=F32)
        a_ref[lo:lo + rc, tf:] = jnp.dot(h, wv_ref[...], preferred_element_type=F32)

    def gate_down(r):
        lo = SUBLANES + r * rc
        c = cb + a_ref[lo:lo + rc, :] * cw[CONV_W - 1:CONV_W, :]
        for k in range(CONV_W - 1):
            lag = CONV_W - 1 - k
            c = c + a_ref[lo - lag:lo - lag + rc, :] * cw[k:k + 1, :]
        cg = c[:, :tf]
        gated = (cg * jax.nn.sigmoid(cg) * c[:, tf:]).astype(BF16)
        o_ref[r * rc:(r + 1) * rc, :] += jnp.dot(gated, wd_ref[...],
                                                 preferred_element_type=F32)

    def step(first_ff_tile):
        if first_ff_tile:
            prep(0)
        up(0)
        for r in range(n_chunks):
            if r + 1 < n_chunks:
                if first_ff_tile:
                    prep(r + 1)
                up(r + 1)
            gate_down(r)

    @pl.when(j == 0)
    def _():
        step(True)

    @pl.when(j > 0)
    def _():
        step(False)

    carry_ref[j] = a_ref[tm:, :]

    if norm_out:
        @pl.when(j == pl.num_programs(1) - 1)
        def _():
            for r in range(n_chunks):
                rows = slice(r * rc, (r + 1) * rc)
                o_ref[rows, :] = _rms(o_ref[rows, :], g_ref[1:2, :])


def conv_ffn(x, g, w_up, conv_w, conv_b, w_down, layer, seq, out_gain=None,
             tm=1024, tf=512, rc=512):
    T, D = x.shape
    d_ff = w_down.shape[1]
    nff = d_ff // tf
    gains = g.reshape(1, D) if out_gain is None else jnp.stack([g, out_gain])
    kern = functools.partial(_conv_ffn_kernel, tm=tm, rc=rc, tf=tf, tiles_per_seq=seq // tm,
                             norm_out=out_gain is not None)
    return pl.pallas_call(
        kern,
        out_shape=jax.ShapeDtypeStruct((T, D), F32),
        grid=(T // tm, nff),
        in_specs=[pl.BlockSpec((tm, D), lambda i, j: (i, 0)),
                  pl.BlockSpec(gains.shape, lambda i, j: (0, 0)),
                  pl.BlockSpec((None, D, tf), lambda i, j: (layer, 0, j)),
                  pl.BlockSpec((None, D, tf), lambda i, j: (layer, 0, nff + j)),
                  pl.BlockSpec((CONV_W, 2 * tf), lambda i, j: (0, j)),
                  pl.BlockSpec((1, 2 * tf), lambda i, j: (0, j)),
                  pl.BlockSpec((None, tf, D), lambda i, j: (layer, j, 0))],
        out_specs=pl.BlockSpec((tm, D), lambda i, j: (i, 0)),
        scratch_shapes=[pltpu.VMEM((tm, D), BF16),
                        pltpu.VMEM((tm + SUBLANES, 2 * tf), F32),
                        pltpu.VMEM((nff, SUBLANES, 2 * tf), F32)],
        compiler_params=_params(("arbitrary", "arbitrary"), vmem=VMEM_LIMIT_HIGH),
        name="conv_ffn",
    )(x, gains, w_up, w_up, conv_w, conv_b.reshape(1, -1), w_down)


def _interleave_gate_value(w, tf):
    lead = w.shape[:-1]
    d_ff = w.shape[-1] // 2
    w = w.reshape(*lead, 2, d_ff // tf, tf)
    return jnp.swapaxes(w, -3, -2).reshape(*lead, 2 * d_ff)


def _q_prep_kernel(q_ref, g_ref, wm_ref, wr_ref, cos_ref, sin_ref, o_ref, *, n_heads, scale):
    qn = _rms(q_ref[...].astype(F32), g_ref[...]).astype(BF16)
    a = jnp.dot(qn, wm_ref[...], preferred_element_type=F32)
    r = jnp.dot(qn, wr_ref[...], preferred_element_type=F32)
    cos = cos_ref[...]
    sin = sin_ref[...]
    for h in range(n_heads):
        base = 2 * LANES * h
        o_ref[0, h, :, :LANES] = (a[:, base:base + LANES] * scale).astype(o_ref.dtype)
        rope = a[:, base + LANES:base + 2 * LANES] * cos + r[:, h * LANES:(h + 1) * LANES] * sin
        o_ref[0, h, :, LANES:] = (rope * scale).astype(o_ref.dtype)


def q_prep(z, g_q, w_main, w_rot, cos_t, sin_t, batch, n_heads, scale, tm=512):
    T = z.shape[0]
    seq = T // batch
    rank = w_main.shape[0]
    per_b = seq // tm
    kern = functools.partial(_q_prep_kernel, n_heads=n_heads, scale=np.float32(scale))
    return pl.pallas_call(
        kern,
        out_shape=jax.ShapeDtypeStruct((batch, n_heads, seq, 2 * LANES), BF16),
        grid=(batch, per_b),
        in_specs=[pl.BlockSpec((tm, rank), lambda b, i: (b * per_b + i, 0)),
                  pl.BlockSpec((1, rank), lambda b, i: (0, 0)),
                  pl.BlockSpec(w_main.shape, lambda b, i: (0, 0)),
                  pl.BlockSpec(w_rot.shape, lambda b, i: (0, 0)),
                  pl.BlockSpec((tm, LANES), lambda b, i: (b * per_b + i, 0)),
                  pl.BlockSpec((tm, LANES), lambda b, i: (b * per_b + i, 0))],
        out_specs=pl.BlockSpec((1, n_heads, tm, 2 * LANES), lambda b, i: (b, 0, i, 0)),
        compiler_params=_params(("arbitrary", "arbitrary")),
        name="q_prep",
    )(z, g_q.reshape(1, rank), w_main, w_rot, cos_t, sin_t)


def _mla_attn_kernel(q_ref, ckv_ref, kr_ref, wkv_ref, o_ref, k_ref, vt_ref, acc_ref,
                     sa_ref, sb_ref, ma_ref, mb_ref, *, tq, tk, hp, v_dim):
    i = pl.program_id(2)
    n_kb = k_ref.shape[1]

    @pl.when(i == 0)
    def _():
        ones = jnp.ones((vt_ref.shape[2] - v_dim, tk), vt_ref.dtype)
        for jb in range(n_kb):
            rows = slice(jb * tk, (jb + 1) * tk)
            for hh in range(hp):
                kv = jnp.dot(ckv_ref[rows, :], wkv_ref[hh], preferred_element_type=F32)
                k_ref[hh, jb, :, :LANES] = kv[:, :LANES].astype(k_ref.dtype)
                k_ref[hh, jb, :, LANES:] = kr_ref[rows, :]
                vt_ref[hh, jb, :v_dim, :] = kv[:, LANES:].T.astype(vt_ref.dtype)
                vt_ref[hh, jb, v_dim:, :] = ones

    acc_ref[...] = jnp.zeros(acc_ref.shape, F32)

    def scores(jb, dst_ref, mdst_ref, q_lo=0):
        for hh in range(hp):
            st = lax.dot_general(k_ref[hh, jb], q_ref[0, hh, q_lo:, :], (((1,), (1,)), ((), ())),
                                 preferred_element_type=F32)
            dst_ref[hh, :, q_lo:] = st
            mdst_ref[hh, :, q_lo:] = jnp.max(st, axis=0, keepdims=True)

    def consume(jb, src_ref, msrc_ref, m_all, diag=False, q_lo=0):
        out = []
        for hh in range(hp):
            m = m_all[hh][:, q_lo:]
            st = src_ref[hh, :, q_lo:]
            if diag:
                k_pos = lax.broadcasted_iota(jnp.int32, st.shape, 0)
                q_pos = lax.broadcasted_iota(jnp.int32, st.shape, 1)
                st = jnp.where(k_pos <= q_pos, st, -jnp.inf)
                m_blk = jnp.max(st, axis=0, keepdims=True)
            else:
                m_blk = msrc_ref[hh, :, q_lo:]
            m_new = jnp.maximum(m, m_blk)
            alpha = jnp.exp2(m - m_new)
            p = jnp.exp2(st - m_new).astype(BF16)
            pv = jnp.dot(vt_ref[hh, jb], p, preferred_element_type=F32)
            acc_ref[hh, :, q_lo:] = alpha * acc_ref[hh, :, q_lo:] + pv
            if q_lo:
                m_new = jnp.concatenate([m_all[hh][:, :q_lo], m_new], axis=1)
            out.append(m_new)
        return tuple(out)

    def pair(t, m_all):
        scores(2 * t + 1, sb_ref, mb_ref)
        m_all = consume(2 * t, sa_ref, ma_ref, m_all)
        scores(2 * t + 2, sa_ref, ma_ref)
        return consume(2 * t + 1, sb_ref, mb_ref, m_all)

    scores(0, sa_ref, ma_ref)
    m_all = lax.fori_loop(0, i, pair, tuple(jnp.full((1, tq), -jnp.inf, F32) for _ in range(hp)))
    scores(2 * i + 1, sb_ref, mb_ref, q_lo=tk)
    m_all = consume(2 * i, sa_ref, ma_ref, m_all, diag=True)
    consume(2 * i + 1, sb_ref, mb_ref, m_all, diag=True, q_lo=tk)
    for hh in range(hp):
        o = acc_ref[hh, :v_dim, :] / acc_ref[hh, v_dim:v_dim + 1, :]
        o_ref[:, hh * v_dim:(hh + 1) * v_dim] = o.T.astype(o_ref.dtype)


def mla_attn(q, c_kv, kr_pad, w_kv, tq=1024, hp=4):
    tk = tq // 2
    batch, n_heads, seq, qk_dim = q.shape
    rank = c_kv.shape[1]
    v_dim = w_kv.shape[2] - LANES
    v_rows = v_dim + 2 * SUBLANES
    nq = seq // tq
    return pl.pallas_call(
        functools.partial(_mla_attn_kernel, tq=tq, tk=tk, hp=hp, v_dim=v_dim),
        out_shape=jax.ShapeDtypeStruct((batch * seq, n_heads * v_dim), BF16),
        grid=(batch, n_heads // hp, nq),
        in_specs=[pl.BlockSpec((1, hp, tq, qk_dim), lambda b, h, i: (b, h, i, 0)),
                  pl.BlockSpec((seq, rank), lambda b, h, i: (b, 0)),
                  pl.BlockSpec((seq, LANES), lambda b, h, i: (b, 0)),
                  pl.BlockSpec((hp, rank, LANES + v_dim), lambda b, h, i: (h, 0, 0))],
        out_specs=pl.BlockSpec((tq, hp * v_dim), lambda b, h, i: (b * nq + i, h)),
        scratch_shapes=[pltpu.VMEM((hp, seq // tk, tk, qk_dim), BF16),
                        pltpu.VMEM((hp, seq // tk, v_rows, tk), BF16),
                        pltpu.VMEM((hp, v_rows, tq), F32),
                        pltpu.VMEM((hp, tk, tq), F32),
                        pltpu.VMEM((hp, tk, tq), F32),
                        pltpu.VMEM((hp, 1, tq), F32),
                        pltpu.VMEM((hp, 1, tq), F32)],
        compiler_params=_params(("arbitrary", "arbitrary", "arbitrary"), vmem=VMEM_LIMIT_HIGH),
        name="mla_attn",
    )(q, c_kv, kr_pad, w_kv)


def _rot_cols(w):
    half = w.shape[-1] // 2
    return jnp.concatenate([-w[..., half:], w[..., :half]], axis=-1)


def _pad_lanes(w):
    pad = [(0, 0)] * (w.ndim - 1) + [(0, LANES - w.shape[-1])]
    return jnp.pad(w, pad)


def kernel(x, mem, positions, g_mix, g_ffn, g_final, w_in_a, g_v, w_sp, b_sp, g_kv, w_kv_a,
           g_kv_lat, w_in_b, g_q_lat, w_uq, w_uk, w_uv, g_mem, w_mem_kv, w_out, w_ffn_up,
           conv_w, conv_b, w_ffn_down):
    batch, seq, d_model = x.shape
    depth = g_mix.shape[0]
    n_a = w_in_a.shape[0]
    n_mem = mem.shape[1]
    T = batch * seq
    kv_rank = g_kv_lat.shape[0]
    q_rank = g_q_lat.shape[1]
    n_heads, nope_dim = w_uk.shape[2], w_uk.shape[3]
    rope_dim = w_kv_a.shape[1] - kv_rank
    mem_w = w_mem_kv.shape[2] // 2
    g_w = g_v.shape[1]
    scale = (nope_dim + rope_dim) ** -0.5 * np.log2(np.e)

    xs = x.reshape(T, d_model)
    mems = mem.reshape(batch * n_mem, d_model)

    inv = 1.0 / (ROPE_THETA ** (jnp.arange(0, rope_dim, 2, dtype=F32) / rope_dim))
    inv_row = _pad_lanes(jnp.concatenate([inv, inv])).reshape(1, LANES)
    cos_t, sin_t = rope_tables(positions.reshape(T, 1), inv_row)

    w_in_a_bf, w_in_b_bf, w_mem_kv_bf = (w.astype(BF16) for w in (w_in_a, w_in_b, w_mem_kv))
    w_out_bf, w_up_bf, w_down_bf = (w.astype(BF16) for w in (w_out, w_ffn_up, w_ffn_down))

    c_kv = kr_pad = None
    for l in range(depth):
        if l == n_a:
            w_kr = w_kv_a[:, kv_rank:]
            w_kv_cat = jnp.concatenate(
                [w_kv_a[:, :kv_rank], _pad_lanes(w_kr), _pad_lanes(_rot_cols(w_kr))], axis=1)
            kv_raw = norm_matmul(xs, g_kv, w_kv_cat.astype(BF16)[None], 0, F32,
                                 tm=1024, name="kv_proj")
            c_kv, kr_pad = kv_post(kv_raw, g_kv_lat, cos_t, sin_t)

        kvm = norm_matmul(mems, g_mem[l], w_mem_kv_bf, l, BF16,
                          tm=1024, name="mem_kv_proj")
        if l < n_a:
            z = norm_matmul(xs, g_mix[l], w_in_a_bf, l, BF16, tm=1024, name="in_proj_a",
                            gelu_cols=2 * g_w)
            b_full = jnp.broadcast_to(b_sp[l][:, :, None], w_sp[l].shape)
            main = gmlp_gate(z, g_v[l], w_sp[l], b_full)
            q_col_block = (2 * g_w) // mem_w
        else:
            j = l - n_a
            z = norm_matmul(xs, g_mix[l], w_in_b_bf, j, BF16, tm=1024, name="in_proj_b")
            wq = w_uq[j].reshape(q_rank, n_heads, nope_dim + rope_dim)
            wq_rope = wq[..., nope_dim:]
            w_main = jnp.concatenate(
                [wq[..., :nope_dim], _pad_lanes(wq_rope)], axis=-1).reshape(q_rank, -1)
            w_rot = _pad_lanes(_rot_cols(wq_rope)).reshape(q_rank, -1)
            q = q_prep(z, g_q_lat[j], w_main.astype(BF16), w_rot.astype(BF16), cos_t, sin_t,
                       batch, n_heads, scale)
            w_kv_h = jnp.concatenate([w_uk[j], w_uv[j]], axis=-1)
            w_kv_h = jnp.transpose(w_kv_h, (1, 0, 2)).astype(BF16)
            main = mla_attn(q, c_kv, kr_pad, w_kv_h)
            q_col_block = q_rank // mem_w
        xs = mix_out(xs, main, z, q_col_block, kvm, w_out_bf, l, batch)
        xs = conv_ffn(xs, g_ffn[l], w_up_bf,
                      _interleave_gate_value(conv_w[l], FFN_TF),
                      _interleave_gate_value(conv_b[l], FFN_TF),
                      w_down_bf, l, seq, tf=FFN_TF,
                      out_gain=g_final if l == depth - 1 else None)
    return xs.reshape(batch, seq, d_model)
```

```python
import functools

import jax
import jax.numpy as jnp
import numpy as np
from jax import lax
from jax.experimental import pallas as pl
from jax.experimental.pallas import tpu as pltpu

EPS = 1e-6
ROPE_THETA = 10000.0
MEM_HEADS = 4
CHUNK = 128
LANES = 128
SUBLANES = 8
CONV_W = 3
FFN_TF = 512
VMEM_LIMIT = 56 * 1024 * 1024
VMEM_LIMIT_HIGH = 62 * 1024 * 1024

F32 = jnp.float32
BF16 = jnp.bfloat16


def _params(semantics, vmem=VMEM_LIMIT):
    return pltpu.CompilerParams(dimension_semantics=semantics, vmem_limit_bytes=vmem)


def _rms(x, g):
    ms = jnp.mean(x * x, axis=-1, keepdims=True)
    return x * lax.rsqrt(ms + EPS) * g


def _gelu(x):
    return 0.5 * x * (1.0 + lax.erf(x * np.float32(np.sqrt(0.5))))


def _rope_table_kernel(pos_ref, inv_ref, cos_ref, sin_ref):
    ang = pos_ref[...].astype(F32) * inv_ref[...]
    cos_ref[...] = jnp.cos(ang)
    sin_ref[...] = jnp.sin(ang)


def rope_tables(pos_col, inv_row, tm=2048):
    T = pos_col.shape[0]
    return pl.pallas_call(
        _rope_table_kernel,
        out_shape=(jax.ShapeDtypeStruct((T, LANES), F32),) * 2,
        grid=(T // tm,),
        in_specs=[pl.BlockSpec((tm, 1), lambda i: (i, 0)),
                  pl.BlockSpec((1, LANES), lambda i: (0, 0))],
        out_specs=(pl.BlockSpec((tm, LANES), lambda i: (i, 0)),) * 2,
        compiler_params=_params(("arbitrary",)),
        name="rope_tables",
    )(pos_col, inv_row)


def _norm_matmul_kernel(x_ref, g_ref, w_ref, o_ref, h_ref, *, rc, gelu_cols):
    n_chunks = x_ref.shape[0] // rc

    def norm(r):
        rows = slice(r * rc, (r + 1) * rc)
        h_ref[rows, :] = _rms(x_ref[rows, :], g_ref[...]).astype(h_ref.dtype)

    norm(0)
    for r in range(n_chunks):
        if r + 1 < n_chunks:
            norm(r + 1)
        rows = slice(r * rc, (r + 1) * rc)
        acc = jnp.dot(h_ref[rows, :], w_ref[...], preferred_element_type=F32)
        if gelu_cols:
            acc = jnp.concatenate([_gelu(acc[:, :gelu_cols]), acc[:, gelu_cols:]], axis=1)
        o_ref[rows, :] = acc.astype(o_ref.dtype)


def norm_matmul(x, g, w, layer, out_dtype, tm, name, rc=256, gelu_cols=0):
    T, K = x.shape
    N = w.shape[2]
    tm = min(tm, T)
    return pl.pallas_call(
        functools.partial(_norm_matmul_kernel, rc=rc, gelu_cols=gelu_cols),
        out_shape=jax.ShapeDtypeStruct((T, N), out_dtype),
        grid=(T // tm,),
        in_specs=[pl.BlockSpec((tm, K), lambda i: (i, 0)),
                  pl.BlockSpec((1, K), lambda i: (0, 0)),
                  pl.BlockSpec((None, K, N), lambda i: (layer, 0, 0),
                               pipeline_mode=pl.Buffered(1))],
        out_specs=pl.BlockSpec((tm, N), lambda i: (i, 0)),
        scratch_shapes=[pltpu.VMEM((tm, K), BF16)],
        compiler_params=_params(("arbitrary",), vmem=VMEM_LIMIT_HIGH),
        name=name,
    )(x, g.reshape(1, K), w)


def _kv_post_kernel(kv_ref, g_ref, cos_ref, sin_ref, ckv_ref, kr_ref, *, rank):
    lat = kv_ref[:, :rank]
    ckv_ref[...] = _rms(lat, g_ref[...]).astype(ckv_ref.dtype)
    kr = kv_ref[:, rank:rank + LANES] * cos_ref[...]
    kr = kr + kv_ref[:, rank + LANES:] * sin_ref[...]
    kr_ref[...] = kr.astype(kr_ref.dtype)


def kv_post(kv_raw, g_lat, cos_t, sin_t, tm=1024):
    T, N = kv_raw.shape
    rank = N - 2 * LANES
    return pl.pallas_call(
        functools.partial(_kv_post_kernel, rank=rank),
        out_shape=(jax.ShapeDtypeStruct((T, rank), BF16),
                   jax.ShapeDtypeStruct((T, LANES), BF16)),
        grid=(T // tm,),
        in_specs=[pl.BlockSpec((tm, N), lambda i: (i, 0)),
                  pl.BlockSpec((1, rank), lambda i: (0, 0)),
                  pl.BlockSpec((tm, LANES), lambda i: (i, 0)),
                  pl.BlockSpec((tm, LANES), lambda i: (i, 0))],
        out_specs=(pl.BlockSpec((tm, rank), lambda i: (i, 0)),
                   pl.BlockSpec((tm, LANES), lambda i: (i, 0))),
        compiler_params=_params(("arbitrary",)),
        name="kv_post",
    )(kv_raw, g_lat.reshape(1, rank), cos_t, sin_t)


def _mix_out_gmlp_kernel(x_ref, u_ref, v_ref, q_ref, kv_ref, gv_ref, wsp_ref, b_ref, w_ref,
                         o_ref, mix_ref, *, n_groups, n_heads, rc):
    tm = x_ref.shape[0]
    gw = n_groups * CHUNK
    width = q_ref.shape[1]
    head_dim = width // n_heads
    scale = np.float32(head_dim ** -0.5)
    row = lax.broadcasted_iota(jnp.int32, (CHUNK, CHUNK), 0)
    col = lax.broadcasted_iota(jnp.int32, (CHUNK, CHUNK), 1)
    causal = row >= col
    w_tril = [jnp.where(causal, wsp_ref[g], 0.0).astype(BF16) for g in range(n_groups)]

    def mixer(r):
        for c in range(r * rc // CHUNK, (r + 1) * rc // CHUNK):
            rows = slice(c * CHUNK, (c + 1) * CHUNK)
            ug = u_ref[rows, :].astype(F32)
            vn = _rms(v_ref[rows, :].astype(F32), gv_ref[...]).astype(BF16)
            for g in range(n_groups):
                cols = slice(g * CHUNK, (g + 1) * CHUNK)
                sv = jnp.dot(w_tril[g], vn[:, cols], preferred_element_type=F32) + b_ref[g]
                mix_ref[rows, cols] = (ug[:, cols] * sv).astype(BF16)
        rows = slice(r * rc, (r + 1) * rc)
        for h in range(n_heads):
            cols = slice(h * head_dim, (h + 1) * head_dim)
            k = kv_ref[:, cols]
            v = kv_ref[:, width + h * head_dim:width + (h + 1) * head_dim]
            s = lax.dot_general(q_ref[rows, cols], k, (((1,), (1,)), ((), ())),
                                preferred_element_type=F32) * scale
            m = jnp.max(s, axis=-1, keepdims=True)
            p = jnp.exp(s - m)
            l = jnp.sum(p, axis=-1, keepdims=True)
            o = jnp.dot(p.astype(BF16), v, preferred_element_type=F32) / l
            mix_ref[rows, gw + h * head_dim:gw + (h + 1) * head_dim] = o.astype(BF16)

    n_chunks = tm // rc
    mixer(0)
    for r in range(n_chunks):
        if r + 1 < n_chunks:
            mixer(r + 1)
        rows = slice(r * rc, (r + 1) * rc)
        o_ref[rows, :] = x_ref[rows, :] + jnp.dot(mix_ref[rows, :], w_ref[...],
                                                  preferred_element_type=F32)


def mix_out_gmlp(x, z, g_v, w_sp, b_full, kvm, w_out, layer, batch, tm=512, rc=256):
    T, D = x.shape
    n_groups = w_sp.shape[0]
    gw = n_groups * CHUNK
    n_mem = kvm.shape[0] // batch
    width = kvm.shape[1] // 2
    per_b = T // batch // tm
    kern = functools.partial(_mix_out_gmlp_kernel, n_groups=n_groups, n_heads=MEM_HEADS, rc=rc)
    return pl.pallas_call(
        kern,
        out_shape=jax.ShapeDtypeStruct((T, D), F32),
        grid=(batch, per_b),
        in_specs=[pl.BlockSpec((tm, D), lambda b, i: (b * per_b + i, 0)),
                  pl.BlockSpec((tm, gw), lambda b, i: (b * per_b + i, 0)),
                  pl.BlockSpec((tm, gw), lambda b, i: (b * per_b + i, 1)),
                  pl.BlockSpec((tm, width), lambda b, i: (b * per_b + i, 2 * gw // width)),
                  pl.BlockSpec((n_mem, 2 * width), lambda b, i: (b, 0)),
                  pl.BlockSpec((1, gw), lambda b, i: (0, 0)),
                  pl.BlockSpec((n_groups, CHUNK, CHUNK), lambda b, i: (0, 0, 0)),
                  pl.BlockSpec((n_groups, CHUNK, CHUNK), lambda b, i: (0, 0, 0)),
                  pl.BlockSpec((None, gw + width, D), lambda b, i: (layer, 0, 0))],
        out_specs=pl.BlockSpec((tm, D), lambda b, i: (b * per_b + i, 0)),
        scratch_shapes=[pltpu.VMEM((tm, gw + width), BF16)],
        compiler_params=_params(("arbitrary", "arbitrary")),
        name="mix_out_gmlp",
    )(x, z, z, z, kvm, g_v.reshape(1, gw), w_sp, b_full, w_out)


def _mix_out_kernel(x_ref, a_ref, q_ref, kv_ref, w_ref, o_ref, *, n_heads):
    ka = a_ref.shape[1]
    width = q_ref.shape[1]
    head_dim = width // n_heads
    scale = np.float32(head_dim ** -0.5)
    acc = jnp.dot(a_ref[...], w_ref[:ka, :], preferred_element_type=F32)
    mo = []
    for h in range(n_heads):
        cols = slice(h * head_dim, (h + 1) * head_dim)
        k = kv_ref[:, cols]
        v = kv_ref[:, width + h * head_dim:width + (h + 1) * head_dim]
        s = lax.dot_general(q_ref[:, cols], k, (((1,), (1,)), ((), ())),
                            preferred_element_type=F32) * scale
        m = jnp.max(s, axis=-1, keepdims=True)
        p = jnp.exp(s - m)
        l = jnp.sum(p, axis=-1, keepdims=True)
        o = jnp.dot(p.astype(BF16), v, preferred_element_type=F32) / l
        mo.append(o.astype(BF16))
    acc = acc + jnp.dot(jnp.concatenate(mo, axis=1), w_ref[ka:, :], preferred_element_type=F32)
    o_ref[...] = x_ref[...] + acc


def mix_out(x, main, z, q_col_block, kvm, w_out, layer, batch, tm=512):
    T, D = x.shape
    ka = main.shape[1]
    n_mem = kvm.shape[0] // batch
    width = kvm.shape[1] // 2
    per_b = T // batch // tm
    return pl.pallas_call(
        functools.partial(_mix_out_kernel, n_heads=MEM_HEADS),
        out_shape=jax.ShapeDtypeStruct((T, D), F32),
        grid=(batch, per_b),
        in_specs=[pl.BlockSpec((tm, D), lambda b, i: (b * per_b + i, 0)),
                  pl.BlockSpec((tm, ka), lambda b, i: (b * per_b + i, 0)),
                  pl.BlockSpec((tm, width), lambda b, i: (b * per_b + i, q_col_block)),
                  pl.BlockSpec((n_mem, 2 * width), lambda b, i: (b, 0)),
                  pl.BlockSpec((None, ka + width, D), lambda b, i: (layer, 0, 0))],
        out_specs=pl.BlockSpec((tm, D), lambda b, i: (b * per_b + i, 0)),
        compiler_params=_params(("arbitrary", "arbitrary")),
        name="mix_out",
    )(x, main, z, kvm, w_out)


def _conv_ffn_kernel(x_ref, g_ref, wg_ref, wv_ref, cw_ref, cb_ref, wd_ref, o_ref,
                     h_ref, a_ref, carry_ref, *, tm, rc, tf, tiles_per_seq, norm_out):
    i = pl.program_id(0)
    j = pl.program_id(1)

    @pl.when(jnp.logical_and(i == 0, j == 0))
    def _():
        carry_ref[...] = jnp.zeros(carry_ref.shape, F32)

    @pl.when(j == 0)
    def _():
        for r in range(tm // rc):
            rows = slice(r * rc, (r + 1) * rc)
            x = x_ref[rows, :]
            h_ref[rows, :] = _rms(x, g_ref[0:1, :]).astype(h_ref.dtype)
            o_ref[rows, :] = x

    first_in_seq = (i % tiles_per_seq) == 0
    a_ref[:SUBLANES, :] = jnp.where(first_in_seq, 0.0, carry_ref[j])
    cw = cw_ref[...]
    cb = cb_ref[...]
    def up(r):
        lo = SUBLANES + r * rc
        h = h_ref[r * rc:(r + 1) * rc, :]
        a_ref[lo:lo + rc, :tf] = jnp.dot(h, wg_ref[...], preferred_element_type=F32)
        a_ref[lo:lo + rc, tf:] = jnp.dot(h, wv_ref[...], preferred_element_type=F32)

    def gate_down(r):
        lo = SUBLANES + r * rc
        c = cb + a_ref[lo:lo + rc, :] * cw[CONV_W - 1:CONV_W, :]
        for k in range(CONV_W - 1):
            lag = CONV_W - 1 - k
            c = c + a_ref[lo - lag:lo - lag + rc, :] * cw[k:k + 1, :]
        cg = c[:, :tf]
        gated = (cg * jax.nn.sigmoid(cg) * c[:, tf:]).astype(BF16)
        o_ref[r * rc:(r + 1) * rc, :] += jnp.dot(gated, wd_ref[...],
                                                 preferred_element_type=F32)

    n_chunks = tm // rc
    up(0)
    for r in range(n_chunks):
        if r + 1 < n_chunks:
            up(r + 1)
        gate_down(r)
    carry_ref[j] = a_ref[tm:, :]

    if norm_out:
        @pl.when(j == pl.num_programs(1) - 1)
        def _():
            for r in range(n_chunks):
                rows = slice(r * rc, (r + 1) * rc)
                o_ref[rows, :] = _rms(o_ref[rows, :], g_ref[1:2, :])


def conv_ffn(x, g, w_up, conv_w, conv_b, w_down, layer, seq, out_gain=None,
             tm=1024, tf=512, rc=512):
    T, D = x.shape
    d_ff = w_down.shape[1]
    nff = d_ff // tf
    gains = g.reshape(1, D) if out_gain is None else jnp.stack([g, out_gain])
    kern = functools.partial(_conv_ffn_kernel, tm=tm, rc=rc, tf=tf, tiles_per_seq=seq // tm,
                             norm_out=out_gain is not None)
    return pl.pallas_call(
        kern,
        out_shape=jax.ShapeDtypeStruct((T, D), F32),
        grid=(T // tm, nff),
        in_specs=[pl.BlockSpec((tm, D), lambda i, j: (i, 0)),
                  pl.BlockSpec(gains.shape, lambda i, j: (0, 0)),
                  pl.BlockSpec((None, D, tf), lambda i, j: (layer, 0, j)),
                  pl.BlockSpec((None, D, tf), lambda i, j: (layer, 0, nff + j)),
                  pl.BlockSpec((CONV_W, 2 * tf), lambda i, j: (0, j)),
                  pl.BlockSpec((1, 2 * tf), lambda i, j: (0, j)),
                  pl.BlockSpec((None, tf, D), lambda i, j: (layer, j, 0))],
        out_specs=pl.BlockSpec((tm, D), lambda i, j: (i, 0)),
        scratch_shapes=[pltpu.VMEM((tm, D), BF16),
                        pltpu.VMEM((tm + SUBLANES, 2 * tf), F32),
                        pltpu.VMEM((nff, SUBLANES, 2 * tf), F32)],
        compiler_params=_params(("arbitrary", "arbitrary"), vmem=VMEM_LIMIT_HIGH),
        name="conv_ffn",
    )(x, gains, w_up, w_up, conv_w, conv_b.reshape(1, -1), w_down)


def _interleave_gate_value(w, tf):
    lead = w.shape[:-1]
    d_ff = w.shape[-1] // 2
    w = w.reshape(*lead, 2, d_ff // tf, tf)
    return jnp.swapaxes(w, -3, -2).reshape(*lead, 2 * d_ff)


def _q_prep_kernel(q_ref, g_ref, wm_ref, wr_ref, cos_ref, sin_ref, o_ref, *, n_heads, scale):
    qn = _rms(q_ref[...].astype(F32), g_ref[...]).astype(BF16)
    a = jnp.dot(qn, wm_ref[...], preferred_element_type=F32)
    r = jnp.dot(qn, wr_ref[...], preferred_element_type=F32)
    cos = cos_ref[...]
    sin = sin_ref[...]
    for h in range(n_heads):
        base = 2 * LANES * h
        o_ref[0, h, :, :LANES] = (a[:, base:base + LANES] * scale).astype(o_ref.dtype)
        rope = a[:, base + LANES:base + 2 * LANES] * cos + r[:, h * LANES:(h + 1) * LANES] * sin
        o_ref[0, h, :, LANES:] = (rope * scale).astype(o_ref.dtype)


def q_prep(z, g_q, w_main, w_rot, cos_t, sin_t, batch, n_heads, scale, tm=512):
    T = z.shape[0]
    seq = T // batch
    rank = w_main.shape[0]
    per_b = seq // tm
    kern = functools.partial(_q_prep_kernel, n_heads=n_heads, scale=np.float32(scale))
    return pl.pallas_call(
        kern,
        out_shape=jax.ShapeDtypeStruct((batch, n_heads, seq, 2 * LANES), BF16),
        grid=(batch, per_b),
        in_specs=[pl.BlockSpec((tm, rank), lambda b, i: (b * per_b + i, 0)),
                  pl.BlockSpec((1, rank), lambda b, i: (0, 0)),
                  pl.BlockSpec(w_main.shape, lambda b, i: (0, 0)),
                  pl.BlockSpec(w_rot.shape, lambda b, i: (0, 0)),
                  pl.BlockSpec((tm, LANES), lambda b, i: (b * per_b + i, 0)),
                  pl.BlockSpec((tm, LANES), lambda b, i: (b * per_b + i, 0))],
        out_specs=pl.BlockSpec((1, n_heads, tm, 2 * LANES), lambda b, i: (b, 0, i, 0)),
        compiler_params=_params(("arbitrary", "arbitrary")),
        name="q_prep",
    )(z, g_q.reshape(1, rank), w_main, w_rot, cos_t, sin_t)


def _mla_attn_kernel(q_ref, ckv_ref, kr_ref, wkv_ref, o_ref, k_ref, vt_ref, acc_ref,
                     sa_ref, sb_ref, ma_ref, mb_ref, *, tq, tk, hp, v_dim):
    i = pl.program_id(2)
    n_kb = k_ref.shape[1]

    @pl.when(i == 0)
    def _():
        ones = jnp.ones((vt_ref.shape[2] - v_dim, tk), vt_ref.dtype)
        for jb in range(n_kb):
            rows = slice(jb * tk, (jb + 1) * tk)
            for hh in range(hp):
                kv = jnp.dot(ckv_ref[rows, :], wkv_ref[hh], preferred_element_type=F32)
                k_ref[hh, jb, :, :LANES] = kv[:, :LANES].astype(k_ref.dtype)
                k_ref[hh, jb, :, LANES:] = kr_ref[rows, :]
                vt_ref[hh, jb, :v_dim, :] = kv[:, LANES:].T.astype(vt_ref.dtype)
                vt_ref[hh, jb, v_dim:, :] = ones

    acc_ref[...] = jnp.zeros(acc_ref.shape, F32)

    def scores(jb, dst_ref, mdst_ref, q_lo=0):
        for hh in range(hp):
            st = lax.dot_general(k_ref[hh, jb], q_ref[0, hh, q_lo:, :], (((1,), (1,)), ((), ())),
                                 preferred_element_type=F32)
            dst_ref[hh, :, q_lo:] = st
            mdst_ref[hh, :, q_lo:] = jnp.max(st, axis=0, keepdims=True)

    def consume(jb, src_ref, msrc_ref, m_all, diag=False, q_lo=0):
        out = []
        for hh in range(hp):
            m = m_all[hh][:, q_lo:]
            st = src_ref[hh, :, q_lo:]
            if diag:
                k_pos = lax.broadcasted_iota(jnp.int32, st.shape, 0)
                q_pos = lax.broadcasted_iota(jnp.int32, st.shape, 1)
                st = jnp.where(k_pos <= q_pos, st, -jnp.inf)
                m_blk = jnp.max(st, axis=0, keepdims=True)
            else:
                m_blk = msrc_ref[hh, :, q_lo:]
            m_new = jnp.maximum(m, m_blk)
            alpha = jnp.exp2(m - m_new)
            p = jnp.exp2(st - m_new).astype(BF16)
            pv = jnp.dot(vt_ref[hh, jb], p, preferred_element_type=F32)
            acc_ref[hh, :, q_lo:] = alpha * acc_ref[hh, :, q_lo:] + pv
            if q_lo:
                m_new = jnp.concatenate([m_all[hh][:, :q_lo], m_new], axis=1)
            out.append(m_new)
        return tuple(out)

    def pair(t, m_all):
        scores(2 * t + 1, sb_ref, mb_ref)
        m_all = consume(2 * t, sa_ref, ma_ref, m_all)
        scores(2 * t + 2, sa_ref, ma_ref)
        return consume(2 * t + 1, sb_ref, mb_ref, m_all)

    scores(0, sa_ref, ma_ref)
    m_all = lax.fori_loop(0, i, pair, tuple(jnp.full((1, tq), -jnp.inf, F32) for _ in range(hp)))
    scores(2 * i + 1, sb_ref, mb_ref, q_lo=tk)
    m_all = consume(2 * i, sa_ref, ma_ref, m_all, diag=True)
    consume(2 * i + 1, sb_ref, mb_ref, m_all, diag=True, q_lo=tk)
    for hh in range(hp):
        o = acc_ref[hh, :v_dim, :] / acc_ref[hh, v_dim:v_dim + 1, :]
        o_ref[:, hh * v_dim:(hh + 1) * v_dim] = o.T.astype(o_ref.dtype)


def mla_attn(q, c_kv, kr_pad, w_kv, tq=1024, hp=4):
    tk = tq // 2
    batch, n_heads, seq, qk_dim = q.shape
    rank = c_kv.shape[1]
    v_dim = w_kv.shape[2] - LANES
    v_rows = v_dim + 2 * SUBLANES
    nq = seq // tq
    return pl.pallas_call(
        functools.partial(_mla_attn_kernel, tq=tq, tk=tk, hp=hp, v_dim=v_dim),
        out_shape=jax.ShapeDtypeStruct((batch * seq, n_heads * v_dim), BF16),
        grid=(batch, n_heads // hp, nq),
        in_specs=[pl.BlockSpec((1, hp, tq, qk_dim), lambda b, h, i: (b, h, i, 0)),
                  pl.BlockSpec((seq, rank), lambda b, h, i: (b, 0)),
                  pl.BlockSpec((seq, LANES), lambda b, h, i: (b, 0)),
                  pl.BlockSpec((hp, rank, LANES + v_dim), lambda b, h, i: (h, 0, 0))],
        out_specs=pl.BlockSpec((tq, hp * v_dim), lambda b, h, i: (b * nq + i, h)),
        scratch_shapes=[pltpu.VMEM((hp, seq // tk, tk, qk_dim), BF16),
                        pltpu.VMEM((hp, seq // tk, v_rows, tk), BF16),
                        pltpu.VMEM((hp, v_rows, tq), F32),
                        pltpu.VMEM((hp, tk, tq), F32),
                        pltpu.VMEM((hp, tk, tq), F32),
                        pltpu.VMEM((hp, 1, tq), F32),
                        pltpu.VMEM((hp, 1, tq), F32)],
        compiler_params=_params(("arbitrary", "arbitrary", "arbitrary"), vmem=VMEM_LIMIT_HIGH),
        name="mla_attn",
    )(q, c_kv, kr_pad, w_kv)


def _rot_cols(w):
    half = w.shape[-1] // 2
    return jnp.concatenate([-w[..., half:], w[..., :half]], axis=-1)


def _pad_lanes(w):
    pad = [(0, 0)] * (w.ndim - 1) + [(0, LANES - w.shape[-1])]
    return jnp.pad(w, pad)


def kernel(x, mem, positions, g_mix, g_ffn, g_final, w_in_a, g_v, w_sp, b_sp, g_kv, w_kv_a,
           g_kv_lat, w_in_b, g_q_lat, w_uq, w_uk, w_uv, g_mem, w_mem_kv, w_out, w_ffn_up,
           conv_w, conv_b, w_ffn_down):
    batch, seq, d_model = x.shape
    depth = g_mix.shape[0]
    n_a = w_in_a.shape[0]
    n_mem = mem.shape[1]
    T = batch * seq
    kv_rank = g_kv_lat.shape[0]
    q_rank = g_q_lat.shape[1]
    n_heads, nope_dim = w_uk.shape[2], w_uk.shape[3]
    rope_dim = w_kv_a.shape[1] - kv_rank
    mem_w = w_mem_kv.shape[2] // 2
    g_w = g_v.shape[1]
    scale = (nope_dim + rope_dim) ** -0.5 * np.log2(np.e)

    xs = x.reshape(T, d_model)
    mems = mem.reshape(batch * n_mem, d_model)

    inv = 1.0 / (ROPE_THETA ** (jnp.arange(0, rope_dim, 2, dtype=F32) / rope_dim))
    inv_row = _pad_lanes(jnp.concatenate([inv, inv])).reshape(1, LANES)
    cos_t, sin_t = rope_tables(positions.reshape(T, 1), inv_row)

    w_in_a_bf, w_in_b_bf, w_mem_kv_bf = (w.astype(BF16) for w in (w_in_a, w_in_b, w_mem_kv))
    w_out_bf, w_up_bf, w_down_bf = (w.astype(BF16) for w in (w_out, w_ffn_up, w_ffn_down))

    c_kv = kr_pad = None
    for l in range(depth):
        if l == n_a:
            w_kr = w_kv_a[:, kv_rank:]
            w_kv_cat = jnp.concatenate(
                [w_kv_a[:, :kv_rank], _pad_lanes(w_kr), _pad_lanes(_rot_cols(w_kr))], axis=1)
            kv_raw = norm_matmul(xs, g_kv, w_kv_cat.astype(BF16)[None], 0, F32,
                                 tm=1024, name="kv_proj")
            c_kv, kr_pad = kv_post(kv_raw, g_kv_lat, cos_t, sin_t)

        kvm = norm_matmul(mems, g_mem[l], w_mem_kv_bf, l, BF16,
                          tm=1024, name="mem_kv_proj")
        if l < n_a:
            z = norm_matmul(xs, g_mix[l], w_in_a_bf, l, BF16, tm=1024, name="in_proj_a",
                            gelu_cols=2 * g_w)
            b_full = jnp.broadcast_to(b_sp[l][:, :, None], w_sp[l].shape)
            xs = mix_out_gmlp(xs, z, g_v[l], w_sp[l], b_full, kvm, w_out_bf, l, batch)
        else:
            j = l - n_a
            z = norm_matmul(xs, g_mix[l], w_in_b_bf, j, BF16, tm=1024, name="in_proj_b")
            wq = w_uq[j].reshape(q_rank, n_heads, nope_dim + rope_dim)
            wq_rope = wq[..., nope_dim:]
            w_main = jnp.concatenate(
                [wq[..., :nope_dim], _pad_lanes(wq_rope)], axis=-1).reshape(q_rank, -1)
            w_rot = _pad_lanes(_rot_cols(wq_rope)).reshape(q_rank, -1)
            q = q_prep(z, g_q_lat[j], w_main.astype(BF16), w_rot.astype(BF16), cos_t, sin_t,
                       batch, n_heads, scale)
            w_kv_h = jnp.concatenate([w_uk[j], w_uv[j]], axis=-1)
            w_kv_h = jnp.transpose(w_kv_h, (1, 0, 2)).astype(BF16)
            main = mla_attn(q, c_kv, kr_pad, w_kv_h)
            xs = mix_out(xs, main, z, q_rank // mem_w, kvm, w_out_bf, l, batch)
        xs = conv_ffn(xs, g_ffn[l], w_up_bf,
                      _interleave_gate_value(conv_w[l], FFN_TF),
                      _interleave_gate_value(conv_b[l], FFN_TF),
                      w_down_bf, l, seq, tf=FFN_TF,
                      out_gain=g_final if l == depth - 1 else None)
    return xs.reshape(batch, seq, d_model)
```

```python
import functools

import jax
import jax.numpy as jnp
import numpy as np
from jax import lax
from jax.experimental import pallas as pl
from jax.experimental.pallas import tpu as pltpu

EPS = 1e-6
ROPE_THETA = 10000.0
MEM_HEADS = 4
CHUNK = 128
LANES = 128
SUBLANES = 8
CONV_W = 3
FFN_TF = 512
VMEM_LIMIT = 56 * 1024 * 1024
VMEM_LIMIT_HIGH = 62 * 1024 * 1024

F32 = jnp.float32
BF16 = jnp.bfloat16


def _params(semantics, vmem=VMEM_LIMIT):
    return pltpu.CompilerParams(dimension_semantics=semantics, vmem_limit_bytes=vmem)


def _rms(x, g):
    ms = jnp.mean(x * x, axis=-1, keepdims=True)
    return x * lax.rsqrt(ms + EPS) * g


def _gelu(x):
    return 0.5 * x * (1.0 + lax.erf(x * np.float32(np.sqrt(0.5))))


def _rope_table_kernel(pos_ref, inv_ref, cos_ref, sin_ref):
    ang = pos_ref[...].astype(F32) * inv_ref[...]
    cos_ref[...] = jnp.cos(ang)
    sin_ref[...] = jnp.sin(ang)


def rope_tables(pos_col, inv_row, tm=2048):
    T = pos_col.shape[0]
    return pl.pallas_call(
        _rope_table_kernel,
        out_shape=(jax.ShapeDtypeStruct((T, LANES), F32),) * 2,
        grid=(T // tm,),
        in_specs=[pl.BlockSpec((tm, 1), lambda i: (i, 0)),
                  pl.BlockSpec((1, LANES), lambda i: (0, 0))],
        out_specs=(pl.BlockSpec((tm, LANES), lambda i: (i, 0)),) * 2,
        compiler_params=_params(("arbitrary",)),
        name="rope_tables",
    )(pos_col, inv_row)


def _norm_matmul_kernel(x_ref, g_ref, w_ref, o_ref, h_ref, *, rc, gelu_cols):
    n_chunks = x_ref.shape[0] // rc

    def norm(r):
        rows = slice(r * rc, (r + 1) * rc)
        h_ref[rows, :] = _rms(x_ref[rows, :], g_ref[...]).astype(h_ref.dtype)

    norm(0)
    for r in range(n_chunks):
        if r + 1 < n_chunks:
            norm(r + 1)
        rows = slice(r * rc, (r + 1) * rc)
        acc = jnp.dot(h_ref[rows, :], w_ref[...], preferred_element_type=F32)
        if gelu_cols:
            acc = jnp.concatenate([_gelu(acc[:, :gelu_cols]), acc[:, gelu_cols:]], axis=1)
        o_ref[rows, :] = acc.astype(o_ref.dtype)


def norm_matmul(x, g, w, layer, out_dtype, tm, name, rc=256, gelu_cols=0):
    T, K = x.shape
    N = w.shape[2]
    tm = min(tm, T)
    return pl.pallas_call(
        functools.partial(_norm_matmul_kernel, rc=rc, gelu_cols=gelu_cols),
        out_shape=jax.ShapeDtypeStruct((T, N), out_dtype),
        grid=(T // tm,),
        in_specs=[pl.BlockSpec((tm, K), lambda i: (i, 0)),
                  pl.BlockSpec((1, K), lambda i: (0, 0)),
                  pl.BlockSpec((None, K, N), lambda i: (layer, 0, 0),
                               pipeline_mode=pl.Buffered(1))],
        out_specs=pl.BlockSpec((tm, N), lambda i: (i, 0)),
        scratch_shapes=[pltpu.VMEM((tm, K), BF16)],
        compiler_params=_params(("arbitrary",), vmem=VMEM_LIMIT_HIGH),
        name=name,
    )(x, g.reshape(1, K), w)


def _kv_post_kernel(kv_ref, g_ref, cos_ref, sin_ref, ckv_ref, kr_ref, *, rank):
    lat = kv_ref[:, :rank]
    ckv_ref[...] = _rms(lat, g_ref[...]).astype(ckv_ref.dtype)
    kr = kv_ref[:, rank:rank + LANES] * cos_ref[...]
    kr = kr + kv_ref[:, rank + LANES:] * sin_ref[...]
    kr_ref[...] = kr.astype(kr_ref.dtype)


def kv_post(kv_raw, g_lat, cos_t, sin_t, tm=1024):
    T, N = kv_raw.shape
    rank = N - 2 * LANES
    return pl.pallas_call(
        functools.partial(_kv_post_kernel, rank=rank),
        out_shape=(jax.ShapeDtypeStruct((T, rank), BF16),
                   jax.ShapeDtypeStruct((T, LANES), BF16)),
        grid=(T // tm,),
        in_specs=[pl.BlockSpec((tm, N), lambda i: (i, 0)),
                  pl.BlockSpec((1, rank), lambda i: (0, 0)),
                  pl.BlockSpec((tm, LANES), lambda i: (i, 0)),
                  pl.BlockSpec((tm, LANES), lambda i: (i, 0))],
        out_specs=(pl.BlockSpec((tm, rank), lambda i: (i, 0)),
                   pl.BlockSpec((tm, LANES), lambda i: (i, 0))),
        compiler_params=_params(("arbitrary",)),
        name="kv_post",
    )(kv_raw, g_lat.reshape(1, rank), cos_t, sin_t)


def _mix_out_gmlp_kernel(x_ref, u_ref, v_ref, q_ref, kv_ref, gv_ref, wsp_ref, b_ref, w_ref,
                         o_ref, mix_ref, *, n_groups, n_heads, rc):
    tm = x_ref.shape[0]
    gw = n_groups * CHUNK
    width = q_ref.shape[1]
    head_dim = width // n_heads
    scale = np.float32(head_dim ** -0.5)
    row = lax.broadcasted_iota(jnp.int32, (CHUNK, CHUNK), 0)
    col = lax.broadcasted_iota(jnp.int32, (CHUNK, CHUNK), 1)
    causal = row >= col
    w_tril = [jnp.where(causal, wsp_ref[g], 0.0).astype(BF16) for g in range(n_groups)]

    def mixer(r):
        for c in range(r * rc // CHUNK, (r + 1) * rc // CHUNK):
            rows = slice(c * CHUNK, (c + 1) * CHUNK)
            ug = u_ref[rows, :].astype(F32)
            vn = _rms(v_ref[rows, :].astype(F32), gv_ref[...]).astype(BF16)
            for g in range(n_groups):
                cols = slice(g * CHUNK, (g + 1) * CHUNK)
                sv = jnp.dot(w_tril[g], vn[:, cols], preferred_element_type=F32) + b_ref[g]
                mix_ref[rows, cols] = (ug[:, cols] * sv).astype(BF16)
        rows = slice(r * rc, (r + 1) * rc)
        for h in range(n_heads):
            cols = slice(h * head_dim, (h + 1) * head_dim)
            k = kv_ref[:, cols]
            v = kv_ref[:, width + h * head_dim:width + (h + 1) * head_dim]
            s = lax.dot_general(q_ref[rows, cols], k, (((1,), (1,)), ((), ())),
                                preferred_element_type=F32) * scale
            m = jnp.max(s, axis=-1, keepdims=True)
            p = jnp.exp(s - m)
            l = jnp.sum(p, axis=-1, keepdims=True)
            o = jnp.dot(p.astype(BF16), v, preferred_element_type=F32) / l
            mix_ref[rows, gw + h * head_dim:gw + (h + 1) * head_dim] = o.astype(BF16)

    n_chunks = tm // rc
    mixer(0)
    for r in range(n_chunks):
        if r + 1 < n_chunks:
            mixer(r + 1)
        rows = slice(r * rc, (r + 1) * rc)
        o_ref[rows, :] = x_ref[rows, :] + jnp.dot(mix_ref[rows, :], w_ref[...],
                                                  preferred_element_type=F32)


def mix_out_gmlp(x, z, g_v, w_sp, b_full, kvm, w_out, layer, batch, tm=512, rc=256):
    T, D = x.shape
    n_groups = w_sp.shape[0]
    gw = n_groups * CHUNK
    n_mem = kvm.shape[0] // batch
    width = kvm.shape[1] // 2
    per_b = T // batch // tm
    kern = functools.partial(_mix_out_gmlp_kernel, n_groups=n_groups, n_heads=MEM_HEADS, rc=rc)
    return pl.pallas_call(
        kern,
        out_shape=jax.ShapeDtypeStruct((T, D), F32),
        grid=(batch, per_b),
        in_specs=[pl.BlockSpec((tm, D), lambda b, i: (b * per_b + i, 0)),
                  pl.BlockSpec((tm, gw), lambda b, i: (b * per_b + i, 0)),
                  pl.BlockSpec((tm, gw), lambda b, i: (b * per_b + i, 1)),
                  pl.BlockSpec((tm, width), lambda b, i: (b * per_b + i, 2 * gw // width)),
                  pl.BlockSpec((n_mem, 2 * width), lambda b, i: (b, 0)),
                  pl.BlockSpec((1, gw), lambda b, i: (0, 0)),
                  pl.BlockSpec((n_groups, CHUNK, CHUNK), lambda b, i: (0, 0, 0)),
                  pl.BlockSpec((n_groups, CHUNK, CHUNK), lambda b, i: (0, 0, 0)),
                  pl.BlockSpec((None, gw + width, D), lambda b, i: (layer, 0, 0))],
        out_specs=pl.BlockSpec((tm, D), lambda b, i: (b * per_b + i, 0)),
        scratch_shapes=[pltpu.VMEM((tm, gw + width), BF16)],
        compiler_params=_params(("arbitrary", "arbitrary")),
        name="mix_out_gmlp",
    )(x, z, z, z, kvm, g_v.reshape(1, gw), w_sp, b_full, w_out)


def _mix_out_kernel(x_ref, a_ref, q_ref, kv_ref, w_ref, o_ref, *, n_heads):
    ka = a_ref.shape[1]
    width = q_ref.shape[1]
    head_dim = width // n_heads
    scale = np.float32(head_dim ** -0.5)
    acc = jnp.dot(a_ref[...], w_ref[:ka, :], preferred_element_type=F32)
    mo = []
    for h in range(n_heads):
        cols = slice(h * head_dim, (h + 1) * head_dim)
        k = kv_ref[:, cols]
        v = kv_ref[:, width + h * head_dim:width + (h + 1) * head_dim]
        s = lax.dot_general(q_ref[:, cols], k, (((1,), (1,)), ((), ())),
                            preferred_element_type=F32) * scale
        m = jnp.max(s, axis=-1, keepdims=True)
        p = jnp.exp(s - m)
        l = jnp.sum(p, axis=-1, keepdims=True)
        o = jnp.dot(p.astype(BF16), v, preferred_element_type=F32) / l
        mo.append(o.astype(BF16))
    acc = acc + jnp.dot(jnp.concatenate(mo, axis=1), w_ref[ka:, :], preferred_element_type=F32)
    o_ref[...] = x_ref[...] + acc


def mix_out(x, main, z, q_col_block, kvm, w_out, layer, batch, tm=512):
    T, D = x.shape
    ka = main.shape[1]
    n_mem = kvm.shape[0] // batch
    width = kvm.shape[1] // 2
    per_b = T // batch // tm
    return pl.pallas_call(
        functools.partial(_mix_out_kernel, n_heads=MEM_HEADS),
        out_shape=jax.ShapeDtypeStruct((T, D), F32),
        grid=(batch, per_b),
        in_specs=[pl.BlockSpec((tm, D), lambda b, i: (b * per_b + i, 0)),
                  pl.BlockSpec((tm, ka), lambda b, i: (b * per_b + i, 0)),
                  pl.BlockSpec((tm, width), lambda b, i: (b * per_b + i, q_col_block)),
                  pl.BlockSpec((n_mem, 2 * width), lambda b, i: (b, 0)),
                  pl.BlockSpec((None, ka + width, D), lambda b, i: (layer, 0, 0))],
        out_specs=pl.BlockSpec((tm, D), lambda b, i: (b * per_b + i, 0)),
        compiler_params=_params(("arbitrary", "arbitrary")),
        name="mix_out",
    )(x, main, z, kvm, w_out)


def _conv_ffn_kernel(x_ref, g_ref, wg_ref, wv_ref, cw_ref, cb_ref, wd_ref, o_ref,
                     h_ref, a_ref, carry_ref, *, tm, rc, tf, tiles_per_seq, norm_out):
    i = pl.program_id(0)
    j = pl.program_id(1)

    @pl.when(jnp.logical_and(i == 0, j == 0))
    def _():
        carry_ref[...] = jnp.zeros(carry_ref.shape, F32)

    @pl.when(j == 0)
    def _():
        for r in range(tm // rc):
            rows = slice(r * rc, (r + 1) * rc)
            x = x_ref[rows, :]
            h_ref[rows, :] = _rms(x, g_ref[0:1, :]).astype(h_ref.dtype)
            o_ref[rows, :] = x

    first_in_seq = (i % tiles_per_seq) == 0
    a_ref[:SUBLANES, :] = jnp.where(first_in_seq, 0.0, carry_ref[j])
    cw = cw_ref[...]
    cb = cb_ref[...]
    def up(r):
        lo = SUBLANES + r * rc
        h = h_ref[r * rc:(r + 1) * rc, :]
        a_ref[lo:lo + rc, :tf] = jnp.dot(h, wg_ref[...], preferred_element_type=F32)
        a_ref[lo:lo + rc, tf:] = jnp.dot(h, wv_ref[...], preferred_element_type=F32)

    def gate_down(r):
        lo = SUBLANES + r * rc
        c = cb + a_ref[lo:lo + rc, :] * cw[CONV_W - 1:CONV_W, :]
        for k in range(CONV_W - 1):
            lag = CONV_W - 1 - k
            c = c + a_ref[lo - lag:lo - lag + rc, :] * cw[k:k + 1, :]
        cg = c[:, :tf]
        gated = (cg * jax.nn.sigmoid(cg) * c[:, tf:]).astype(BF16)
        o_ref[r * rc:(r + 1) * rc, :] += jnp.dot(gated, wd_ref[...],
                                                 preferred_element_type=F32)

    n_chunks = tm // rc
    up(0)
    for r in range(n_chunks):
        if r + 1 < n_chunks:
            up(r + 1)
        gate_down(r)
    carry_ref[j] = a_ref[tm:, :]

    if norm_out:
        @pl.when(j == pl.num_programs(1) - 1)
        def _():
            for r in range(n_chunks):
                rows = slice(r * rc, (r + 1) * rc)
                o_ref[rows, :] = _rms(o_ref[rows, :], g_ref[1:2, :])


def conv_ffn(x, g, w_up, conv_w, conv_b, w_down, layer, seq, out_gain=None,
             tm=1024, tf=512, rc=512):
    T, D = x.shape
    d_ff = w_down.shape[1]
    nff = d_ff // tf
    gains = g.reshape(1, D) if out_gain is None else jnp.stack([g, out_gain])
    kern = functools.partial(_conv_ffn_kernel, tm=tm, rc=rc, tf=tf, tiles_per_seq=seq // tm,
                             norm_out=out_gain is not None)
    return pl.pallas_call(
        kern,
        out_shape=jax.ShapeDtypeStruct((T, D), F32),
        grid=(T // tm, nff),
        in_specs=[pl.BlockSpec((tm, D), lambda i, j: (i, 0)),
                  pl.BlockSpec(gains.shape, lambda i, j: (0, 0)),
                  pl.BlockSpec((None, D, tf), lambda i, j: (layer, 0, j)),
                  pl.BlockSpec((None, D, tf), lambda i, j: (layer, 0, nff + j)),
                  pl.BlockSpec((CONV_W, 2 * tf), lambda i, j: (0, j)),
                  pl.BlockSpec((1, 2 * tf), lambda i, j: (0, j)),
                  pl.BlockSpec((None, tf, D), lambda i, j: (layer, j, 0))],
        out_specs=pl.BlockSpec((tm, D), lambda i, j: (i, 0)),
        scratch_shapes=[pltpu.VMEM((tm, D), BF16),
                        pltpu.VMEM((tm + SUBLANES, 2 * tf), F32),
                        pltpu.VMEM((nff, SUBLANES, 2 * tf), F32)],
        compiler_params=_params(("arbitrary", "arbitrary"), vmem=VMEM_LIMIT_HIGH),
        name="conv_ffn",
    )(x, gains, w_up, w_up, conv_w, conv_b.reshape(1, -1), w_down)


def _interleave_gate_value(w, tf):
    lead = w.shape[:-1]
    d_ff = w.shape[-1] // 2
    w = w.reshape(*lead, 2, d_ff // tf, tf)
    return jnp.swapaxes(w, -3, -2).reshape(*lead, 2 * d_ff)


def _in_proj_q_kernel(x_ref, g_ref, w_ref, gq_ref, wm_ref, wr_ref, cos_ref, sin_ref,
                      qm_ref, o_ref, h_ref, *, n_heads, scale, rc):
    n_chunks = x_ref.shape[0] // rc
    rank = gq_ref.shape[1]

    def norm(r):
        rows = slice(r * rc, (r + 1) * rc)
        h_ref[rows, :] = _rms(x_ref[rows, :], g_ref[...]).astype(h_ref.dtype)

    norm(0)
    for r in range(n_chunks):
        if r + 1 < n_chunks:
            norm(r + 1)
        rows = slice(r * rc, (r + 1) * rc)
        z = jnp.dot(h_ref[rows, :], w_ref[...], preferred_element_type=F32)
        qm_ref[rows, :] = z[:, rank:].astype(qm_ref.dtype)
        qn = _rms(z[:, :rank], gq_ref[...]).astype(BF16)
        a = jnp.dot(qn, wm_ref[...], preferred_element_type=F32)
        rot = jnp.dot(qn, wr_ref[...], preferred_element_type=F32)
        cos = cos_ref[rows, :]
        sin = sin_ref[rows, :]
        for h in range(n_heads):
            base = 2 * LANES * h
            o_ref[0, h, rows, :LANES] = (a[:, base:base + LANES] * scale).astype(o_ref.dtype)
            rope = (a[:, base + LANES:base + 2 * LANES] * cos
                    + rot[:, h * LANES:(h + 1) * LANES] * sin)
            o_ref[0, h, rows, LANES:] = (rope * scale).astype(o_ref.dtype)


def in_proj_q(x, g, w_in, layer, g_q, w_main, w_rot, cos_t, sin_t, batch, n_heads, scale,
              tm=512, rc=256):
    T, D = x.shape
    seq = T // batch
    rank = w_main.shape[0]
    n_in = w_in.shape[2]
    per_b = seq // tm
    kern = functools.partial(_in_proj_q_kernel, n_heads=n_heads, scale=np.float32(scale), rc=rc)
    return pl.pallas_call(
        kern,
        out_shape=(jax.ShapeDtypeStruct((T, n_in - rank), BF16),
                   jax.ShapeDtypeStruct((batch, n_heads, seq, 2 * LANES), BF16)),
        grid=(batch, per_b),
        in_specs=[pl.BlockSpec((tm, D), lambda b, i: (b * per_b + i, 0)),
                  pl.BlockSpec((1, D), lambda b, i: (0, 0)),
                  pl.BlockSpec((None, D, n_in), lambda b, i: (layer, 0, 0)),
                  pl.BlockSpec((1, rank), lambda b, i: (0, 0)),
                  pl.BlockSpec(w_main.shape, lambda b, i: (0, 0)),
                  pl.BlockSpec(w_rot.shape, lambda b, i: (0, 0)),
                  pl.BlockSpec((tm, LANES), lambda b, i: (b * per_b + i, 0)),
                  pl.BlockSpec((tm, LANES), lambda b, i: (b * per_b + i, 0))],
        out_specs=(pl.BlockSpec((tm, n_in - rank), lambda b, i: (b * per_b + i, 0)),
                   pl.BlockSpec((1, n_heads, tm, 2 * LANES), lambda b, i: (b, 0, i, 0))),
        scratch_shapes=[pltpu.VMEM((tm, D), BF16)],
        compiler_params=_params(("arbitrary", "arbitrary")),
        name="in_proj_q",
    )(x, g.reshape(1, D), w_in, g_q.reshape(1, rank), w_main, w_rot, cos_t, sin_t)


def _mla_attn_kernel(q_ref, ckv_ref, kr_ref, wkv_ref, o_ref, k_ref, vt_ref, acc_ref,
                     sa_ref, sb_ref, ma_ref, mb_ref, *, tq, tk, hp, v_dim):
    i = pl.program_id(2)
    n_kb = k_ref.shape[1]

    @pl.when(i == 0)
    def _():
        ones = jnp.ones((vt_ref.shape[2] - v_dim, tk), vt_ref.dtype)
        for jb in range(n_kb):
            rows = slice(jb * tk, (jb + 1) * tk)
            for hh in range(hp):
                kv = jnp.dot(ckv_ref[rows, :], wkv_ref[hh], preferred_element_type=F32)
                k_ref[hh, jb, :, :LANES] = kv[:, :LANES].astype(k_ref.dtype)
                k_ref[hh, jb, :, LANES:] = kr_ref[rows, :]
                vt_ref[hh, jb, :v_dim, :] = kv[:, LANES:].T.astype(vt_ref.dtype)
                vt_ref[hh, jb, v_dim:, :] = ones

    acc_ref[...] = jnp.zeros(acc_ref.shape, F32)

    def scores(jb, dst_ref, mdst_ref, q_lo=0):
        for hh in range(hp):
            st = lax.dot_general(k_ref[hh, jb], q_ref[0, hh, q_lo:, :], (((1,), (1,)), ((), ())),
                                 preferred_element_type=F32)
            dst_ref[hh, :, q_lo:] = st
            mdst_ref[hh, :, q_lo:] = jnp.max(st, axis=0, keepdims=True)

    def consume(jb, src_ref, msrc_ref, m_all, diag=False, q_lo=0):
        out = []
        for hh in range(hp):
            m = m_all[hh][:, q_lo:]
            st = src_ref[hh, :, q_lo:]
            if diag:
                k_pos = lax.broadcasted_iota(jnp.int32, st.shape, 0)
                q_pos = lax.broadcasted_iota(jnp.int32, st.shape, 1)
                st = jnp.where(k_pos <= q_pos, st, -jnp.inf)
                m_blk = jnp.max(st, axis=0, keepdims=True)
            else:
                m_blk = msrc_ref[hh, :, q_lo:]
            m_new = jnp.maximum(m, m_blk)
            alpha = jnp.exp2(m - m_new)
            p = jnp.exp2(st - m_new).astype(BF16)
            pv = jnp.dot(vt_ref[hh, jb], p, preferred_element_type=F32)
            acc_ref[hh, :, q_lo:] = alpha * acc_ref[hh, :, q_lo:] + pv
            if q_lo:
                m_new = jnp.concatenate([m_all[hh][:, :q_lo], m_new], axis=1)
            out.append(m_new)
        return tuple(out)

    def pair(t, m_all):
        scores(2 * t + 1, sb_ref, mb_ref)
        m_all = consume(2 * t, sa_ref, ma_ref, m_all)
        scores(2 * t + 2, sa_ref, ma_ref)
        return consume(2 * t + 1, sb_ref, mb_ref, m_all)

    scores(0, sa_ref, ma_ref)
    m_all = lax.fori_loop(0, i, pair, tuple(jnp.full((1, tq), -jnp.inf, F32) for _ in range(hp)))
    scores(2 * i + 1, sb_ref, mb_ref, q_lo=tk)
    m_all = consume(2 * i, sa_ref, ma_ref, m_all, diag=True)
    consume(2 * i + 1, sb_ref, mb_ref, m_all, diag=True, q_lo=tk)
    for hh in range(hp):
        o = acc_ref[hh, :v_dim, :] / acc_ref[hh, v_dim:v_dim + 1, :]
        o_ref[:, hh * v_dim:(hh + 1) * v_dim] = o.T.astype(o_ref.dtype)


def mla_attn(q, c_kv, kr_pad, w_kv, tq=1024, hp=4):
    tk = tq // 2
    batch, n_heads, seq, qk_dim = q.shape
    rank = c_kv.shape[1]
    v_dim = w_kv.shape[2] - LANES
    v_rows = v_dim + 2 * SUBLANES
    nq = seq // tq
    return pl.pallas_call(
        functools.partial(_mla_attn_kernel, tq=tq, tk=tk, hp=hp, v_dim=v_dim),
        out_shape=jax.ShapeDtypeStruct((batch * seq, n_heads * v_dim), BF16),
        grid=(batch, n_heads // hp, nq),
        in_specs=[pl.BlockSpec((1, hp, tq, qk_dim), lambda b, h, i: (b, h, i, 0)),
                  pl.BlockSpec((seq, rank), lambda b, h, i: (b, 0)),
                  pl.BlockSpec((seq, LANES), lambda b, h, i: (b, 0)),
                  pl.BlockSpec((hp, rank, LANES + v_dim), lambda b, h, i: (h, 0, 0))],
        out_specs=pl.BlockSpec((tq, hp * v_dim), lambda b, h, i: (b * nq + i, h)),
        scratch_shapes=[pltpu.VMEM((hp, seq // tk, tk, qk_dim), BF16),
                        pltpu.VMEM((hp, seq // tk, v_rows, tk), BF16),
                        pltpu.VMEM((hp, v_rows, tq), F32),
                        pltpu.VMEM((hp, tk, tq), F32),
                        pltpu.VMEM((hp, tk, tq), F32),
                        pltpu.VMEM((hp, 1, tq), F32),
                        pltpu.VMEM((hp, 1, tq), F32)],
        compiler_params=_params(("arbitrary", "arbitrary", "arbitrary"), vmem=VMEM_LIMIT_HIGH),
        name="mla_attn",
    )(q, c_kv, kr_pad, w_kv)


def _rot_cols(w):
    half = w.shape[-1] // 2
    return jnp.concatenate([-w[..., half:], w[..., :half]], axis=-1)


def _pad_lanes(w):
    pad = [(0, 0)] * (w.ndim - 1) + [(0, LANES - w.shape[-1])]
    return jnp.pad(w, pad)


def kernel(x, mem, positions, g_mix, g_ffn, g_final, w_in_a, g_v, w_sp, b_sp, g_kv, w_kv_a,
           g_kv_lat, w_in_b, g_q_lat, w_uq, w_uk, w_uv, g_mem, w_mem_kv, w_out, w_ffn_up,
           conv_w, conv_b, w_ffn_down):
    batch, seq, d_model = x.shape
    depth = g_mix.shape[0]
    n_a = w_in_a.shape[0]
    n_mem = mem.shape[1]
    T = batch * seq
    kv_rank = g_kv_lat.shape[0]
    q_rank = g_q_lat.shape[1]
    n_heads, nope_dim = w_uk.shape[2], w_uk.shape[3]
    rope_dim = w_kv_a.shape[1] - kv_rank
    mem_w = w_mem_kv.shape[2] // 2
    g_w = g_v.shape[1]
    scale = (nope_dim + rope_dim) ** -0.5 * np.log2(np.e)

    xs = x.reshape(T, d_model)
    mems = mem.reshape(batch * n_mem, d_model)

    inv = 1.0 / (ROPE_THETA ** (jnp.arange(0, rope_dim, 2, dtype=F32) / rope_dim))
    inv_row = _pad_lanes(jnp.concatenate([inv, inv])).reshape(1, LANES)
    cos_t, sin_t = rope_tables(positions.reshape(T, 1), inv_row)

    w_in_a_bf, w_in_b_bf, w_mem_kv_bf = (w.astype(BF16) for w in (w_in_a, w_in_b, w_mem_kv))
    w_out_bf, w_up_bf, w_down_bf = (w.astype(BF16) for w in (w_out, w_ffn_up, w_ffn_down))

    c_kv = kr_pad = None
    for l in range(depth):
        if l == n_a:
            w_kr = w_kv_a[:, kv_rank:]
            w_kv_cat = jnp.concatenate(
                [w_kv_a[:, :kv_rank], _pad_lanes(w_kr), _pad_lanes(_rot_cols(w_kr))], axis=1)
            kv_raw = norm_matmul(xs, g_kv, w_kv_cat.astype(BF16)[None], 0, F32,
                                 tm=1024, name="kv_proj")
            c_kv, kr_pad = kv_post(kv_raw, g_kv_lat, cos_t, sin_t)

        kvm = norm_matmul(mems, g_mem[l], w_mem_kv_bf, l, BF16,
                          tm=1024, name="mem_kv_proj")
        if l < n_a:
            z = norm_matmul(xs, g_mix[l], w_in_a_bf, l, BF16, tm=1024, name="in_proj_a",
                            gelu_cols=2 * g_w)
            b_full = jnp.broadcast_to(b_sp[l][:, :, None], w_sp[l].shape)
            xs = mix_out_gmlp(xs, z, g_v[l], w_sp[l], b_full, kvm, w_out_bf, l, batch)
        else:
            j = l - n_a
            wq = w_uq[j].reshape(q_rank, n_heads, nope_dim + rope_dim)
            wq_rope = wq[..., nope_dim:]
            w_main = jnp.concatenate(
                [wq[..., :nope_dim], _pad_lanes(wq_rope)], axis=-1).reshape(q_rank, -1)
            w_rot = _pad_lanes(_rot_cols(wq_rope)).reshape(q_rank, -1)
            q_mem, q = in_proj_q(xs, g_mix[l], w_in_b_bf, j, g_q_lat[j], w_main.astype(BF16),
                                 w_rot.astype(BF16), cos_t, sin_t, batch, n_heads, scale)
            w_kv_h = jnp.concatenate([w_uk[j], w_uv[j]], axis=-1)
            w_kv_h = jnp.transpose(w_kv_h, (1, 0, 2)).astype(BF16)
            main = mla_attn(q, c_kv, kr_pad, w_kv_h)
            xs = mix_out(xs, main, q_mem, 0, kvm, w_out_bf, l, batch)
        xs = conv_ffn(xs, g_ffn[l], w_up_bf,
                      _interleave_gate_value(conv_w[l], FFN_TF),
                      _interleave_gate_value(conv_b[l], FFN_TF),
                      w_down_bf, l, seq, tf=FFN_TF,
                      out_gain=g_final if l == depth - 1 else None)
    return xs.reshape(batch, seq, d_model)
```

```python
import functools

import jax
import jax.numpy as jnp
import numpy as np
from jax import lax
from jax.experimental import pallas as pl
from jax.experimental.pallas import tpu as pltpu

EPS = 1e-6
ROPE_THETA = 10000.0
MEM_HEADS = 4
CHUNK = 128
LANES = 128
SUBLANES = 8
CONV_W = 3
FFN_TF = 512
VMEM_LIMIT = 56 * 1024 * 1024
VMEM_LIMIT_HIGH = 62 * 1024 * 1024

F32 = jnp.float32
BF16 = jnp.bfloat16


def _params(semantics, vmem=VMEM_LIMIT):
    return pltpu.CompilerParams(dimension_semantics=semantics, vmem_limit_bytes=vmem)


def _rms(x, g):
    ms = jnp.mean(x * x, axis=-1, keepdims=True)
    return x * lax.rsqrt(ms + EPS) * g


def _gelu(x):
    return 0.5 * x * (1.0 + lax.erf(x * np.float32(np.sqrt(0.5))))


def _rope_table_kernel(pos_ref, inv_ref, cos_ref, sin_ref):
    ang = pos_ref[...].astype(F32) * inv_ref[...]
    cos_ref[...] = jnp.cos(ang)
    sin_ref[...] = jnp.sin(ang)


def rope_tables(pos_col, inv_row, tm=2048):
    T = pos_col.shape[0]
    return pl.pallas_call(
        _rope_table_kernel,
        out_shape=(jax.ShapeDtypeStruct((T, LANES), F32),) * 2,
        grid=(T // tm,),
        in_specs=[pl.BlockSpec((tm, 1), lambda i: (i, 0)),
                  pl.BlockSpec((1, LANES), lambda i: (0, 0))],
        out_specs=(pl.BlockSpec((tm, LANES), lambda i: (i, 0)),) * 2,
        compiler_params=_params(("arbitrary",)),
        name="rope_tables",
    )(pos_col, inv_row)


def _norm_matmul_kernel(x_ref, g_ref, w_ref, o_ref, h_ref, *, rc, gelu_cols):
    n_chunks = x_ref.shape[0] // rc

    def norm(r):
        rows = slice(r * rc, (r + 1) * rc)
        h_ref[rows, :] = _rms(x_ref[rows, :], g_ref[...]).astype(h_ref.dtype)

    norm(0)
    for r in range(n_chunks):
        if r + 1 < n_chunks:
            norm(r + 1)
        rows = slice(r * rc, (r + 1) * rc)
        acc = jnp.dot(h_ref[rows, :], w_ref[...], preferred_element_type=F32)
        if gelu_cols:
            acc = jnp.concatenate([_gelu(acc[:, :gelu_cols]), acc[:, gelu_cols:]], axis=1)
        o_ref[rows, :] = acc.astype(o_ref.dtype)


def norm_matmul(x, g, w, layer, out_dtype, tm, name, rc=256, gelu_cols=0):
    T, K = x.shape
    N = w.shape[2]
    tm = min(tm, T)
    return pl.pallas_call(
        functools.partial(_norm_matmul_kernel, rc=rc, gelu_cols=gelu_cols),
        out_shape=jax.ShapeDtypeStruct((T, N), out_dtype),
        grid=(T // tm,),
        in_specs=[pl.BlockSpec((tm, K), lambda i: (i, 0)),
                  pl.BlockSpec((1, K), lambda i: (0, 0)),
                  pl.BlockSpec((None, K, N), lambda i: (layer, 0, 0),
                               pipeline_mode=pl.Buffered(1))],
        out_specs=pl.BlockSpec((tm, N), lambda i: (i, 0)),
        scratch_shapes=[pltpu.VMEM((tm, K), BF16)],
        compiler_params=_params(("arbitrary",), vmem=VMEM_LIMIT_HIGH),
        name=name,
    )(x, g.reshape(1, K), w)


def _kv_side_kernel(x_ref, g_ref, w_ref, glat_ref, cos_ref, sin_ref, ckv_ref, kr_ref, h_ref,
                    *, rc):
    n_chunks = x_ref.shape[0] // rc
    rank = glat_ref.shape[1]

    def norm(r):
        rows = slice(r * rc, (r + 1) * rc)
        h_ref[rows, :] = _rms(x_ref[rows, :], g_ref[...]).astype(h_ref.dtype)

    norm(0)
    for r in range(n_chunks):
        if r + 1 < n_chunks:
            norm(r + 1)
        rows = slice(r * rc, (r + 1) * rc)
        kv = jnp.dot(h_ref[rows, :], w_ref[...], preferred_element_type=F32)
        ckv_ref[rows, :] = _rms(kv[:, :rank], glat_ref[...]).astype(ckv_ref.dtype)
        kr = kv[:, rank:rank + LANES] * cos_ref[rows, :] + kv[:, rank + LANES:] * sin_ref[rows, :]
        kr_ref[rows, :] = kr.astype(kr_ref.dtype)


def kv_side(x, g, w_kv, g_lat, cos_t, sin_t, tm=1024, rc=256):
    T, D = x.shape
    N = w_kv.shape[1]
    rank = N - 2 * LANES
    return pl.pallas_call(
        functools.partial(_kv_side_kernel, rc=rc),
        out_shape=(jax.ShapeDtypeStruct((T, rank), BF16),
                   jax.ShapeDtypeStruct((T, LANES), BF16)),
        grid=(T // tm,),
        in_specs=[pl.BlockSpec((tm, D), lambda i: (i, 0)),
                  pl.BlockSpec((1, D), lambda i: (0, 0)),
                  pl.BlockSpec((D, N), lambda i: (0, 0)),
                  pl.BlockSpec((1, rank), lambda i: (0, 0)),
                  pl.BlockSpec((tm, LANES), lambda i: (i, 0)),
                  pl.BlockSpec((tm, LANES), lambda i: (i, 0))],
        out_specs=(pl.BlockSpec((tm, rank), lambda i: (i, 0)),
                   pl.BlockSpec((tm, LANES), lambda i: (i, 0))),
        scratch_shapes=[pltpu.VMEM((tm, D), BF16)],
        compiler_params=_params(("arbitrary",)),
        name="kv_side",
    )(x, g.reshape(1, D), w_kv, g_lat.reshape(1, rank), cos_t, sin_t)


def _mix_out_gmlp_kernel(x_ref, u_ref, v_ref, q_ref, kv_ref, gv_ref, wsp_ref, b_ref, w_ref,
                         o_ref, mix_ref, *, n_groups, n_heads, rc):
    tm = x_ref.shape[0]
    gw = n_groups * CHUNK
    width = q_ref.shape[1]
    head_dim = width // n_heads
    scale = np.float32(head_dim ** -0.5)
    row = lax.broadcasted_iota(jnp.int32, (CHUNK, CHUNK), 0)
    col = lax.broadcasted_iota(jnp.int32, (CHUNK, CHUNK), 1)
    causal = row >= col
    w_tril = [jnp.where(causal, wsp_ref[g], 0.0).astype(BF16) for g in range(n_groups)]

    def mixer(r):
        for c in range(r * rc // CHUNK, (r + 1) * rc // CHUNK):
            rows = slice(c * CHUNK, (c + 1) * CHUNK)
            ug = u_ref[rows, :].astype(F32)
            vn = _rms(v_ref[rows, :].astype(F32), gv_ref[...]).astype(BF16)
            for g in range(n_groups):
                cols = slice(g * CHUNK, (g + 1) * CHUNK)
                sv = jnp.dot(w_tril[g], vn[:, cols], preferred_element_type=F32) + b_ref[g]
                mix_ref[rows, cols] = (ug[:, cols] * sv).astype(BF16)
        rows = slice(r * rc, (r + 1) * rc)
        for h in range(n_heads):
            cols = slice(h * head_dim, (h + 1) * head_dim)
            k = kv_ref[:, cols]
            v = kv_ref[:, width + h * head_dim:width + (h + 1) * head_dim]
            s = lax.dot_general(q_ref[rows, cols], k, (((1,), (1,)), ((), ())),
                                preferred_element_type=F32) * scale
            m = jnp.max(s, axis=-1, keepdims=True)
            p = jnp.exp(s - m)
            l = jnp.sum(p, axis=-1, keepdims=True)
            o = jnp.dot(p.astype(BF16), v, preferred_element_type=F32) / l
            mix_ref[rows, gw + h * head_dim:gw + (h + 1) * head_dim] = o.astype(BF16)

    n_chunks = tm // rc
    mixer(0)
    for r in range(n_chunks):
        if r + 1 < n_chunks:
            mixer(r + 1)
        rows = slice(r * rc, (r + 1) * rc)
        o_ref[rows, :] = x_ref[rows, :] + jnp.dot(mix_ref[rows, :], w_ref[...],
                                                  preferred_element_type=F32)


def mix_out_gmlp(x, z, g_v, w_sp, b_full, kvm, w_out, layer, batch, tm=512, rc=256):
    T, D = x.shape
    n_groups = w_sp.shape[0]
    gw = n_groups * CHUNK
    n_mem = kvm.shape[0] // batch
    width = kvm.shape[1] // 2
    per_b = T // batch // tm
    kern = functools.partial(_mix_out_gmlp_kernel, n_groups=n_groups, n_heads=MEM_HEADS, rc=rc)
    return pl.pallas_call(
        kern,
        out_shape=jax.ShapeDtypeStruct((T, D), F32),
        grid=(batch, per_b),
        in_specs=[pl.BlockSpec((tm, D), lambda b, i: (b * per_b + i, 0)),
                  pl.BlockSpec((tm, gw), lambda b, i: (b * per_b + i, 0)),
                  pl.BlockSpec((tm, gw), lambda b, i: (b * per_b + i, 1)),
                  pl.BlockSpec((tm, width), lambda b, i: (b * per_b + i, 2 * gw // width)),
                  pl.BlockSpec((n_mem, 2 * width), lambda b, i: (b, 0)),
                  pl.BlockSpec((1, gw), lambda b, i: (0, 0)),
                  pl.BlockSpec((n_groups, CHUNK, CHUNK), lambda b, i: (0, 0, 0)),
                  pl.BlockSpec((n_groups, CHUNK, CHUNK), lambda b, i: (0, 0, 0)),
                  pl.BlockSpec((None, gw + width, D), lambda b, i: (layer, 0, 0))],
        out_specs=pl.BlockSpec((tm, D), lambda b, i: (b * per_b + i, 0)),
        scratch_shapes=[pltpu.VMEM((tm, gw + width), BF16)],
        compiler_params=_params(("arbitrary", "arbitrary")),
        name="mix_out_gmlp",
    )(x, z, z, z, kvm, g_v.reshape(1, gw), w_sp, b_full, w_out)


def _mix_out_kernel(x_ref, a_ref, q_ref, kv_ref, w_ref, o_ref, *, n_heads):
    ka = a_ref.shape[1]
    width = q_ref.shape[1]
    head_dim = width // n_heads
    scale = np.float32(head_dim ** -0.5)
    acc = jnp.dot(a_ref[...], w_ref[:ka, :], preferred_element_type=F32)
    mo = []
    for h in range(n_heads):
        cols = slice(h * head_dim, (h + 1) * head_dim)
        k = kv_ref[:, cols]
        v = kv_ref[:, width + h * head_dim:width + (h + 1) * head_dim]
        s = lax.dot_general(q_ref[:, cols], k, (((1,), (1,)), ((), ())),
                            preferred_element_type=F32) * scale
        m = jnp.max(s, axis=-1, keepdims=True)
        p = jnp.exp(s - m)
        l = jnp.sum(p, axis=-1, keepdims=True)
        o = jnp.dot(p.astype(BF16), v, preferred_element_type=F32) / l
        mo.append(o.astype(BF16))
    acc = acc + jnp.dot(jnp.concatenate(mo, axis=1), w_ref[ka:, :], preferred_element_type=F32)
    o_ref[...] = x_ref[...] + acc


def mix_out(x, main, z, q_col_block, kvm, w_out, layer, batch, tm=512):
    T, D = x.shape
    ka = main.shape[1]
    n_mem = kvm.shape[0] // batch
    width = kvm.shape[1] // 2
    per_b = T // batch // tm
    return pl.pallas_call(
        functools.partial(_mix_out_kernel, n_heads=MEM_HEADS),
        out_shape=jax.ShapeDtypeStruct((T, D), F32),
        grid=(batch, per_b),
        in_specs=[pl.BlockSpec((tm, D), lambda b, i: (b * per_b + i, 0)),
                  pl.BlockSpec((tm, ka), lambda b, i: (b * per_b + i, 0)),
                  pl.BlockSpec((tm, width), lambda b, i: (b * per_b + i, q_col_block)),
                  pl.BlockSpec((n_mem, 2 * width), lambda b, i: (b, 0)),
                  pl.BlockSpec((None, ka + width, D), lambda b, i: (layer, 0, 0))],
        out_specs=pl.BlockSpec((tm, D), lambda b, i: (b * per_b + i, 0)),
        compiler_params=_params(("arbitrary", "arbitrary")),
        name="mix_out",
    )(x, main, z, kvm, w_out)


def _conv_ffn_kernel(x_ref, g_ref, wg_ref, wv_ref, cw_ref, cb_ref, wd_ref, o_ref,
                     h_ref, a_ref, carry_ref, *, tm, rc, tf, tiles_per_seq, norm_out):
    i = pl.program_id(0)
    j = pl.program_id(1)

    @pl.when(jnp.logical_and(i == 0, j == 0))
    def _():
        carry_ref[...] = jnp.zeros(carry_ref.shape, F32)

    @pl.when(j == 0)
    def _():
        for r in range(tm // rc):
            rows = slice(r * rc, (r + 1) * rc)
            x = x_ref[rows, :]
            h_ref[rows, :] = _rms(x, g_ref[0:1, :]).astype(h_ref.dtype)
            o_ref[rows, :] = x

    first_in_seq = (i % tiles_per_seq) == 0
    a_ref[:SUBLANES, :] = jnp.where(first_in_seq, 0.0, carry_ref[j])
    cw = cw_ref[...]
    cb = cb_ref[...]
    def up(r):
        lo = SUBLANES + r * rc
        h = h_ref[r * rc:(r + 1) * rc, :]
        a_ref[lo:lo + rc, :tf] = jnp.dot(h, wg_ref[...], preferred_element_type=F32)
        a_ref[lo:lo + rc, tf:] = jnp.dot(h, wv_ref[...], preferred_element_type=F32)

    def gate_down(r):
        lo = SUBLANES + r * rc
        c = cb + a_ref[lo:lo + rc, :] * cw[CONV_W - 1:CONV_W, :]
        for k in range(CONV_W - 1):
            lag = CONV_W - 1 - k
            c = c + a_ref[lo - lag:lo - lag + rc, :] * cw[k:k + 1, :]
        cg = c[:, :tf]
        gated = (cg * jax.nn.sigmoid(cg) * c[:, tf:]).astype(BF16)
        o_ref[r * rc:(r + 1) * rc, :] += jnp.dot(gated, wd_ref[...],
                                                 preferred_element_type=F32)

    n_chunks = tm // rc
    up(0)
    for r in range(n_chunks):
        if r + 1 < n_chunks:
            up(r + 1)
        gate_down(r)
    carry_ref[j] = a_ref[tm:, :]

    if norm_out:
        @pl.when(j == pl.num_programs(1) - 1)
        def _():
            for r in range(n_chunks):
                rows = slice(r * rc, (r + 1) * rc)
                o_ref[rows, :] = _rms(o_ref[rows, :], g_ref[1:2, :])


def conv_ffn(x, g, w_up, conv_w, conv_b, w_down, layer, seq, out_gain=None,
             tm=1024, tf=512, rc=512):
    T, D = x.shape
    d_ff = w_down.shape[1]
    nff = d_ff // tf
    gains = g.reshape(1, D) if out_gain is None else jnp.stack([g, out_gain])
    kern = functools.partial(_conv_ffn_kernel, tm=tm, rc=rc, tf=tf, tiles_per_seq=seq // tm,
                             norm_out=out_gain is not None)
    return pl.pallas_call(
        kern,
        out_shape=jax.ShapeDtypeStruct((T, D), F32),
        grid=(T // tm, nff),
        in_specs=[pl.BlockSpec((tm, D), lambda i, j: (i, 0)),
                  pl.BlockSpec(gains.shape, lambda i, j: (0, 0)),
                  pl.BlockSpec((None, D, tf), lambda i, j: (layer, 0, j)),
                  pl.BlockSpec((None, D, tf), lambda i, j: (layer, 0, nff + j)),
                  pl.BlockSpec((CONV_W, 2 * tf), lambda i, j: (0, j)),
                  pl.BlockSpec((1, 2 * tf), lambda i, j: (0, j)),
                  pl.BlockSpec((None, tf, D), lambda i, j: (layer, j, 0))],
        out_specs=pl.BlockSpec((tm, D), lambda i, j: (i, 0)),
        scratch_shapes=[pltpu.VMEM((tm, D), BF16),
                        pltpu.VMEM((tm + SUBLANES, 2 * tf), F32),
                        pltpu.VMEM((nff, SUBLANES, 2 * tf), F32)],
        compiler_params=_params(("arbitrary", "arbitrary"), vmem=VMEM_LIMIT_HIGH),
        name="conv_ffn",
    )(x, gains, w_up, w_up, conv_w, conv_b.reshape(1, -1), w_down)


def _interleave_gate_value(w, tf):
    lead = w.shape[:-1]
    d_ff = w.shape[-1] // 2
    w = w.reshape(*lead, 2, d_ff // tf, tf)
    return jnp.swapaxes(w, -3, -2).reshape(*lead, 2 * d_ff)


def _in_proj_q_kernel(x_ref, g_ref, w_ref, gq_ref, wm_ref, wr_ref, cos_ref, sin_ref,
                      qm_ref, o_ref, h_ref, *, n_heads, scale, rc):
    n_chunks = x_ref.shape[0] // rc
    rank = gq_ref.shape[1]

    def norm(r):
        rows = slice(r * rc, (r + 1) * rc)
        h_ref[rows, :] = _rms(x_ref[rows, :], g_ref[...]).astype(h_ref.dtype)

    norm(0)
    for r in range(n_chunks):
        if r + 1 < n_chunks:
            norm(r + 1)
        rows = slice(r * rc, (r + 1) * rc)
        z = jnp.dot(h_ref[rows, :], w_ref[...], preferred_element_type=F32)
        qm_ref[rows, :] = z[:, rank:].astype(qm_ref.dtype)
        qn = _rms(z[:, :rank], gq_ref[...]).astype(BF16)
        a = jnp.dot(qn, wm_ref[...], preferred_element_type=F32)
        rot = jnp.dot(qn, wr_ref[...], preferred_element_type=F32)
        cos = cos_ref[rows, :]
        sin = sin_ref[rows, :]
        for h in range(n_heads):
            base = 2 * LANES * h
            o_ref[0, h, rows, :LANES] = (a[:, base:base + LANES] * scale).astype(o_ref.dtype)
            rope = (a[:, base + LANES:base + 2 * LANES] * cos
                    + rot[:, h * LANES:(h + 1) * LANES] * sin)
            o_ref[0, h, rows, LANES:] = (rope * scale).astype(o_ref.dtype)


def in_proj_q(x, g, w_in, layer, g_q, w_main, w_rot, cos_t, sin_t, batch, n_heads, scale,
              tm=512, rc=256):
    T, D = x.shape
    seq = T // batch
    rank = w_main.shape[0]
    n_in = w_in.shape[2]
    per_b = seq // tm
    kern = functools.partial(_in_proj_q_kernel, n_heads=n_heads, scale=np.float32(scale), rc=rc)
    return pl.pallas_call(
        kern,
        out_shape=(jax.ShapeDtypeStruct((T, n_in - rank), BF16),
                   jax.ShapeDtypeStruct((batch, n_heads, seq, 2 * LANES), BF16)),
        grid=(batch, per_b),
        in_specs=[pl.BlockSpec((tm, D), lambda b, i: (b * per_b + i, 0)),
                  pl.BlockSpec((1, D), lambda b, i: (0, 0)),
                  pl.BlockSpec((None, D, n_in), lambda b, i: (layer, 0, 0)),
                  pl.BlockSpec((1, rank), lambda b, i: (0, 0)),
                  pl.BlockSpec(w_main.shape, lambda b, i: (0, 0)),
                  pl.BlockSpec(w_rot.shape, lambda b, i: (0, 0)),
                  pl.BlockSpec((tm, LANES), lambda b, i: (b * per_b + i, 0)),
                  pl.BlockSpec((tm, LANES), lambda b, i: (b * per_b + i, 0))],
        out_specs=(pl.BlockSpec((tm, n_in - rank), lambda b, i: (b * per_b + i, 0)),
                   pl.BlockSpec((1, n_heads, tm, 2 * LANES), lambda b, i: (b, 0, i, 0))),
        scratch_shapes=[pltpu.VMEM((tm, D), BF16)],
        compiler_params=_params(("arbitrary", "arbitrary")),
        name="in_proj_q",
    )(x, g.reshape(1, D), w_in, g_q.reshape(1, rank), w_main, w_rot, cos_t, sin_t)


def _mla_attn_kernel(q_ref, ckv_ref, kr_ref, wkv_ref, o_ref, k_ref, vt_ref, acc_ref,
                     sa_ref, sb_ref, ma_ref, mb_ref, *, tq, tk, hp, v_dim):
    i = pl.program_id(2)
    n_kb = k_ref.shape[1]

    @pl.when(i == 0)
    def _():
        ones = jnp.ones((vt_ref.shape[2] - v_dim, tk), vt_ref.dtype)
        for jb in range(n_kb):
            rows = slice(jb * tk, (jb + 1) * tk)
            for hh in range(hp):
                kv = jnp.dot(ckv_ref[rows, :], wkv_ref[hh], preferred_element_type=F32)
                k_ref[hh, jb, :, :LANES] = kv[:, :LANES].astype(k_ref.dtype)
                k_ref[hh, jb, :, LANES:] = kr_ref[rows, :]
                vt_ref[hh, jb, :v_dim, :] = kv[:, LANES:].T.astype(vt_ref.dtype)
                vt_ref[hh, jb, v_dim:, :] = ones

    acc_ref[...] = jnp.zeros(acc_ref.shape, F32)

    def scores(jb, dst_ref, mdst_ref, q_lo=0):
        for hh in range(hp):
            st = lax.dot_general(k_ref[hh, jb], q_ref[0, hh, q_lo:, :], (((1,), (1,)), ((), ())),
                                 preferred_element_type=F32)
            dst_ref[hh, :, q_lo:] = st
            mdst_ref[hh, :, q_lo:] = jnp.max(st, axis=0, keepdims=True)

    def consume(jb, src_ref, msrc_ref, m_all, diag=False, q_lo=0):
        out = []
        for hh in range(hp):
            m = m_all[hh][:, q_lo:]
            st = src_ref[hh, :, q_lo:]
            if diag:
                k_pos = lax.broadcasted_iota(jnp.int32, st.shape, 0)
                q_pos = lax.broadcasted_iota(jnp.int32, st.shape, 1)
                st = jnp.where(k_pos <= q_pos, st, -jnp.inf)
                m_blk = jnp.max(st, axis=0, keepdims=True)
            else:
                m_blk = msrc_ref[hh, :, q_lo:]
            m_new = jnp.maximum(m, m_blk)
            alpha = jnp.exp2(m - m_new)
            p = jnp.exp2(st - m_new).astype(BF16)
            pv = jnp.dot(vt_ref[hh, jb], p, preferred_element_type=F32)
            acc_ref[hh, :, q_lo:] = alpha * acc_ref[hh, :, q_lo:] + pv
            if q_lo:
                m_new = jnp.concatenate([m_all[hh][:, :q_lo], m_new], axis=1)
            out.append(m_new)
        return tuple(out)

    def pair(t, m_all):
        scores(2 * t + 1, sb_ref, mb_ref)
        m_all = consume(2 * t, sa_ref, ma_ref, m_all)
        scores(2 * t + 2, sa_ref, ma_ref)
        return consume(2 * t + 1, sb_ref, mb_ref, m_all)

    scores(0, sa_ref, ma_ref)
    m_all = lax.fori_loop(0, i, pair, tuple(jnp.full((1, tq), -jnp.inf, F32) for _ in range(hp)))
    scores(2 * i + 1, sb_ref, mb_ref, q_lo=tk)
    m_all = consume(2 * i, sa_ref, ma_ref, m_all, diag=True)
    consume(2 * i + 1, sb_ref, mb_ref, m_all, diag=True, q_lo=tk)
    for hh in range(hp):
        o = acc_ref[hh, :v_dim, :] / acc_ref[hh, v_dim:v_dim + 1, :]
        o_ref[:, hh * v_dim:(hh + 1) * v_dim] = o.T.astype(o_ref.dtype)


def mla_attn(q, c_kv, kr_pad, w_kv, tq=1024, hp=4):
    tk = tq // 2
    batch, n_heads, seq, qk_dim = q.shape
    rank = c_kv.shape[1]
    v_dim = w_kv.shape[2] - LANES
    v_rows = v_dim + 2 * SUBLANES
    nq = seq // tq
    return pl.pallas_call(
        functools.partial(_mla_attn_kernel, tq=tq, tk=tk, hp=hp, v_dim=v_dim),
        out_shape=jax.ShapeDtypeStruct((batch * seq, n_heads * v_dim), BF16),
        grid=(batch, n_heads // hp, nq),
        in_specs=[pl.BlockSpec((1, hp, tq, qk_dim), lambda b, h, i: (b, h, i, 0)),
                  pl.BlockSpec((seq, rank), lambda b, h, i: (b, 0)),
                  pl.BlockSpec((seq, LANES), lambda b, h, i: (b, 0)),
                  pl.BlockSpec((hp, rank, LANES + v_dim), lambda b, h, i: (h, 0, 0))],
        out_specs=pl.BlockSpec((tq, hp * v_dim), lambda b, h, i: (b * nq + i, h)),
        scratch_shapes=[pltpu.VMEM((hp, seq // tk, tk, qk_dim), BF16),
                        pltpu.VMEM((hp, seq // tk, v_rows, tk), BF16),
                        pltpu.VMEM((hp, v_rows, tq), F32),
                        pltpu.VMEM((hp, tk, tq), F32),
                        pltpu.VMEM((hp, tk, tq), F32),
                        pltpu.VMEM((hp, 1, tq), F32),
                        pltpu.VMEM((hp, 1, tq), F32)],
        compiler_params=_params(("arbitrary", "arbitrary", "arbitrary"), vmem=VMEM_LIMIT_HIGH),
        name="mla_attn",
    )(q, c_kv, kr_pad, w_kv)


def _rot_cols(w):
    half = w.shape[-1] // 2
    return jnp.concatenate([-w[..., half:], w[..., :half]], axis=-1)


def _pad_lanes(w):
    pad = [(0, 0)] * (w.ndim - 1) + [(0, LANES - w.shape[-1])]
    return jnp.pad(w, pad)


def kernel(x, mem, positions, g_mix, g_ffn, g_final, w_in_a, g_v, w_sp, b_sp, g_kv, w_kv_a,
           g_kv_lat, w_in_b, g_q_lat, w_uq, w_uk, w_uv, g_mem, w_mem_kv, w_out, w_ffn_up,
           conv_w, conv_b, w_ffn_down):
    batch, seq, d_model = x.shape
    depth = g_mix.shape[0]
    n_a = w_in_a.shape[0]
    n_mem = mem.shape[1]
    T = batch * seq
    kv_rank = g_kv_lat.shape[0]
    q_rank = g_q_lat.shape[1]
    n_heads, nope_dim = w_uk.shape[2], w_uk.shape[3]
    rope_dim = w_kv_a.shape[1] - kv_rank
    mem_w = w_mem_kv.shape[2] // 2
    g_w = g_v.shape[1]
    scale = (nope_dim + rope_dim) ** -0.5 * np.log2(np.e)

    xs = x.reshape(T, d_model)
    mems = mem.reshape(batch * n_mem, d_model)

    inv = 1.0 / (ROPE_THETA ** (jnp.arange(0, rope_dim, 2, dtype=F32) / rope_dim))
    inv_row = _pad_lanes(jnp.concatenate([inv, inv])).reshape(1, LANES)
    cos_t, sin_t = rope_tables(positions.reshape(T, 1), inv_row)

    w_in_a_bf, w_in_b_bf, w_mem_kv_bf = (w.astype(BF16) for w in (w_in_a, w_in_b, w_mem_kv))
    w_out_bf, w_up_bf, w_down_bf = (w.astype(BF16) for w in (w_out, w_ffn_up, w_ffn_down))

    c_kv = kr_pad = None
    for l in range(depth):
        if l == n_a:
            w_kr = w_kv_a[:, kv_rank:]
            w_kv_cat = jnp.concatenate(
                [w_kv_a[:, :kv_rank], _pad_lanes(w_kr), _pad_lanes(_rot_cols(w_kr))], axis=1)
            c_kv, kr_pad = kv_side(xs, g_kv, w_kv_cat.astype(BF16), g_kv_lat, cos_t, sin_t)

        kvm = norm_matmul(mems, g_mem[l], w_mem_kv_bf, l, BF16,
                          tm=1024, name="mem_kv_proj")
        if l < n_a:
            z = norm_matmul(xs, g_mix[l], w_in_a_bf, l, BF16, tm=1024, name="in_proj_a",
                            gelu_cols=2 * g_w)
            b_full = jnp.broadcast_to(b_sp[l][:, :, None], w_sp[l].shape)
            xs = mix_out_gmlp(xs, z, g_v[l], w_sp[l], b_full, kvm, w_out_bf, l, batch)
        else:
            j = l - n_a
            wq = w_uq[j].reshape(q_rank, n_heads, nope_dim + rope_dim)
            wq_rope = wq[..., nope_dim:]
            w_main = jnp.concatenate(
                [wq[..., :nope_dim], _pad_lanes(wq_rope)], axis=-1).reshape(q_rank, -1)
            w_rot = _pad_lanes(_rot_cols(wq_rope)).reshape(q_rank, -1)
            q_mem, q = in_proj_q(xs, g_mix[l], w_in_b_bf, j, g_q_lat[j], w_main.astype(BF16),
                                 w_rot.astype(BF16), cos_t, sin_t, batch, n_heads, scale)
            w_kv_h = jnp.concatenate([w_uk[j], w_uv[j]], axis=-1)
            w_kv_h = jnp.transpose(w_kv_h, (1, 0, 2)).astype(BF16)
            main = mla_attn(q, c_kv, kr_pad, w_kv_h)
            xs = mix_out(xs, main, q_mem, 0, kvm, w_out_bf, l, batch)
        xs = conv_ffn(xs, g_ffn[l], w_up_bf,
                      _interleave_gate_value(conv_w[l], FFN_TF),
                      _interleave_gate_value(conv_b[l], FFN_TF),
                      w_down_bf, l, seq, tf=FFN_TF,
                      out_gain=g_final if l == depth - 1 else None)
    return xs.reshape(batch, seq, d_model)
```

```python
import functools

import jax
import jax.numpy as jnp
import numpy as np
from jax import lax
from jax.experimental import pallas as pl
from jax.experimental.pallas import tpu as pltpu

EPS = 1e-6
ROPE_THETA = 10000.0
MEM_HEADS = 4
CHUNK = 128
LANES = 128
SUBLANES = 8
CONV_W = 3
FFN_TF = 512
FFN_SKEW_TF = 256
VMEM_LIMIT = 56 * 1024 * 1024
VMEM_LIMIT_HIGH = 62 * 1024 * 1024

F32 = jnp.float32
BF16 = jnp.bfloat16


def _params(semantics, vmem=VMEM_LIMIT):
    return pltpu.CompilerParams(dimension_semantics=semantics, vmem_limit_bytes=vmem)


def _rms(x, g):
    ms = jnp.mean(x * x, axis=-1, keepdims=True)
    return x * lax.rsqrt(ms + EPS) * g


def _gelu(x):
    return 0.5 * x * (1.0 + lax.erf(x * np.float32(np.sqrt(0.5))))


def _rope_table_kernel(pos_ref, inv_ref, cos_ref, sin_ref):
    ang = pos_ref[...].astype(F32) * inv_ref[...]
    cos_ref[...] = jnp.cos(ang)
    sin_ref[...] = jnp.sin(ang)


def rope_tables(pos_col, inv_row, tm=2048):
    T = pos_col.shape[0]
    return pl.pallas_call(
        _rope_table_kernel,
        out_shape=(jax.ShapeDtypeStruct((T, LANES), F32),) * 2,
        grid=(T // tm,),
        in_specs=[pl.BlockSpec((tm, 1), lambda i: (i, 0)),
                  pl.BlockSpec((1, LANES), lambda i: (0, 0))],
        out_specs=(pl.BlockSpec((tm, LANES), lambda i: (i, 0)),) * 2,
        compiler_params=_params(("arbitrary",)),
        name="rope_tables",
    )(pos_col, inv_row)


def _norm_matmul_kernel(x_ref, g_ref, w_ref, o_ref, h_ref, *, rc, gelu_cols):
    n_chunks = x_ref.shape[0] // rc

    def norm(r):
        rows = slice(r * rc, (r + 1) * rc)
        h_ref[rows, :] = _rms(x_ref[rows, :], g_ref[...]).astype(h_ref.dtype)

    norm(0)
    for r in range(n_chunks):
        if r + 1 < n_chunks:
            norm(r + 1)
        rows = slice(r * rc, (r + 1) * rc)
        acc = jnp.dot(h_ref[rows, :], w_ref[...], preferred_element_type=F32)
        if gelu_cols:
            acc = jnp.concatenate([_gelu(acc[:, :gelu_cols]), acc[:, gelu_cols:]], axis=1)
        o_ref[rows, :] = acc.astype(o_ref.dtype)


def norm_matmul(x, g, w, layer, out_dtype, tm, name, rc=256, gelu_cols=0):
    T, K = x.shape
    N = w.shape[2]
    tm = min(tm, T)
    return pl.pallas_call(
        functools.partial(_norm_matmul_kernel, rc=rc, gelu_cols=gelu_cols),
        out_shape=jax.ShapeDtypeStruct((T, N), out_dtype),
        grid=(T // tm,),
        in_specs=[pl.BlockSpec((tm, K), lambda i: (i, 0)),
                  pl.BlockSpec((1, K), lambda i: (0, 0)),
                  pl.BlockSpec((None, K, N), lambda i: (layer, 0, 0),
                               pipeline_mode=pl.Buffered(1))],
        out_specs=pl.BlockSpec((tm, N), lambda i: (i, 0)),
        scratch_shapes=[pltpu.VMEM((tm, K), BF16)],
        compiler_params=_params(("arbitrary",), vmem=VMEM_LIMIT_HIGH),
        name=name,
    )(x, g.reshape(1, K), w)


def _kv_side_kernel(x_ref, g_ref, w_ref, glat_ref, cos_ref, sin_ref, ckv_ref, kr_ref, h_ref,
                    *, rc):
    n_chunks = x_ref.shape[0] // rc
    rank = glat_ref.shape[1]

    def norm(r):
        rows = slice(r * rc, (r + 1) * rc)
        h_ref[rows, :] = _rms(x_ref[rows, :], g_ref[...]).astype(h_ref.dtype)

    norm(0)
    for r in range(n_chunks):
        if r + 1 < n_chunks:
            norm(r + 1)
        rows = slice(r * rc, (r + 1) * rc)
        kv = jnp.dot(h_ref[rows, :], w_ref[...], preferred_element_type=F32)
        ckv_ref[rows, :] = _rms(kv[:, :rank], glat_ref[...]).astype(ckv_ref.dtype)
        kr = kv[:, rank:rank + LANES] * cos_ref[rows, :] + kv[:, rank + LANES:] * sin_ref[rows, :]
        kr_ref[rows, :] = kr.astype(kr_ref.dtype)


def kv_side(x, g, w_kv, g_lat, cos_t, sin_t, tm=1024, rc=256):
    T, D = x.shape
    N = w_kv.shape[1]
    rank = N - 2 * LANES
    return pl.pallas_call(
        functools.partial(_kv_side_kernel, rc=rc),
        out_shape=(jax.ShapeDtypeStruct((T, rank), BF16),
                   jax.ShapeDtypeStruct((T, LANES), BF16)),
        grid=(T // tm,),
        in_specs=[pl.BlockSpec((tm, D), lambda i: (i, 0)),
                  pl.BlockSpec((1, D), lambda i: (0, 0)),
                  pl.BlockSpec((D, N), lambda i: (0, 0)),
                  pl.BlockSpec((1, rank), lambda i: (0, 0)),
                  pl.BlockSpec((tm, LANES), lambda i: (i, 0)),
                  pl.BlockSpec((tm, LANES), lambda i: (i, 0))],
        out_specs=(pl.BlockSpec((tm, rank), lambda i: (i, 0)),
                   pl.BlockSpec((tm, LANES), lambda i: (i, 0))),
        scratch_shapes=[pltpu.VMEM((tm, D), BF16)],
        compiler_params=_params(("arbitrary",)),
        name="kv_side",
    )(x, g.reshape(1, D), w_kv, g_lat.reshape(1, rank), cos_t, sin_t)


def _mix_out_gmlp_kernel(x_ref, u_ref, v_ref, q_ref, kv_ref, gv_ref, wsp_ref, b_ref, w_ref,
                         o_ref, mix_ref, *, n_groups, n_heads, rc):
    tm = x_ref.shape[0]
    gw = n_groups * CHUNK
    width = q_ref.shape[1]
    head_dim = width // n_heads
    scale = np.float32(head_dim ** -0.5)
    row = lax.broadcasted_iota(jnp.int32, (CHUNK, CHUNK), 0)
    col = lax.broadcasted_iota(jnp.int32, (CHUNK, CHUNK), 1)
    causal = row >= col
    w_tril = [jnp.where(causal, wsp_ref[g], 0.0).astype(BF16) for g in range(n_groups)]

    def mixer(r):
        for c in range(r * rc // CHUNK, (r + 1) * rc // CHUNK):
            rows = slice(c * CHUNK, (c + 1) * CHUNK)
            ug = u_ref[rows, :].astype(F32)
            vn = _rms(v_ref[rows, :].astype(F32), gv_ref[...]).astype(BF16)
            for g in range(n_groups):
                cols = slice(g * CHUNK, (g + 1) * CHUNK)
                sv = jnp.dot(w_tril[g], vn[:, cols], preferred_element_type=F32) + b_ref[g]
                mix_ref[rows, cols] = (ug[:, cols] * sv).astype(BF16)
        rows = slice(r * rc, (r + 1) * rc)
        for h in range(n_heads):
            cols = slice(h * head_dim, (h + 1) * head_dim)
            k = kv_ref[:, cols]
            v = kv_ref[:, width + h * head_dim:width + (h + 1) * head_dim]
            s = lax.dot_general(q_ref[rows, cols], k, (((1,), (1,)), ((), ())),
                                preferred_element_type=F32) * scale
            m = jnp.max(s, axis=-1, keepdims=True)
            p = jnp.exp(s - m)
            l = jnp.sum(p, axis=-1, keepdims=True)
            o = jnp.dot(p.astype(BF16), v, preferred_element_type=F32) / l
            mix_ref[rows, gw + h * head_dim:gw + (h + 1) * head_dim] = o.astype(BF16)

    n_chunks = tm // rc
    mixer(0)
    for r in range(n_chunks):
        if r + 1 < n_chunks:
            mixer(r + 1)
        rows = slice(r * rc, (r + 1) * rc)
        o_ref[rows, :] = x_ref[rows, :] + jnp.dot(mix_ref[rows, :], w_ref[...],
                                                  preferred_element_type=F32)


def mix_out_gmlp(x, z, g_v, w_sp, b_full, kvm, w_out, layer, batch, tm=512, rc=256):
    T, D = x.shape
    n_groups = w_sp.shape[0]
    gw = n_groups * CHUNK
    n_mem = kvm.shape[0] // batch
    width = kvm.shape[1] // 2
    per_b = T // batch // tm
    kern = functools.partial(_mix_out_gmlp_kernel, n_groups=n_groups, n_heads=MEM_HEADS, rc=rc)
    return pl.pallas_call(
        kern,
        out_shape=jax.ShapeDtypeStruct((T, D), F32),
        grid=(batch, per_b),
        in_specs=[pl.BlockSpec((tm, D), lambda b, i: (b * per_b + i, 0)),
                  pl.BlockSpec((tm, gw), lambda b, i: (b * per_b + i, 0)),
                  pl.BlockSpec((tm, gw), lambda b, i: (b * per_b + i, 1)),
                  pl.BlockSpec((tm, width), lambda b, i: (b * per_b + i, 2 * gw // width)),
                  pl.BlockSpec((n_mem, 2 * width), lambda b, i: (b, 0)),
                  pl.BlockSpec((1, gw), lambda b, i: (0, 0)),
                  pl.BlockSpec((n_groups, CHUNK, CHUNK), lambda b, i: (0, 0, 0)),
                  pl.BlockSpec((n_groups, CHUNK, CHUNK), lambda b, i: (0, 0, 0)),
                  pl.BlockSpec((None, gw + width, D), lambda b, i: (layer, 0, 0))],
        out_specs=pl.BlockSpec((tm, D), lambda b, i: (b * per_b + i, 0)),
        scratch_shapes=[pltpu.VMEM((tm, gw + width), BF16)],
        compiler_params=_params(("arbitrary", "arbitrary")),
        name="mix_out_gmlp",
    )(x, z, z, z, kvm, g_v.reshape(1, gw), w_sp, b_full, w_out)


def _mix_out_kernel(x_ref, a_ref, q_ref, kv_ref, w_ref, o_ref, *, n_heads):
    ka = a_ref.shape[1]
    width = q_ref.shape[1]
    head_dim = width // n_heads
    scale = np.float32(head_dim ** -0.5)
    acc = jnp.dot(a_ref[...], w_ref[:ka, :], preferred_element_type=F32)
    mo = []
    for h in range(n_heads):
        cols = slice(h * head_dim, (h + 1) * head_dim)
        k = kv_ref[:, cols]
        v = kv_ref[:, width + h * head_dim:width + (h + 1) * head_dim]
        s = lax.dot_general(q_ref[:, cols], k, (((1,), (1,)), ((), ())),
                            preferred_element_type=F32) * scale
        m = jnp.max(s, axis=-1, keepdims=True)
        p = jnp.exp(s - m)
        l = jnp.sum(p, axis=-1, keepdims=True)
        o = jnp.dot(p.astype(BF16), v, preferred_element_type=F32) / l
        mo.append(o.astype(BF16))
    acc = acc + jnp.dot(jnp.concatenate(mo, axis=1), w_ref[ka:, :], preferred_element_type=F32)
    o_ref[...] = x_ref[...] + acc


def mix_out(x, main, z, q_col_block, kvm, w_out, layer, batch, tm=512):
    T, D = x.shape
    ka = main.shape[1]
    n_mem = kvm.shape[0] // batch
    width = kvm.shape[1] // 2
    per_b = T // batch // tm
    return pl.pallas_call(
        functools.partial(_mix_out_kernel, n_heads=MEM_HEADS),
        out_shape=jax.ShapeDtypeStruct((T, D), F32),
        grid=(batch, per_b),
        in_specs=[pl.BlockSpec((tm, D), lambda b, i: (b * per_b + i, 0)),
                  pl.BlockSpec((tm, ka), lambda b, i: (b * per_b + i, 0)),
                  pl.BlockSpec((tm, width), lambda b, i: (b * per_b + i, q_col_block)),
                  pl.BlockSpec((n_mem, 2 * width), lambda b, i: (b, 0)),
                  pl.BlockSpec((None, ka + width, D), lambda b, i: (layer, 0, 0))],
        out_specs=pl.BlockSpec((tm, D), lambda b, i: (b * per_b + i, 0)),
        compiler_params=_params(("arbitrary", "arbitrary")),
        name="mix_out",
    )(x, main, z, kvm, w_out)


def _conv_ffn_kernel(x_ref, g_ref, wg_ref, wv_ref, cw_ref, cb_ref, wd_ref, o_ref,
                     h_ref, a_ref, carry_ref, *, tm, rc, tf, tiles_per_seq, norm_out):
    i = pl.program_id(0)
    j = pl.program_id(1)

    @pl.when(jnp.logical_and(i == 0, j == 0))
    def _():
        carry_ref[...] = jnp.zeros(carry_ref.shape, F32)

    @pl.when(j == 0)
    def _():
        for r in range(tm // rc):
            rows = slice(r * rc, (r + 1) * rc)
            x = x_ref[rows, :]
            h_ref[rows, :] = _rms(x, g_ref[0:1, :]).astype(h_ref.dtype)
            o_ref[rows, :] = x

    first_in_seq = (i % tiles_per_seq) == 0
    a_ref[:SUBLANES, :] = jnp.where(first_in_seq, 0.0, carry_ref[j])
    cw = cw_ref[...]
    cb = cb_ref[...]
    def up(r):
        lo = SUBLANES + r * rc
        h = h_ref[r * rc:(r + 1) * rc, :]
        a_ref[lo:lo + rc, :tf] = jnp.dot(h, wg_ref[...], preferred_element_type=F32)
        a_ref[lo:lo + rc, tf:] = jnp.dot(h, wv_ref[...], preferred_element_type=F32)

    def gate_down(r):
        lo = SUBLANES + r * rc
        c = cb + a_ref[lo:lo + rc, :] * cw[CONV_W - 1:CONV_W, :]
        for k in range(CONV_W - 1):
            lag = CONV_W - 1 - k
            c = c + a_ref[lo - lag:lo - lag + rc, :] * cw[k:k + 1, :]
        cg = c[:, :tf]
        gated = (cg * jax.nn.sigmoid(cg) * c[:, tf:]).astype(BF16)
        o_ref[r * rc:(r + 1) * rc, :] += jnp.dot(gated, wd_ref[...],
                                                 preferred_element_type=F32)

    n_chunks = tm // rc
    up(0)
    for r in range(n_chunks):
        if r + 1 < n_chunks:
            up(r + 1)
        gate_down(r)
    carry_ref[j] = a_ref[tm:, :]

    if norm_out:
        @pl.when(j == pl.num_programs(1) - 1)
        def _():
            for r in range(n_chunks):
                rows = slice(r * rc, (r + 1) * rc)
                o_ref[rows, :] = _rms(o_ref[rows, :], g_ref[1:2, :])


def conv_ffn(x, g, w_up, conv_w, conv_b, w_down, layer, seq, out_gain=None,
             tm=1024, tf=512, rc=512):
    T, D = x.shape
    d_ff = w_down.shape[1]
    nff = d_ff // tf
    gains = g.reshape(1, D) if out_gain is None else jnp.stack([g, out_gain])
    kern = functools.partial(_conv_ffn_kernel, tm=tm, rc=rc, tf=tf, tiles_per_seq=seq // tm,
                             norm_out=out_gain is not None)
    return pl.pallas_call(
        kern,
        out_shape=jax.ShapeDtypeStruct((T, D), F32),
        grid=(T // tm, nff),
        in_specs=[pl.BlockSpec((tm, D), lambda i, j: (i, 0)),
                  pl.BlockSpec(gains.shape, lambda i, j: (0, 0)),
                  pl.BlockSpec((None, D, tf), lambda i, j: (layer, 0, j)),
                  pl.BlockSpec((None, D, tf), lambda i, j: (layer, 0, nff + j)),
                  pl.BlockSpec((CONV_W, 2 * tf), lambda i, j: (0, j)),
                  pl.BlockSpec((1, 2 * tf), lambda i, j: (0, j)),
                  pl.BlockSpec((None, tf, D), lambda i, j: (layer, j, 0))],
        out_specs=pl.BlockSpec((tm, D), lambda i, j: (i, 0)),
        scratch_shapes=[pltpu.VMEM((tm, D), BF16),
                        pltpu.VMEM((tm + SUBLANES, 2 * tf), F32),
                        pltpu.VMEM((nff, SUBLANES, 2 * tf), F32)],
        compiler_params=_params(("arbitrary", "arbitrary"), vmem=VMEM_LIMIT_HIGH),
        name="conv_ffn",
    )(x, gains, w_up, w_up, conv_w, conv_b.reshape(1, -1), w_down)


def _conv_ffn_skew_kernel(x_ref, g_ref, wg_ref, wv_ref, cw_ref, cb_ref, cwp_ref, cbp_ref,
                          wdp_ref, wd0_ref, wd1_ref, o_ref, h_ref, aa_ref, ab_ref, carry_ref,
                          *, tm, tf, tiles_per_seq, norm_out):
    i = pl.program_id(0)
    s = pl.program_id(1)
    last = pl.num_programs(1) - 1
    rc = tm // 2

    @pl.when(jnp.logical_and(i == 0, s == 0))
    def _():
        carry_ref[...] = jnp.zeros(carry_ref.shape, F32)
        ab_ref[...] = jnp.zeros(ab_ref.shape, F32)

    @pl.when(s == 0)
    def _():
        for r in range(tm // rc):
            rows = slice(r * rc, (r + 1) * rc)
            x = x_ref[rows, :]
            h_ref[rows, :] = _rms(x, g_ref[0:1, :]).astype(h_ref.dtype)
            o_ref[rows, :] = x

    first_in_seq = (i % tiles_per_seq) == 0

    def up(t_local, a_ref):
        t = 2 * s + t_local
        cols = slice(t_local * tf, (t_local + 1) * tf)
        a_ref[:SUBLANES, :] = jnp.where(first_in_seq, 0.0, carry_ref[t])
        for r in range(tm // rc):
            lo = SUBLANES + r * rc
            h = h_ref[r * rc:(r + 1) * rc, :]
            a_ref[lo:lo + rc, :tf] = jnp.dot(h, wg_ref[:, cols], preferred_element_type=F32)
            a_ref[lo:lo + rc, tf:] = jnp.dot(h, wv_ref[:, cols], preferred_element_type=F32)
        carry_ref[t] = a_ref[tm:, :]

    def gate(a_ref, r, cw, cb):
        lo = SUBLANES + r * rc
        c = cb + a_ref[lo:lo + rc, :] * cw[CONV_W - 1:CONV_W, :]
        for k in range(CONV_W - 1):
            lag = CONV_W - 1 - k
            c = c + a_ref[lo - lag:lo - lag + rc, :] * cw[k:k + 1, :]
        cg = c[:, :tf]
        return cg * jax.nn.sigmoid(cg) * c[:, tf:]

    cw = cw_ref[...]
    cb = cb_ref[...]
    keep = jnp.where(s == 0, 0.0, 1.0)
    n_chunks = tm // rc
    up(0, aa_ref)
    g_prev = [(gate(ab_ref, r, cwp_ref[...], cbp_ref[...]) * keep).astype(BF16)
              for r in range(n_chunks)]
    up(1, ab_ref)
    for r in range(n_chunks):
        g_cur = gate(aa_ref, r, cw[:, :2 * tf], cb[:, :2 * tf]).astype(BF16)
        o_ref[r * rc:(r + 1) * rc, :] += (
            jnp.dot(g_prev[r], wdp_ref[...], preferred_element_type=F32)
            + jnp.dot(g_cur, wd0_ref[...], preferred_element_type=F32))

    @pl.when(s == last)
    def _():
        for r in range(n_chunks):
            g_last = gate(ab_ref, r, cw[:, 2 * tf:], cb[:, 2 * tf:]).astype(BF16)
            o_ref[r * rc:(r + 1) * rc, :] += jnp.dot(g_last, wd1_ref[...],
                                                     preferred_element_type=F32)
        if norm_out:
            for r in range(tm // rc):
                rows = slice(r * rc, (r + 1) * rc)
                o_ref[rows, :] = _rms(o_ref[rows, :], g_ref[1:2, :])


def conv_ffn_skew(x, g, w_up, conv_w, conv_b, w_down, layer, seq, out_gain=None,
                  tm=1024, tf=256):
    T, D = x.shape
    d_ff = w_down.shape[1]
    n_s = d_ff // (2 * tf)
    gains = g.reshape(1, D) if out_gain is None else jnp.stack([g, out_gain])
    kern = functools.partial(_conv_ffn_skew_kernel, tm=tm, tf=tf, tiles_per_seq=seq // tm,
                             norm_out=out_gain is not None)

    def prev(s):
        return jnp.maximum(2 * s - 1, 0)

    return pl.pallas_call(
        kern,
        out_shape=jax.ShapeDtypeStruct((T, D), F32),
        grid=(T // tm, n_s),
        in_specs=[pl.BlockSpec((tm, D), lambda i, s: (i, 0)),
                  pl.BlockSpec(gains.shape, lambda i, s: (0, 0)),
                  pl.BlockSpec((None, D, 2 * tf), lambda i, s: (layer, 0, s)),
                  pl.BlockSpec((None, D, 2 * tf), lambda i, s: (layer, 0, n_s + s)),
                  pl.BlockSpec((CONV_W, 4 * tf), lambda i, s: (0, s)),
                  pl.BlockSpec((1, 4 * tf), lambda i, s: (0, s)),
                  pl.BlockSpec((CONV_W, 2 * tf), lambda i, s: (0, prev(s))),
                  pl.BlockSpec((1, 2 * tf), lambda i, s: (0, prev(s))),
                  pl.BlockSpec((None, tf, D), lambda i, s: (layer, prev(s), 0)),
                  pl.BlockSpec((None, tf, D), lambda i, s: (layer, 2 * s, 0)),
                  pl.BlockSpec((None, tf, D), lambda i, s: (layer, 2 * s + 1, 0))],
        out_specs=pl.BlockSpec((tm, D), lambda i, s: (i, 0)),
        scratch_shapes=[pltpu.VMEM((tm, D), BF16),
                        pltpu.VMEM((tm + SUBLANES, 2 * tf), F32),
                        pltpu.VMEM((tm + SUBLANES, 2 * tf), F32),
                        pltpu.VMEM((d_ff // tf, SUBLANES, 2 * tf), F32)],
        compiler_params=_params(("arbitrary", "arbitrary"), vmem=VMEM_LIMIT_HIGH),
        name="conv_ffn_skew",
    )(x, gains, w_up, w_up, conv_w, conv_b.reshape(1, -1), conv_w, conv_b.reshape(1, -1),
      w_down, w_down, w_down)


def _interleave_gate_value(w, tf):
    lead = w.shape[:-1]
    d_ff = w.shape[-1] // 2
    w = w.reshape(*lead, 2, d_ff // tf, tf)
    return jnp.swapaxes(w, -3, -2).reshape(*lead, 2 * d_ff)


def _in_proj_q_kernel(x_ref, g_ref, w_ref, gq_ref, wm_ref, wr_ref, cos_ref, sin_ref,
                      qm_ref, o_ref, h_ref, *, n_heads, scale, rc):
    n_chunks = x_ref.shape[0] // rc
    rank = gq_ref.shape[1]

    def norm(r):
        rows = slice(r * rc, (r + 1) * rc)
        h_ref[rows, :] = _rms(x_ref[rows, :], g_ref[...]).astype(h_ref.dtype)

    norm(0)
    for r in range(n_chunks):
        if r + 1 < n_chunks:
            norm(r + 1)
        rows = slice(r * rc, (r + 1) * rc)
        z = jnp.dot(h_ref[rows, :], w_ref[...], preferred_element_type=F32)
        qm_ref[rows, :] = z[:, rank:].astype(qm_ref.dtype)
        qn = _rms(z[:, :rank], gq_ref[...]).astype(BF16)
        a = jnp.dot(qn, wm_ref[...], preferred_element_type=F32)
        rot = jnp.dot(qn, wr_ref[...], preferred_element_type=F32)
        cos = cos_ref[rows, :]
        sin = sin_ref[rows, :]
        for h in range(n_heads):
            base = 2 * LANES * h
            o_ref[0, h, rows, :LANES] = (a[:, base:base + LANES] * scale).astype(o_ref.dtype)
            rope = (a[:, base + LANES:base + 2 * LANES] * cos
                    + rot[:, h * LANES:(h + 1) * LANES] * sin)
            o_ref[0, h, rows, LANES:] = (rope * scale).astype(o_ref.dtype)


def in_proj_q(x, g, w_in, layer, g_q, w_main, w_rot, cos_t, sin_t, batch, n_heads, scale,
              tm=512, rc=256):
    T, D = x.shape
    seq = T // batch
    rank = w_main.shape[0]
    n_in = w_in.shape[2]
    per_b = seq // tm
    kern = functools.partial(_in_proj_q_kernel, n_heads=n_heads, scale=np.float32(scale), rc=rc)
    return pl.pallas_call(
        kern,
        out_shape=(jax.ShapeDtypeStruct((T, n_in - rank), BF16),
                   jax.ShapeDtypeStruct((batch, n_heads, seq, 2 * LANES), BF16)),
        grid=(batch, per_b),
        in_specs=[pl.BlockSpec((tm, D), lambda b, i: (b * per_b + i, 0)),
                  pl.BlockSpec((1, D), lambda b, i: (0, 0)),
                  pl.BlockSpec((None, D, n_in), lambda b, i: (layer, 0, 0)),
                  pl.BlockSpec((1, rank), lambda b, i: (0, 0)),
                  pl.BlockSpec(w_main.shape, lambda b, i: (0, 0)),
                  pl.BlockSpec(w_rot.shape, lambda b, i: (0, 0)),
                  pl.BlockSpec((tm, LANES), lambda b, i: (b * per_b + i, 0)),
                  pl.BlockSpec((tm, LANES), lambda b, i: (b * per_b + i, 0))],
        out_specs=(pl.BlockSpec((tm, n_in - rank), lambda b, i: (b * per_b + i, 0)),
                   pl.BlockSpec((1, n_heads, tm, 2 * LANES), lambda b, i: (b, 0, i, 0))),
        scratch_shapes=[pltpu.VMEM((tm, D), BF16)],
        compiler_params=_params(("arbitrary", "arbitrary")),
        name="in_proj_q",
    )(x, g.reshape(1, D), w_in, g_q.reshape(1, rank), w_main, w_rot, cos_t, sin_t)


def _mla_attn_kernel(q_ref, ckv_ref, kr_ref, wkv_ref, o_ref, k_ref, vt_ref, acc_ref,
                     sa_ref, sb_ref, ma_ref, mb_ref, *, tq, tk, hp, v_dim):
    i = pl.program_id(2)
    n_kb = k_ref.shape[1]

    @pl.when(i == 0)
    def _():
        ones = jnp.ones((vt_ref.shape[2] - v_dim, tk), vt_ref.dtype)
        for jb in range(n_kb):
            rows = slice(jb * tk, (jb + 1) * tk)
            for hh in range(hp):
                kv = jnp.dot(ckv_ref[rows, :], wkv_ref[hh], preferred_element_type=F32)
                k_ref[hh, jb, :, :LANES] = kv[:, :LANES].astype(k_ref.dtype)
                k_ref[hh, jb, :, LANES:] = kr_ref[rows, :]
                vt_ref[hh, jb, :v_dim, :] = kv[:, LANES:].T.astype(vt_ref.dtype)
                vt_ref[hh, jb, v_dim:, :] = ones

    acc_ref[...] = jnp.zeros(acc_ref.shape, F32)

    def scores(jb, dst_ref, mdst_ref, q_lo=0):
        for hh in range(hp):
            st = lax.dot_general(k_ref[hh, jb], q_ref[0, hh, q_lo:, :], (((1,), (1,)), ((), ())),
                                 preferred_element_type=F32)
            dst_ref[hh, :, q_lo:] = st
            mdst_ref[hh, :, q_lo:] = jnp.max(st, axis=0, keepdims=True)

    def consume(jb, src_ref, msrc_ref, m_all, diag=False, q_lo=0):
        out = []
        for hh in range(hp):
            m = m_all[hh][:, q_lo:]
            st = src_ref[hh, :, q_lo:]
            if diag:
                k_pos = lax.broadcasted_iota(jnp.int32, st.shape, 0)
                q_pos = lax.broadcasted_iota(jnp.int32, st.shape, 1)
                st = jnp.where(k_pos <= q_pos, st, -jnp.inf)
                m_blk = jnp.max(st, axis=0, keepdims=True)
            else:
                m_blk = msrc_ref[hh, :, q_lo:]
            m_new = jnp.maximum(m, m_blk)
            alpha = jnp.exp2(m - m_new)
            p = jnp.exp2(st - m_new).astype(BF16)
            pv = jnp.dot(vt_ref[hh, jb], p, preferred_element_type=F32)
            acc_ref[hh, :, q_lo:] = alpha * acc_ref[hh, :, q_lo:] + pv
            if q_lo:
                m_new = jnp.concatenate([m_all[hh][:, :q_lo], m_new], axis=1)
            out.append(m_new)
        return tuple(out)

    def pair(t, m_all):
        scores(2 * t + 1, sb_ref, mb_ref)
        m_all = consume(2 * t, sa_ref, ma_ref, m_all)
        scores(2 * t + 2, sa_ref, ma_ref)
        return consume(2 * t + 1, sb_ref, mb_ref, m_all)

    scores(0, sa_ref, ma_ref)
    m_all = lax.fori_loop(0, i, pair, tuple(jnp.full((1, tq), -jnp.inf, F32) for _ in range(hp)))
    scores(2 * i + 1, sb_ref, mb_ref, q_lo=tk)
    m_all = consume(2 * i, sa_ref, ma_ref, m_all, diag=True)
    consume(2 * i + 1, sb_ref, mb_ref, m_all, diag=True, q_lo=tk)
    for hh in range(hp):
        o = acc_ref[hh, :v_dim, :] / acc_ref[hh, v_dim:v_dim + 1, :]
        o_ref[:, hh * v_dim:(hh + 1) * v_dim] = o.T.astype(o_ref.dtype)


def mla_attn(q, c_kv, kr_pad, w_kv, tq=1024, hp=4):
    tk = tq // 2
    batch, n_heads, seq, qk_dim = q.shape
    rank = c_kv.shape[1]
    v_dim = w_kv.shape[2] - LANES
    v_rows = v_dim + 2 * SUBLANES
    nq = seq // tq
    return pl.pallas_call(
        functools.partial(_mla_attn_kernel, tq=tq, tk=tk, hp=hp, v_dim=v_dim),
        out_shape=jax.ShapeDtypeStruct((batch * seq, n_heads * v_dim), BF16),
        grid=(batch, n_heads // hp, nq),
        in_specs=[pl.BlockSpec((1, hp, tq, qk_dim), lambda b, h, i: (b, h, i, 0)),
                  pl.BlockSpec((seq, rank), lambda b, h, i: (b, 0)),
                  pl.BlockSpec((seq, LANES), lambda b, h, i: (b, 0)),
                  pl.BlockSpec((hp, rank, LANES + v_dim), lambda b, h, i: (h, 0, 0))],
        out_specs=pl.BlockSpec((tq, hp * v_dim), lambda b, h, i: (b * nq + i, h)),
        scratch_shapes=[pltpu.VMEM((hp, seq // tk, tk, qk_dim), BF16),
                        pltpu.VMEM((hp, seq // tk, v_rows, tk), BF16),
                        pltpu.VMEM((hp, v_rows, tq), F32),
                        pltpu.VMEM((hp, tk, tq), F32),
                        pltpu.VMEM((hp, tk, tq), F32),
                        pltpu.VMEM((hp, 1, tq), F32),
                        pltpu.VMEM((hp, 1, tq), F32)],
        compiler_params=_params(("arbitrary", "arbitrary", "arbitrary"), vmem=VMEM_LIMIT_HIGH),
        name="mla_attn",
    )(q, c_kv, kr_pad, w_kv)


def _rot_cols(w):
    half = w.shape[-1] // 2
    return jnp.concatenate([-w[..., half:], w[..., :half]], axis=-1)


def _pad_lanes(w):
    pad = [(0, 0)] * (w.ndim - 1) + [(0, LANES - w.shape[-1])]
    return jnp.pad(w, pad)


def kernel(x, mem, positions, g_mix, g_ffn, g_final, w_in_a, g_v, w_sp, b_sp, g_kv, w_kv_a,
           g_kv_lat, w_in_b, g_q_lat, w_uq, w_uk, w_uv, g_mem, w_mem_kv, w_out, w_ffn_up,
           conv_w, conv_b, w_ffn_down):
    batch, seq, d_model = x.shape
    depth = g_mix.shape[0]
    n_a = w_in_a.shape[0]
    n_mem = mem.shape[1]
    T = batch * seq
    kv_rank = g_kv_lat.shape[0]
    q_rank = g_q_lat.shape[1]
    n_heads, nope_dim = w_uk.shape[2], w_uk.shape[3]
    rope_dim = w_kv_a.shape[1] - kv_rank
    mem_w = w_mem_kv.shape[2] // 2
    g_w = g_v.shape[1]
    scale = (nope_dim + rope_dim) ** -0.5 * np.log2(np.e)

    xs = x.reshape(T, d_model)
    mems = mem.reshape(batch * n_mem, d_model)

    inv = 1.0 / (ROPE_THETA ** (jnp.arange(0, rope_dim, 2, dtype=F32) / rope_dim))
    inv_row = _pad_lanes(jnp.concatenate([inv, inv])).reshape(1, LANES)
    cos_t, sin_t = rope_tables(positions.reshape(T, 1), inv_row)

    w_in_a_bf, w_in_b_bf, w_mem_kv_bf = (w.astype(BF16) for w in (w_in_a, w_in_b, w_mem_kv))
    w_out_bf, w_up_bf, w_down_bf = (w.astype(BF16) for w in (w_out, w_ffn_up, w_ffn_down))

    c_kv = kr_pad = None
    for l in range(depth):
        if l == n_a:
            w_kr = w_kv_a[:, kv_rank:]
            w_kv_cat = jnp.concatenate(
                [w_kv_a[:, :kv_rank], _pad_lanes(w_kr), _pad_lanes(_rot_cols(w_kr))], axis=1)
            c_kv, kr_pad = kv_side(xs, g_kv, w_kv_cat.astype(BF16), g_kv_lat, cos_t, sin_t)

        kvm = norm_matmul(mems, g_mem[l], w_mem_kv_bf, l, BF16,
                          tm=1024, name="mem_kv_proj")
        if l < n_a:
            z = norm_matmul(xs, g_mix[l], w_in_a_bf, l, BF16, tm=1024, name="in_proj_a",
                            gelu_cols=2 * g_w)
            b_full = jnp.broadcast_to(b_sp[l][:, :, None], w_sp[l].shape)
            xs = mix_out_gmlp(xs, z, g_v[l], w_sp[l], b_full, kvm, w_out_bf, l, batch)
        else:
            j = l - n_a
            wq = w_uq[j].reshape(q_rank, n_heads, nope_dim + rope_dim)
            wq_rope = wq[..., nope_dim:]
            w_main = jnp.concatenate(
                [wq[..., :nope_dim], _pad_lanes(wq_rope)], axis=-1).reshape(q_rank, -1)
            w_rot = _pad_lanes(_rot_cols(wq_rope)).reshape(q_rank, -1)
            q_mem, q = in_proj_q(xs, g_mix[l], w_in_b_bf, j, g_q_lat[j], w_main.astype(BF16),
                                 w_rot.astype(BF16), cos_t, sin_t, batch, n_heads, scale)
            w_kv_h = jnp.concatenate([w_uk[j], w_uv[j]], axis=-1)
            w_kv_h = jnp.transpose(w_kv_h, (1, 0, 2)).astype(BF16)
            main = mla_attn(q, c_kv, kr_pad, w_kv_h)
            xs = mix_out(xs, main, q_mem, 0, kvm, w_out_bf, l, batch)
        xs = conv_ffn_skew(xs, g_ffn[l], w_up_bf,
                           _interleave_gate_value(conv_w[l], FFN_SKEW_TF),
                           _interleave_gate_value(conv_b[l], FFN_SKEW_TF),
                           w_down_bf, l, seq, tf=FFN_SKEW_TF,
                           out_gain=g_final if l == depth - 1 else None)
    return xs.reshape(batch, seq, d_model)
```

```python
import functools

import jax
import jax.numpy as jnp
import numpy as np
from jax import lax
from jax.experimental import pallas as pl
from jax.experimental.pallas import tpu as pltpu

EPS = 1e-6
ROPE_THETA = 10000.0
MEM_HEADS = 4
CHUNK = 128
LANES = 128
SUBLANES = 8
CONV_W = 3
FFN_TF = 512
VMEM_LIMIT = 56 * 1024 * 1024
VMEM_LIMIT_HIGH = 62 * 1024 * 1024

F32 = jnp.float32
BF16 = jnp.bfloat16


def _params(semantics, vmem=VMEM_LIMIT):
    return pltpu.CompilerParams(dimension_semantics=semantics, vmem_limit_bytes=vmem)


def _rms(x, g):
    ms = jnp.mean(x * x, axis=-1, keepdims=True)
    return x * lax.rsqrt(ms + EPS) * g


def _gelu(x):
    return 0.5 * x * (1.0 + lax.erf(x * np.float32(np.sqrt(0.5))))


def _rope_table_kernel(pos_ref, inv_ref, cos_ref, sin_ref):
    ang = pos_ref[...].astype(F32) * inv_ref[...]
    cos_ref[...] = jnp.cos(ang)
    sin_ref[...] = jnp.sin(ang)


def rope_tables(pos_col, inv_row, tm=2048):
    T = pos_col.shape[0]
    return pl.pallas_call(
        _rope_table_kernel,
        out_shape=(jax.ShapeDtypeStruct((T, LANES), F32),) * 2,
        grid=(T // tm,),
        in_specs=[pl.BlockSpec((tm, 1), lambda i: (i, 0)),
                  pl.BlockSpec((1, LANES), lambda i: (0, 0))],
        out_specs=(pl.BlockSpec((tm, LANES), lambda i: (i, 0)),) * 2,
        compiler_params=_params(("arbitrary",)),
        name="rope_tables",
    )(pos_col, inv_row)


def _norm_matmul_kernel(x_ref, g_ref, w_ref, o_ref, h_ref, *, rc, gelu_cols):
    n_chunks = x_ref.shape[0] // rc

    def norm(r):
        rows = slice(r * rc, (r + 1) * rc)
        h_ref[rows, :] = _rms(x_ref[rows, :], g_ref[...]).astype(h_ref.dtype)

    norm(0)
    for r in range(n_chunks):
        if r + 1 < n_chunks:
            norm(r + 1)
        rows = slice(r * rc, (r + 1) * rc)
        acc = jnp.dot(h_ref[rows, :], w_ref[...], preferred_element_type=F32)
        if gelu_cols:
            acc = jnp.concatenate([_gelu(acc[:, :gelu_cols]), acc[:, gelu_cols:]], axis=1)
        o_ref[rows, :] = acc.astype(o_ref.dtype)


def norm_matmul(x, g, w, layer, out_dtype, tm, name, rc=256, gelu_cols=0):
    T, K = x.shape
    N = w.shape[2]
    tm = min(tm, T)
    return pl.pallas_call(
        functools.partial(_norm_matmul_kernel, rc=rc, gelu_cols=gelu_cols),
        out_shape=jax.ShapeDtypeStruct((T, N), out_dtype),
        grid=(T // tm,),
        in_specs=[pl.BlockSpec((tm, K), lambda i: (i, 0)),
                  pl.BlockSpec((1, K), lambda i: (0, 0)),
                  pl.BlockSpec((None, K, N), lambda i: (layer, 0, 0),
                               pipeline_mode=pl.Buffered(1))],
        out_specs=pl.BlockSpec((tm, N), lambda i: (i, 0)),
        scratch_shapes=[pltpu.VMEM((tm, K), BF16)],
        compiler_params=_params(("arbitrary",), vmem=VMEM_LIMIT_HIGH),
        name=name,
    )(x, g.reshape(1, K), w)


def _kv_side_kernel(x_ref, g_ref, w_ref, glat_ref, cos_ref, sin_ref, ckv_ref, kr_ref, h_ref,
                    *, rc):
    n_chunks = x_ref.shape[0] // rc
    rank = glat_ref.shape[1]

    def norm(r):
        rows = slice(r * rc, (r + 1) * rc)
        h_ref[rows, :] = _rms(x_ref[rows, :], g_ref[...]).astype(h_ref.dtype)

    norm(0)
    for r in range(n_chunks):
        if r + 1 < n_chunks:
            norm(r + 1)
        rows = slice(r * rc, (r + 1) * rc)
        kv = jnp.dot(h_ref[rows, :], w_ref[...], preferred_element_type=F32)
        ckv_ref[rows, :] = _rms(kv[:, :rank], glat_ref[...]).astype(ckv_ref.dtype)
        kr = kv[:, rank:rank + LANES] * cos_ref[rows, :] + kv[:, rank + LANES:] * sin_ref[rows, :]
        kr_ref[rows, :] = kr.astype(kr_ref.dtype)


def kv_side(x, g, w_kv, g_lat, cos_t, sin_t, tm=1024, rc=256):
    T, D = x.shape
    N = w_kv.shape[1]
    rank = N - 2 * LANES
    return pl.pallas_call(
        functools.partial(_kv_side_kernel, rc=rc),
        out_shape=(jax.ShapeDtypeStruct((T, rank), BF16),
                   jax.ShapeDtypeStruct((T, LANES), BF16)),
        grid=(T // tm,),
        in_specs=[pl.BlockSpec((tm, D), lambda i: (i, 0)),
                  pl.BlockSpec((1, D), lambda i: (0, 0)),
                  pl.BlockSpec((D, N), lambda i: (0, 0)),
                  pl.BlockSpec((1, rank), lambda i: (0, 0)),
                  pl.BlockSpec((tm, LANES), lambda i: (i, 0)),
                  pl.BlockSpec((tm, LANES), lambda i: (i, 0))],
        out_specs=(pl.BlockSpec((tm, rank), lambda i: (i, 0)),
                   pl.BlockSpec((tm, LANES), lambda i: (i, 0))),
        scratch_shapes=[pltpu.VMEM((tm, D), BF16)],
        compiler_params=_params(("arbitrary",)),
        name="kv_side",
    )(x, g.reshape(1, D), w_kv, g_lat.reshape(1, rank), cos_t, sin_t)


def _mix_out_gmlp_kernel(x_ref, u_ref, v_ref, q_ref, kv_ref, gv_ref, wsp_ref, b_ref, w_ref,
                         o_ref, mix_ref, *, n_groups, n_heads, rc):
    tm = x_ref.shape[0]
    gw = n_groups * CHUNK
    width = q_ref.shape[1]
    head_dim = width // n_heads
    scale = np.float32(head_dim ** -0.5)
    row = lax.broadcasted_iota(jnp.int32, (CHUNK, CHUNK), 0)
    col = lax.broadcasted_iota(jnp.int32, (CHUNK, CHUNK), 1)
    causal = row >= col
    w_tril = [jnp.where(causal, wsp_ref[g], 0.0).astype(BF16) for g in range(n_groups)]

    def mixer(r):
        for c in range(r * rc // CHUNK, (r + 1) * rc // CHUNK):
            rows = slice(c * CHUNK, (c + 1) * CHUNK)
            ug = u_ref[rows, :].astype(F32)
            vn = _rms(v_ref[rows, :].astype(F32), gv_ref[...]).astype(BF16)
            for g in range(n_groups):
                cols = slice(g * CHUNK, (g + 1) * CHUNK)
                sv = jnp.dot(w_tril[g], vn[:, cols], preferred_element_type=F32) + b_ref[g]
                mix_ref[rows, cols] = (ug[:, cols] * sv).astype(BF16)
        rows = slice(r * rc, (r + 1) * rc)
        for h in range(n_heads):
            cols = slice(h * head_dim, (h + 1) * head_dim)
            k = kv_ref[:, cols]
            v = kv_ref[:, width + h * head_dim:width + (h + 1) * head_dim]
            s = lax.dot_general(q_ref[rows, cols], k, (((1,), (1,)), ((), ())),
                                preferred_element_type=F32) * scale
            m = jnp.max(s, axis=-1, keepdims=True)
            p = jnp.exp(s - m)
            l = jnp.sum(p, axis=-1, keepdims=True)
            o = jnp.dot(p.astype(BF16), v, preferred_element_type=F32) / l
            mix_ref[rows, gw + h * head_dim:gw + (h + 1) * head_dim] = o.astype(BF16)

    n_chunks = tm // rc
    mixer(0)
    for r in range(n_chunks):
        if r + 1 < n_chunks:
            mixer(r + 1)
        rows = slice(r * rc, (r + 1) * rc)
        o_ref[rows, :] = x_ref[rows, :] + jnp.dot(mix_ref[rows, :], w_ref[...],
                                                  preferred_element_type=F32)


def mix_out_gmlp(x, z, g_v, w_sp, b_full, kvm, w_out, layer, batch, tm=512, rc=256):
    T, D = x.shape
    n_groups = w_sp.shape[0]
    gw = n_groups * CHUNK
    n_mem = kvm.shape[0] // batch
    width = kvm.shape[1] // 2
    per_b = T // batch // tm
    kern = functools.partial(_mix_out_gmlp_kernel, n_groups=n_groups, n_heads=MEM_HEADS, rc=rc)
    return pl.pallas_call(
        kern,
        out_shape=jax.ShapeDtypeStruct((T, D), F32),
        grid=(batch, per_b),
        in_specs=[pl.BlockSpec((tm, D), lambda b, i: (b * per_b + i, 0)),
                  pl.BlockSpec((tm, gw), lambda b, i: (b * per_b + i, 0)),
                  pl.BlockSpec((tm, gw), lambda b, i: (b * per_b + i, 1)),
                  pl.BlockSpec((tm, width), lambda b, i: (b * per_b + i, 2 * gw // width)),
                  pl.BlockSpec((n_mem, 2 * width), lambda b, i: (b, 0)),
                  pl.BlockSpec((1, gw), lambda b, i: (0, 0)),
                  pl.BlockSpec((n_groups, CHUNK, CHUNK), lambda b, i: (0, 0, 0)),
                  pl.BlockSpec((n_groups, CHUNK, CHUNK), lambda b, i: (0, 0, 0)),
                  pl.BlockSpec((None, gw + width, D), lambda b, i: (layer, 0, 0))],
        out_specs=pl.BlockSpec((tm, D), lambda b, i: (b * per_b + i, 0)),
        scratch_shapes=[pltpu.VMEM((tm, gw + width), BF16)],
        compiler_params=_params(("arbitrary", "arbitrary")),
        name="mix_out_gmlp",
    )(x, z, z, z, kvm, g_v.reshape(1, gw), w_sp, b_full, w_out)


def _mix_out_kernel(x_ref, a_ref, q_ref, kv_ref, w_ref, o_ref, *, n_heads):
    ka = a_ref.shape[1]
    width = q_ref.shape[1]
    head_dim = width // n_heads
    scale = np.float32(head_dim ** -0.5)
    acc = jnp.dot(a_ref[...], w_ref[:ka, :], preferred_element_type=F32)
    mo = []
    for h in range(n_heads):
        cols = slice(h * head_dim, (h + 1) * head_dim)
        k = kv_ref[:, cols]
        v = kv_ref[:, width + h * head_dim:width + (h + 1) * head_dim]
        s = lax.dot_general(q_ref[:, cols], k, (((1,), (1,)), ((), ())),
                            preferred_element_type=F32) * scale
        m = jnp.max(s, axis=-1, keepdims=True)
        p = jnp.exp(s - m)
        l = jnp.sum(p, axis=-1, keepdims=True)
        o = jnp.dot(p.astype(BF16), v, preferred_element_type=F32) / l
        mo.append(o.astype(BF16))
    acc = acc + jnp.dot(jnp.concatenate(mo, axis=1), w_ref[ka:, :], preferred_element_type=F32)
    o_ref[...] = x_ref[...] + acc


def mix_out(x, main, z, q_col_block, kvm, w_out, layer, batch, tm=1024):
    T, D = x.shape
    ka = main.shape[1]
    n_mem = kvm.shape[0] // batch
    width = kvm.shape[1] // 2
    per_b = T // batch // tm
    return pl.pallas_call(
        functools.partial(_mix_out_kernel, n_heads=MEM_HEADS),
        out_shape=jax.ShapeDtypeStruct((T, D), F32),
        grid=(batch, per_b),
        in_specs=[pl.BlockSpec((tm, D), lambda b, i: (b * per_b + i, 0)),
                  pl.BlockSpec((tm, ka), lambda b, i: (b * per_b + i, 0)),
                  pl.BlockSpec((tm, width), lambda b, i: (b * per_b + i, q_col_block)),
                  pl.BlockSpec((n_mem, 2 * width), lambda b, i: (b, 0)),
                  pl.BlockSpec((None, ka + width, D), lambda b, i: (layer, 0, 0),
                               pipeline_mode=pl.Buffered(1))],
        out_specs=pl.BlockSpec((tm, D), lambda b, i: (b * per_b + i, 0)),
        compiler_params=_params(("arbitrary", "arbitrary"), vmem=VMEM_LIMIT_HIGH),
        name="mix_out",
    )(x, main, z, kvm, w_out)


def _conv_ffn_kernel(x_ref, g_ref, wg_ref, wv_ref, cw_ref, cb_ref, wd_ref, o_ref,
                     h_ref, a_ref, carry_ref, *, tm, rc, tf, tiles_per_seq, norm_out):
    i = pl.program_id(0)
    j = pl.program_id(1)

    @pl.when(jnp.logical_and(i == 0, j == 0))
    def _():
        carry_ref[...] = jnp.zeros(carry_ref.shape, F32)

    @pl.when(j == 0)
    def _():
        for r in range(tm // rc):
            rows = slice(r * rc, (r + 1) * rc)
            x = x_ref[rows, :]
            h_ref[rows, :] = _rms(x, g_ref[0:1, :]).astype(h_ref.dtype)
            o_ref[rows, :] = x

    first_in_seq = (i % tiles_per_seq) == 0
    a_ref[:SUBLANES, :] = jnp.where(first_in_seq, 0.0, carry_ref[j])
    cw = cw_ref[...]
    cb = cb_ref[...]
    def up(r):
        lo = SUBLANES + r * rc
        h = h_ref[r * rc:(r + 1) * rc, :]
        a_ref[lo:lo + rc, :tf] = jnp.dot(h, wg_ref[...], preferred_element_type=F32)
        a_ref[lo:lo + rc, tf:] = jnp.dot(h, wv_ref[...], preferred_element_type=F32)

    def gate_down(r):
        lo = SUBLANES + r * rc
        c = cb + a_ref[lo:lo + rc, :] * cw[CONV_W - 1:CONV_W, :]
        for k in range(CONV_W - 1):
            lag = CONV_W - 1 - k
            c = c + a_ref[lo - lag:lo - lag + rc, :] * cw[k:k + 1, :]
        cg = c[:, :tf]
        gated = (cg * jax.nn.sigmoid(cg) * c[:, tf:]).astype(BF16)
        o_ref[r * rc:(r + 1) * rc, :] += jnp.dot(gated, wd_ref[...],
                                                 preferred_element_type=F32)

    n_chunks = tm // rc
    up(0)
    for r in range(n_chunks):
        if r + 1 < n_chunks:
            up(r + 1)
        gate_down(r)
    carry_ref[j] = a_ref[tm:, :]

    if norm_out:
        @pl.when(j == pl.num_programs(1) - 1)
        def _():
            for r in range(n_chunks):
                rows = slice(r * rc, (r + 1) * rc)
                o_ref[rows, :] = _rms(o_ref[rows, :], g_ref[1:2, :])


def conv_ffn(x, g, w_up, conv_w, conv_b, w_down, layer, seq, out_gain=None,
             tm=1024, tf=512, rc=512):
    T, D = x.shape
    d_ff = w_down.shape[1]
    nff = d_ff // tf
    gains = g.reshape(1, D) if out_gain is None else jnp.stack([g, out_gain])
    kern = functools.partial(_conv_ffn_kernel, tm=tm, rc=rc, tf=tf, tiles_per_seq=seq // tm,
                             norm_out=out_gain is not None)
    return pl.pallas_call(
        kern,
        out_shape=jax.ShapeDtypeStruct((T, D), F32),
        grid=(T // tm, nff),
        in_specs=[pl.BlockSpec((tm, D), lambda i, j: (i, 0)),
                  pl.BlockSpec(gains.shape, lambda i, j: (0, 0)),
                  pl.BlockSpec((None, D, tf), lambda i, j: (layer, 0, j)),
                  pl.BlockSpec((None, D, tf), lambda i, j: (layer, 0, nff + j)),
                  pl.BlockSpec((CONV_W, 2 * tf), lambda i, j: (0, j)),
                  pl.BlockSpec((1, 2 * tf), lambda i, j: (0, j)),
                  pl.BlockSpec((None, tf, D), lambda i, j: (layer, j, 0))],
        out_specs=pl.BlockSpec((tm, D), lambda i, j: (i, 0)),
        scratch_shapes=[pltpu.VMEM((tm, D), BF16),
                        pltpu.VMEM((tm + SUBLANES, 2 * tf), F32),
                        pltpu.VMEM((nff, SUBLANES, 2 * tf), F32)],
        compiler_params=_params(("arbitrary", "arbitrary"), vmem=VMEM_LIMIT_HIGH),
        name="conv_ffn",
    )(x, gains, w_up, w_up, conv_w, conv_b.reshape(1, -1), w_down)


def _interleave_gate_value(w, tf):
    lead = w.shape[:-1]
    d_ff = w.shape[-1] // 2
    w = w.reshape(*lead, 2, d_ff // tf, tf)
    return jnp.swapaxes(w, -3, -2).reshape(*lead, 2 * d_ff)


def _in_proj_q_kernel(x_ref, g_ref, w_ref, gq_ref, wm_ref, wr_ref, cos_ref, sin_ref,
                      qm_ref, o_ref, h_ref, *, n_heads, scale, rc):
    n_chunks = x_ref.shape[0] // rc
    rank = gq_ref.shape[1]

    def norm(r):
        rows = slice(r * rc, (r + 1) * rc)
        h_ref[rows, :] = _rms(x_ref[rows, :], g_ref[...]).astype(h_ref.dtype)

    norm(0)
    for r in range(n_chunks):
        if r + 1 < n_chunks:
            norm(r + 1)
        rows = slice(r * rc, (r + 1) * rc)
        z = jnp.dot(h_ref[rows, :], w_ref[...], preferred_element_type=F32)
        qm_ref[rows, :] = z[:, rank:].astype(qm_ref.dtype)
        qn = _rms(z[:, :rank], gq_ref[...]).astype(BF16)
        a = jnp.dot(qn, wm_ref[...], preferred_element_type=F32)
        rot = jnp.dot(qn, wr_ref[...], preferred_element_type=F32)
        cos = cos_ref[rows, :]
        sin = sin_ref[rows, :]
        for h in range(n_heads):
            base = 2 * LANES * h
            o_ref[0, h, rows, :LANES] = (a[:, base:base + LANES] * scale).astype(o_ref.dtype)
            rope = (a[:, base + LANES:base + 2 * LANES] * cos
                    + rot[:, h * LANES:(h + 1) * LANES] * sin)
            o_ref[0, h, rows, LANES:] = (rope * scale).astype(o_ref.dtype)


def in_proj_q(x, g, w_in, layer, g_q, w_main, w_rot, cos_t, sin_t, batch, n_heads, scale,
              tm=512, rc=256):
    T, D = x.shape
    seq = T // batch
    rank = w_main.shape[0]
    n_in = w_in.shape[2]
    per_b = seq // tm
    kern = functools.partial(_in_proj_q_kernel, n_heads=n_heads, scale=np.float32(scale), rc=rc)
    return pl.pallas_call(
        kern,
        out_shape=(jax.ShapeDtypeStruct((T, n_in - rank), BF16),
                   jax.ShapeDtypeStruct((batch, n_heads, seq, 2 * LANES), BF16)),
        grid=(batch, per_b),
        in_specs=[pl.BlockSpec((tm, D), lambda b, i: (b * per_b + i, 0)),
                  pl.BlockSpec((1, D), lambda b, i: (0, 0)),
                  pl.BlockSpec((None, D, n_in), lambda b, i: (layer, 0, 0)),
                  pl.BlockSpec((1, rank), lambda b, i: (0, 0)),
                  pl.BlockSpec(w_main.shape, lambda b, i: (0, 0)),
                  pl.BlockSpec(w_rot.shape, lambda b, i: (0, 0)),
                  pl.BlockSpec((tm, LANES), lambda b, i: (b * per_b + i, 0)),
                  pl.BlockSpec((tm, LANES), lambda b, i: (b * per_b + i, 0))],
        out_specs=(pl.BlockSpec((tm, n_in - rank), lambda b, i: (b * per_b + i, 0)),
                   pl.BlockSpec((1, n_heads, tm, 2 * LANES), lambda b, i: (b, 0, i, 0))),
        scratch_shapes=[pltpu.VMEM((tm, D), BF16)],
        compiler_params=_params(("arbitrary", "arbitrary")),
        name="in_proj_q",
    )(x, g.reshape(1, D), w_in, g_q.reshape(1, rank), w_main, w_rot, cos_t, sin_t)


def _mla_attn_kernel(q_ref, ckv_ref, kr_ref, wkv_ref, o_ref, k_ref, vt_ref, acc_ref,
                     sa_ref, sb_ref, ma_ref, mb_ref, *, tq, tk, hp, v_dim):
    i = pl.program_id(2)
    n_kb = k_ref.shape[1]

    @pl.when(i == 0)
    def _():
        ones = jnp.ones((vt_ref.shape[2] - v_dim, tk), vt_ref.dtype)
        for jb in range(n_kb):
            rows = slice(jb * tk, (jb + 1) * tk)
            for hh in range(hp):
                kv = jnp.dot(ckv_ref[rows, :], wkv_ref[hh], preferred_element_type=F32)
                k_ref[hh, jb, :, :LANES] = kv[:, :LANES].astype(k_ref.dtype)
                k_ref[hh, jb, :, LANES:] = kr_ref[rows, :]
                vt_ref[hh, jb, :v_dim, :] = kv[:, LANES:].T.astype(vt_ref.dtype)
                vt_ref[hh, jb, v_dim:, :] = ones

    acc_ref[...] = jnp.zeros(acc_ref.shape, F32)

    def scores(jb, dst_ref, mdst_ref, q_lo=0):
        for hh in range(hp):
            st = lax.dot_general(k_ref[hh, jb], q_ref[0, hh, q_lo:, :], (((1,), (1,)), ((), ())),
                                 preferred_element_type=F32)
            dst_ref[hh, :, q_lo:] = st
            mdst_ref[hh, :, q_lo:] = jnp.max(st, axis=0, keepdims=True)

    def consume(jb, src_ref, msrc_ref, m_all, diag=False, q_lo=0):
        out = []
        for hh in range(hp):
            m = m_all[hh][:, q_lo:]
            st = src_ref[hh, :, q_lo:]
            if diag:
                k_pos = lax.broadcasted_iota(jnp.int32, st.shape, 0)
                q_pos = lax.broadcasted_iota(jnp.int32, st.shape, 1)
                st = jnp.where(k_pos <= q_pos, st, -jnp.inf)
                m_blk = jnp.max(st, axis=0, keepdims=True)
            else:
                m_blk = msrc_ref[hh, :, q_lo:]
            m_new = jnp.maximum(m, m_blk)
            alpha = jnp.exp2(m - m_new)
            p = jnp.exp2(st - m_new).astype(BF16)
            pv = jnp.dot(vt_ref[hh, jb], p, preferred_element_type=F32)
            acc_ref[hh, :, q_lo:] = alpha * acc_ref[hh, :, q_lo:] + pv
            if q_lo:
                m_new = jnp.concatenate([m_all[hh][:, :q_lo], m_new], axis=1)
            out.append(m_new)
        return tuple(out)

    def pair(t, m_all):
        scores(2 * t + 1, sb_ref, mb_ref)
        m_all = consume(2 * t, sa_ref, ma_ref, m_all)
        scores(2 * t + 2, sa_ref, ma_ref)
        return consume(2 * t + 1, sb_ref, mb_ref, m_all)

    scores(0, sa_ref, ma_ref)
    m_all = lax.fori_loop(0, i, pair, tuple(jnp.full((1, tq), -jnp.inf, F32) for _ in range(hp)))
    scores(2 * i + 1, sb_ref, mb_ref, q_lo=tk)
    m_all = consume(2 * i, sa_ref, ma_ref, m_all, diag=True)
    consume(2 * i + 1, sb_ref, mb_ref, m_all, diag=True, q_lo=tk)
    for hh in range(hp):
        o = acc_ref[hh, :v_dim, :] / acc_ref[hh, v_dim:v_dim + 1, :]
        o_ref[:, hh * v_dim:(hh + 1) * v_dim] = o.T.astype(o_ref.dtype)


def mla_attn(q, c_kv, kr_pad, w_kv, tq=1024, hp=4):
    tk = tq // 2
    batch, n_heads, seq, qk_dim = q.shape
    rank = c_kv.shape[1]
    v_dim = w_kv.shape[2] - LANES
    v_rows = v_dim + 2 * SUBLANES
    nq = seq // tq
    return pl.pallas_call(
        functools.partial(_mla_attn_kernel, tq=tq, tk=tk, hp=hp, v_dim=v_dim),
        out_shape=jax.ShapeDtypeStruct((batch * seq, n_heads * v_dim), BF16),
        grid=(batch, n_heads // hp, nq),
        in_specs=[pl.BlockSpec((1, hp, tq, qk_dim), lambda b, h, i: (b, h, i, 0)),
                  pl.BlockSpec((seq, rank), lambda b, h, i: (b, 0)),
                  pl.BlockSpec((seq, LANES), lambda b, h, i: (b, 0)),
                  pl.BlockSpec((hp, rank, LANES + v_dim), lambda b, h, i: (h, 0, 0))],
        out_specs=pl.BlockSpec((tq, hp * v_dim), lambda b, h, i: (b * nq + i, h)),
        scratch_shapes=[pltpu.VMEM((hp, seq // tk, tk, qk_dim), BF16),
                        pltpu.VMEM((hp, seq // tk, v_rows, tk), BF16),
                        pltpu.VMEM((hp, v_rows, tq), F32),
                        pltpu.VMEM((hp, tk, tq), F32),
                        pltpu.VMEM((hp, tk, tq), F32),
                        pltpu.VMEM((hp, 1, tq), F32),
                        pltpu.VMEM((hp, 1, tq), F32)],
        compiler_params=_params(("arbitrary", "arbitrary", "arbitrary"), vmem=VMEM_LIMIT_HIGH),
        name="mla_attn",
    )(q, c_kv, kr_pad, w_kv)


def _rot_cols(w):
    half = w.shape[-1] // 2
    return jnp.concatenate([-w[..., half:], w[..., :half]], axis=-1)


def _pad_lanes(w):
    pad = [(0, 0)] * (w.ndim - 1) + [(0, LANES - w.shape[-1])]
    return jnp.pad(w, pad)


def kernel(x, mem, positions, g_mix, g_ffn, g_final, w_in_a, g_v, w_sp, b_sp, g_kv, w_kv_a,
           g_kv_lat, w_in_b, g_q_lat, w_uq, w_uk, w_uv, g_mem, w_mem_kv, w_out, w_ffn_up,
           conv_w, conv_b, w_ffn_down):
    batch, seq, d_model = x.shape
    depth = g_mix.shape[0]
    n_a = w_in_a.shape[0]
    n_mem = mem.shape[1]
    T = batch * seq
    kv_rank = g_kv_lat.shape[0]
    q_rank = g_q_lat.shape[1]
    n_heads, nope_dim = w_uk.shape[2], w_uk.shape[3]
    rope_dim = w_kv_a.shape[1] - kv_rank
    mem_w = w_mem_kv.shape[2] // 2
    g_w = g_v.shape[1]
    scale = (nope_dim + rope_dim) ** -0.5 * np.log2(np.e)

    xs = x.reshape(T, d_model)
    mems = mem.reshape(batch * n_mem, d_model)

    inv = 1.0 / (ROPE_THETA ** (jnp.arange(0, rope_dim, 2, dtype=F32) / rope_dim))
    inv_row = _pad_lanes(jnp.concatenate([inv, inv])).reshape(1, LANES)
    cos_t, sin_t = rope_tables(positions.reshape(T, 1), inv_row)

    w_in_a_bf, w_in_b_bf, w_mem_kv_bf = (w.astype(BF16) for w in (w_in_a, w_in_b, w_mem_kv))
    w_out_bf, w_up_bf, w_down_bf = (w.astype(BF16) for w in (w_out, w_ffn_up, w_ffn_down))

    c_kv = kr_pad = None
    for l in range(depth):
        if l == n_a:
            w_kr = w_kv_a[:, kv_rank:]
            w_kv_cat = jnp.concatenate(
                [w_kv_a[:, :kv_rank], _pad_lanes(w_kr), _pad_lanes(_rot_cols(w_kr))], axis=1)
            c_kv, kr_pad = kv_side(xs, g_kv, w_kv_cat.astype(BF16), g_kv_lat, cos_t, sin_t)

        kvm = norm_matmul(mems, g_mem[l], w_mem_kv_bf, l, BF16,
                          tm=1024, name="mem_kv_proj")
        if l < n_a:
            z = norm_matmul(xs, g_mix[l], w_in_a_bf, l, BF16, tm=1024, name="in_proj_a",
                            gelu_cols=2 * g_w)
            b_full = jnp.broadcast_to(b_sp[l][:, :, None], w_sp[l].shape)
            xs = mix_out_gmlp(xs, z, g_v[l], w_sp[l], b_full, kvm, w_out_bf, l, batch)
        else:
            j = l - n_a
            wq = w_uq[j].reshape(q_rank, n_heads, nope_dim + rope_dim)
            wq_rope = wq[..., nope_dim:]
            w_main = jnp.concatenate(
                [wq[..., :nope_dim], _pad_lanes(wq_rope)], axis=-1).reshape(q_rank, -1)
            w_rot = _pad_lanes(_rot_cols(wq_rope)).reshape(q_rank, -1)
            q_mem, q = in_proj_q(xs, g_mix[l], w_in_b_bf, j, g_q_lat[j], w_main.astype(BF16),
                                 w_rot.astype(BF16), cos_t, sin_t, batch, n_heads, scale)
            w_kv_h = jnp.concatenate([w_uk[j], w_uv[j]], axis=-1)
            w_kv_h = jnp.transpose(w_kv_h, (1, 0, 2)).astype(BF16)
            main = mla_attn(q, c_kv, kr_pad, w_kv_h)
            xs = mix_out(xs, main, q_mem, 0, kvm, w_out_bf, l, batch)
        xs = conv_ffn(xs, g_ffn[l], w_up_bf,
                      _interleave_gate_value(conv_w[l], FFN_TF),
                      _interleave_gate_value(conv_b[l], FFN_TF),
                      w_down_bf, l, seq, tf=FFN_TF,
                      out_gain=g_final if l == depth - 1 else None)
    return xs.reshape(batch, seq, d_model)
```
